```python
import math
import jax, jax.numpy as jnp
from jax import lax
import numpy as np

D_MODEL = 1024
BATCH = 8
SEQ = 8192
DEPTH = 2

MEM_LEN = 256
RMS_EPS = 1e-6
POOL_GROUPS = 4
POOL_GROUP_W = D_MODEL // 16
POOL_W = POOL_GROUPS * POOL_GROUP_W
POOL_WINDOWS = (2, 4, 8, 16)
MOBA_HEADS = 4
MOBA_HD = D_MODEL // 16
MOBA_W = MOBA_HEADS * MOBA_HD
MOBA_BLOCK = 256
MOBA_TOPK = 3
MOBA_Q_CHUNK = 16
ROPE_THETA = 500000.0
ROPE_DIMS = MOBA_HD // 4
DN_HEADS = 8
DN_HD = D_MODEL // 16
DN_W = DN_HEADS * DN_HD
DN_CONV = 4
DN_CHUNK = 64
N_BRANCH = 3
OFF_POOL = 0
OFF_MOBA = OFF_POOL + POOL_W
OFF_DN_QKV = OFF_MOBA + 3 * MOBA_W
OFF_DN_BETA = OFF_DN_QKV + 3 * DN_W
OFF_DN_ALPHA = OFF_DN_BETA + DN_HEADS
OFF_DN_Z = OFF_DN_ALPHA + DN_HEADS
OFF_GATE = OFF_DN_Z + DN_W
IN_COLS = OFF_GATE + N_BRANCH * D_MODEL
XA_HEADS = 4
XA_HD = D_MODEL // 8
XA_W = XA_HEADS * XA_HD
DENSE_FF = 2816
N_EXPERTS = 8
MOE_TOPK = 2
EXPERT_FF = DENSE_FF // 2
N_DENSE = (DEPTH + 1) // 2
N_MOE = DEPTH // 2

kernel_name = "hybrid_pool_moba_gdn_moe_block"


def rmsnorm(x, g):
    x32 = x.astype(jnp.float32)
    y = x32 * lax.rsqrt(jnp.mean(x32 * x32, axis=-1, keepdims=True) + RMS_EPS)
    return (y * g.astype(jnp.float32)).astype(x.dtype)


def l2norm(x):
    x32 = x.astype(jnp.float32)
    return x32 * lax.rsqrt(jnp.sum(x32 * x32, axis=-1, keepdims=True) + RMS_EPS)


def partial_rope(x, positions):
    half = ROPE_DIMS // 2
    inv_freq = jnp.power(ROPE_THETA, -jnp.arange(half, dtype=jnp.float32) * 2.0 / ROPE_DIMS)
    ang = positions.astype(jnp.float32)[:, :, None] * inv_freq
    cos = jnp.cos(ang)[:, :, None, :].astype(x.dtype)
    sin = jnp.sin(ang)[:, :, None, :].astype(x.dtype)
    x1 = x[..., :half]
    x2 = x[..., half:ROPE_DIMS]
    return jnp.concatenate([x1 * cos - x2 * sin, x2 * cos + x1 * sin, x[..., ROPE_DIMS:]], axis=-1)


def pool_mixer(p, pool_w, pool_scale):
    B, S, _ = p.shape
    pg = p.reshape(B, S, POOL_GROUPS, POOL_GROUP_W).astype(jnp.float32)
    cs = jnp.cumsum(pg, axis=1)
    t1 = jnp.arange(1, S + 1, dtype=jnp.float32)
    outs = []
    for gi, w in enumerate(POOL_WINDOWS):
        c = cs[:, :, gi]
        lagged = jnp.pad(c, ((0, 0), (w, 0), (0, 0)))[:, :S]
        count = jnp.minimum(t1, float(w))[None, :, None]
        outs.append((c - lagged) / count - pg[:, :, gi])
    pooled = jnp.stack(outs, axis=2).astype(p.dtype)
    mixed = jnp.einsum('bsgc,gcd->bsgd', pooled, pool_w)
    return mixed.reshape(B, S, POOL_W) * pool_scale


def moba_attention(q, k, v):
    B, S, H, hd = q.shape
    nb = -(-S // MOBA_BLOCK)
    s_pad = nb * MOBA_BLOCK
    topk = min(MOBA_TOPK, nb)
    pad = ((0, 0), (0, 0), (0, s_pad - S), (0, 0))
    qh = jnp.pad(q.transpose(0, 2, 1, 3), pad)
    kb = jnp.pad(k.transpose(0, 2, 1, 3), pad).reshape(B, H, nb, MOBA_BLOCK, hd)
    vb = jnp.pad(v.transpose(0, 2, 1, 3), pad).reshape(B, H, nb, MOBA_BLOCK, hd)
    kbar = jnp.mean(kb.astype(jnp.float32), axis=3)
    scale = hd ** -0.5
    bi = jnp.arange(B)[:, None, None, None]
    hi = jnp.arange(H)[None, :, None, None]
    blk_ids = jnp.arange(nb)

    def chunk(ci):
        start = ci * MOBA_Q_CHUNK
        c = start // MOBA_BLOCK
        qc = lax.dynamic_slice_in_dim(qh, start, MOBA_Q_CHUNK, axis=2)
        gate = jnp.einsum('bhqd,bhnd->bhqn', qc.astype(jnp.float32), kbar)
        gate = jnp.where(blk_ids < c, gate, -jnp.inf)
        _, idx = lax.top_k(gate, topk)
        valid = idx < c
        kg = kb[bi, hi, idx]
        vg = vb[bi, hi, idx]
        s_sel = jnp.einsum('bhqd,bhqknd->bhqkn', qc, kg).astype(jnp.float32) * scale
        s_sel = jnp.where(valid[..., None], s_sel, -jnp.inf)
        s_sel = s_sel.reshape(B, H, MOBA_Q_CHUNK, topk * MOBA_BLOCK)
        k_own = lax.dynamic_index_in_dim(kb, c, axis=2, keepdims=False)
        v_own = lax.dynamic_index_in_dim(vb, c, axis=2, keepdims=False)
        s_own = jnp.einsum('bhqd,bhnd->bhqn', qc, k_own).astype(jnp.float32) * scale
        qpos = start + jnp.arange(MOBA_Q_CHUNK)
        kpos = c * MOBA_BLOCK + jnp.arange(MOBA_BLOCK)
        s_own = jnp.where(kpos[None, :] <= qpos[:, None], s_own, -jnp.inf)
        p = jax.nn.softmax(jnp.concatenate([s_sel, s_own], axis=-1), axis=-1).astype(q.dtype)
        p_sel = p[..., :topk * MOBA_BLOCK].reshape(B, H, MOBA_Q_CHUNK, topk, MOBA_BLOCK)
        p_own = p[..., topk * MOBA_BLOCK:]
        return (jnp.einsum('bhqkn,bhqknd->bhqd', p_sel, vg)
                + jnp.einsum('bhqn,bhnd->bhqd', p_own, v_own))

    out = lax.map(chunk, jnp.arange(s_pad // MOBA_Q_CHUNK))
    out = out.transpose(1, 0, 3, 2, 4).reshape(B, s_pad, H, hd)[:, :S]
    return out.reshape(B, S, H * hd)


def short_conv(x, w):
    S = x.shape[1]
    K = w.shape[0]
    xp = jnp.pad(x, ((0, 0), (K - 1, 0), (0, 0)))
    y = w[0] * xp[:, :S]
    for j in range(1, K):
        y = y + w[j] * xp[:, j:j + S]
    return y


def gated_delta_rule(q, k, v, g, beta):
    B, H, S, dk = q.shape
    dv = v.shape[-1]
    C = DN_CHUNK
    nc = S // C

    def chunks(t):
        return t.reshape(t.shape[:2] + (nc, C) + t.shape[3:])

    q, k, v, g, beta = chunks(q), chunks(k), chunks(v), chunks(g), chunks(beta)
    G = jnp.cumsum(g, axis=-1)
    tril = jnp.tril(jnp.ones((C, C), dtype=bool))
    strict = jnp.tril(jnp.ones((C, C), dtype=bool), -1)
    decay = jnp.exp(jnp.where(tril, G[..., :, None] - G[..., None, :], -jnp.inf))
    kbeta = k * beta[..., None]
    a = jnp.where(strict, jnp.einsum('bhnid,bhnjd->bhnij', kbeta, k) * decay, 0.0)
    lhs = a + jnp.eye(C, dtype=jnp.float32)
    rhs = jnp.concatenate([v * beta[..., None], kbeta * jnp.exp(G)[..., None]], axis=-1)
    sol = lax.linalg.triangular_solve(lhs, rhs, left_side=True, lower=True, unit_diagonal=True)
    u = sol[..., :dv]
    w = sol[..., dv:]
    aqk = jnp.einsum('bhnid,bhnjd->bhnij', q, k) * decay
    q_dec = q * jnp.exp(G)[..., None]
    g_last = G[..., -1]
    k_dec = k * jnp.exp(g_last[..., None] - G)[..., None]

    def step(state, xs):
        u_c, w_c, qd_c, aqk_c, kd_c, gl_c = xs
        v_new = u_c - jnp.einsum('bhcd,bhde->bhce', w_c, state)
        o_c = (jnp.einsum('bhcd,bhde->bhce', qd_c, state)
               + jnp.einsum('bhij,bhje->bhie', aqk_c, v_new))
        state = state * jnp.exp(gl_c)[..., None, None] + jnp.einsum('bhcd,bhce->bhde', kd_c, v_new)
        return state, o_c

    xs = (jnp.moveaxis(u, 2, 0), jnp.moveaxis(w, 2, 0), jnp.moveaxis(q_dec, 2, 0),
          jnp.moveaxis(aqk, 2, 0), jnp.moveaxis(k_dec, 2, 0), jnp.moveaxis(g_last, 2, 0))
    _, o = lax.scan(step, jnp.zeros((B, H, dk, dv), jnp.float32), xs)
    return jnp.moveaxis(o, 0, 2).reshape(B, H, S, dv)


def deltanet_mixer(qkv, b_raw, a_raw, z, conv_w, a_log, dt_bias, norm_g):
    B, S, _ = qkv.shape
    qkv = jax.nn.silu(short_conv(qkv, conv_w))
    qkv = qkv.reshape(B, S, 3, DN_HEADS, DN_HD).transpose(2, 0, 3, 1, 4)
    q = l2norm(qkv[0]) * (DN_HD ** -0.5)
    k = l2norm(qkv[1])
    v = qkv[2].astype(jnp.float32)
    beta = jax.nn.sigmoid(b_raw.astype(jnp.float32)).transpose(0, 2, 1)
    g = -jnp.exp(a_log.astype(jnp.float32)) * jax.nn.softplus(
        a_raw.astype(jnp.float32) + dt_bias.astype(jnp.float32))
    g = g.transpose(0, 2, 1)
    o = gated_delta_rule(q, k, v, g, beta).transpose(0, 2, 1, 3)
    o = o * lax.rsqrt(jnp.mean(o * o, axis=-1, keepdims=True) + RMS_EPS) * norm_g.astype(jnp.float32)
    o = o * jax.nn.silu(z.astype(jnp.float32).reshape(B, S, DN_HEADS, DN_HD))
    return o.reshape(B, S, DN_W).astype(z.dtype)


def memory_cross_attention(h, mem_n, wq, wkv, wo):
    B, S, _ = h.shape
    M = mem_n.shape[1]
    q = (h @ wq).reshape(B, S, XA_HEADS, XA_HD)
    kv = (mem_n @ wkv).reshape(B, M, 2, XA_HEADS, XA_HD)
    s = jnp.einsum('bshd,bmhd->bhsm', q, kv[:, :, 0]).astype(jnp.float32) * (XA_HD ** -0.5)
    p = jax.nn.softmax(s, axis=-1).astype(h.dtype)
    o = jnp.einsum('bhsm,bmhd->bshd', p, kv[:, :, 1]).reshape(B, S, XA_W)
    return o @ wo


def swiglu(h, w_gate_up, w_down):
    gate, up = jnp.split(h @ w_gate_up, 2, axis=-1)
    return (jax.nn.silu(gate) * up) @ w_down


def moe_ffn(h, router, w_gate_up, w_down):
    B, S, D = h.shape
    hf = h.reshape(B * S, D)
    logits = (hf @ router).astype(jnp.float32)
    top_logit, top_idx = lax.top_k(logits, MOE_TOPK)
    top_w = jax.nn.softmax(top_logit, axis=-1)
    combine = jnp.sum(jax.nn.one_hot(top_idx, N_EXPERTS, dtype=jnp.float32) * top_w[..., None], axis=1)
    combine = combine.astype(h.dtype)
    y = jnp.zeros_like(hf)
    for e in range(N_EXPERTS):
        y = y + combine[:, e:e + 1] * swiglu(hf, w_gate_up[e], w_down[e])
    return y.reshape(B, S, D)


def setup_inputs(seed: int = 0) -> dict:
    f32 = jnp.float32
    ks = iter(jax.random.split(jax.random.key(seed), 40))

    def nrm(shape, scale):
        return jax.random.normal(next(ks), shape, f32) * scale

    def gain(shape):
        return 1.0 + nrm(shape, 0.02)

    x = nrm((BATCH, SEQ, D_MODEL), 1.0)
    mem = nrm((BATCH, MEM_LEN, D_MODEL), 1.0)
    positions = jnp.broadcast_to(jnp.arange(SEQ, dtype=jnp.int32)[None, :], (BATCH, SEQ))
    mix_norm_g = gain((DEPTH, D_MODEL))
    w_in = nrm((DEPTH, D_MODEL, IN_COLS), D_MODEL ** -0.5)
    pool_w = nrm((DEPTH, POOL_GROUPS, POOL_GROUP_W, POOL_GROUP_W), POOL_GROUP_W ** -0.5)
    pool_scale = 1.0 + nrm((DEPTH, POOL_W), 0.1)
    dn_conv_w = nrm((DEPTH, DN_CONV, 3 * DN_W), DN_CONV ** -0.5)
    dn_a_log = jnp.log(jax.random.uniform(next(ks), (DEPTH, DN_HEADS), f32, 1.0, 16.0))
    dt = jnp.exp(jax.random.uniform(next(ks), (DEPTH, DN_HEADS), f32, math.log(1e-3), math.log(1e-1)))
    dn_dt_bias = dt + jnp.log(-jnp.expm1(-dt))
    dn_norm_g = gain((DEPTH, DN_HD))
    w_up_pool = nrm((DEPTH, POOL_W, D_MODEL), POOL_W ** -0.5)
    w_up_moba = nrm((DEPTH, MOBA_W, D_MODEL), MOBA_W ** -0.5)
    w_up_dn = nrm((DEPTH, DN_W, D_MODEL), DN_W ** -0.5)
    w_out = nrm((DEPTH, D_MODEL, D_MODEL), D_MODEL ** -0.5)
    xa_norm_g = gain((DEPTH, D_MODEL))
    mem_norm_g = gain((DEPTH, D_MODEL))
    xa_wq = nrm((DEPTH, D_MODEL, XA_W), D_MODEL ** -0.5)
    xa_wkv = nrm((DEPTH, D_MODEL, 2 * XA_W), D_MODEL ** -0.5)
    xa_wo = nrm((DEPTH, XA_W, D_MODEL), XA_W ** -0.5)
    ffn_norm_g = gain((DEPTH, D_MODEL))
    dense_w_gate_up = nrm((N_DENSE, D_MODEL, 2 * DENSE_FF), D_MODEL ** -0.5)
    dense_w_down = nrm((N_DENSE, DENSE_FF, D_MODEL), DENSE_FF ** -0.5)
    moe_router = nrm((N_MOE, D_MODEL, N_EXPERTS), D_MODEL ** -0.5)
    moe_w_gate_up = nrm((N_MOE, N_EXPERTS, D_MODEL, 2 * EXPERT_FF), D_MODEL ** -0.5)
    moe_w_down = nrm((N_MOE, N_EXPERTS, EXPERT_FF, D_MODEL), EXPERT_FF ** -0.5)
    final_norm_g = gain((D_MODEL,))
    return {"x": x, "mem": mem, "positions": positions, "mix_norm_g": mix_norm_g, "w_in": w_in,
            "pool_w": pool_w, "pool_scale": pool_scale, "dn_conv_w": dn_conv_w, "dn_a_log": dn_a_log,
            "dn_dt_bias": dn_dt_bias, "dn_norm_g": dn_norm_g, "w_up_pool": w_up_pool,
            "w_up_moba": w_up_moba, "w_up_dn": w_up_dn, "w_out": w_out, "xa_norm_g": xa_norm_g,
            "mem_norm_g": mem_norm_g, "xa_wq": xa_wq, "xa_wkv": xa_wkv, "xa_wo": xa_wo,
            "ffn_norm_g": ffn_norm_g, "dense_w_gate_up": dense_w_gate_up, "dense_w_down": dense_w_down,
            "moe_router": moe_router, "moe_w_gate_up": moe_w_gate_up, "moe_w_down": moe_w_down,
            "final_norm_g": final_norm_g}


def reference(x, mem, positions, mix_norm_g, w_in, pool_w, pool_scale, dn_conv_w, dn_a_log,
              dn_dt_bias, dn_norm_g, w_up_pool, w_up_moba, w_up_dn, w_out, xa_norm_g, mem_norm_g,
              xa_wq, xa_wkv, xa_wo, ffn_norm_g, dense_w_gate_up, dense_w_down, moe_router,
              moe_w_gate_up, moe_w_down, final_norm_g):
    B, S, D = x.shape
    for layer in range(DEPTH):
        h = rmsnorm(x, mix_norm_g[layer])
        proj = h @ w_in[layer]
        p_in = proj[..., OFF_POOL:OFF_MOBA]
        mqkv = proj[..., OFF_MOBA:OFF_DN_QKV].reshape(B, S, 3, MOBA_HEADS, MOBA_HD)
        mq = partial_rope(mqkv[:, :, 0], positions)
        mk = partial_rope(mqkv[:, :, 1], positions)
        mv = mqkv[:, :, 2]
        dn_qkv = proj[..., OFF_DN_QKV:OFF_DN_BETA]
        dn_b = proj[..., OFF_DN_BETA:OFF_DN_ALPHA]
        dn_a = proj[..., OFF_DN_ALPHA:OFF_DN_Z]
        dn_z = proj[..., OFF_DN_Z:OFF_GATE]
        gates = jax.nn.sigmoid(proj[..., OFF_GATE:].astype(jnp.float32)).astype(x.dtype)
        gates = gates.reshape(B, S, N_BRANCH, D)

        y_pool = pool_mixer(p_in, pool_w[layer], pool_scale[layer]) @ w_up_pool[layer]
        y_moba = moba_attention(mq, mk, mv) @ w_up_moba[layer]
        y_dn = deltanet_mixer(dn_qkv, dn_b, dn_a, dn_z, dn_conv_w[layer], dn_a_log[layer],
                              dn_dt_bias[layer], dn_norm_g[layer]) @ w_up_dn[layer]
        merged = gates[:, :, 0] * y_pool + gates[:, :, 1] * y_moba + gates[:, :, 2] * y_dn
        x = x + merged @ w_out[layer]

        x = x + memory_cross_attention(rmsnorm(x, xa_norm_g[layer]), rmsnorm(mem, mem_norm_g[layer]),
                                       xa_wq[layer], xa_wkv[layer], xa_wo[layer])

        h = rmsnorm(x, ffn_norm_g[layer])
        if layer % 2 == 0:
            x = x + swiglu(h, dense_w_gate_up[layer // 2], dense_w_down[layer // 2])
        else:
            x = x + moe_ffn(h, moe_router[layer // 2], moe_w_gate_up[layer // 2], moe_w_down[layer // 2])
    return rmsnorm(x, final_norm_g)
```

```python
import functools

import numpy as np
import jax
import jax.numpy as jnp
from jax import lax
from jax.experimental import pallas as pl
from jax.experimental.pallas import tpu as pltpu

F32 = jnp.float32
BF16 = jnp.bfloat16

RMS_EPS = 1e-6
D_MODEL = 1024
POOL_W = 256
POOL_GROUP_W = 64
POOL_WINDOWS = (2, 4, 8, 16)
SEQ_HALO = 16
MOBA_HEADS = 4
MOBA_HD = 64
MOBA_W = 256
MOBA_BLOCK = 256
MOBA_TOPK = 3
ROPE_THETA = 500000.0
ROPE_DIMS = 16
DN_HEADS = 8
DN_HD = 64
DN_W = 512
DN_CONV = 4
DN_CHUNK = 128
XA_HEADS = 4
XA_HD = 128
XA_W = 512
N_EXPERTS = 8
MOE_TOPK = 2
COL_POOL, COL_MQ, COL_MK, COL_MV = 0, 1, 2, 3
COL_DQ, COL_DZ = 2, 5
COL_GATE0 = 3
PROJ_COLS = 6144
PROJ_TN = 512
PROJ_ID_BLOCKS = 6
BA_LANES = 128
VMEM_LIMIT = 56 * 1024 * 1024
NEG_BIG = -1e30


def _cparams(*sem):
    return pltpu.CompilerParams(dimension_semantics=sem, vmem_limit_bytes=VMEM_LIMIT)


def _rms(x, g):
    ms = jnp.mean(x * x, axis=-1, keepdims=True)
    return x * lax.rsqrt(ms + RMS_EPS) * g


def _silu(x):
    return x * jax.nn.sigmoid(x)


def _dot(a, b):
    return jnp.dot(a, b, preferred_element_type=F32)


def _dot_nt(a, b):
    return lax.dot_general(a, b, (((1,), (1,)), ((), ())), preferred_element_type=F32)


def _dot_tn(a, b):
    return lax.dot_general(a, b, (((0,), (0,)), ((), ())), preferred_element_type=F32)


def _inproj_kernel(x_ref, g_ref, w_ref, wba_ref, proj_ref, ba_ref, h_ref):
    j = pl.program_id(1)

    @pl.when(j == 0)
    def _():
        h = _rms(x_ref[...], g_ref[...]).astype(BF16)
        h_ref[...] = h
        ba_ref[...] = _dot(h, wba_ref[...])

    acc = _dot(h_ref[...], w_ref[...])

    @pl.when(j < PROJ_ID_BLOCKS)
    def _():
        proj_ref[...] = acc.astype(BF16)

    @pl.when(j >= PROJ_ID_BLOCKS)
    def _():
        proj_ref[...] = jax.nn.sigmoid(acc).astype(BF16)


def _inproj(x, g, w_main, w_ba, tm):
    n = x.shape[0]
    return pl.pallas_call(
        _inproj_kernel,
        grid=(n // tm, PROJ_COLS // PROJ_TN),
        in_specs=[
            pl.BlockSpec((tm, D_MODEL), lambda i, j: (i, 0)),
            pl.BlockSpec((1, D_MODEL), lambda i, j: (0, 0)),
            pl.BlockSpec((D_MODEL, PROJ_TN), lambda i, j: (0, j)),
            pl.BlockSpec((D_MODEL, BA_LANES), lambda i, j: (0, 0)),
        ],
        out_specs=[
            pl.BlockSpec((tm, PROJ_TN), lambda i, j: (i, j)),
            pl.BlockSpec((tm, BA_LANES), lambda i, j: (i, 0)),
        ],
        out_shape=[jax.ShapeDtypeStruct((n, PROJ_COLS), BF16),
                   jax.ShapeDtypeStruct((n, BA_LANES), F32)],
        scratch_shapes=[pltpu.VMEM((tm, D_MODEL), BF16)],
        compiler_params=_cparams("parallel", "arbitrary"),
        name="inproj",
    )(x, g, w_main, w_ba)


def _rope_table_kernel(pos_ref, freq_ref, sign_ref, c_ref, s_ref):
    ang = pos_ref[...] * freq_ref[...]
    rot = sign_ref[...]
    c_ref[...] = jnp.where(rot != 0.0, jnp.cos(ang), 1.0)
    s_ref[...] = jnp.sin(ang) * rot


def _rope_tables(pos_f32, tm):
    n = pos_f32.shape[0]
    half = ROPE_DIMS // 2
    inv_freq = np.power(ROPE_THETA, -np.arange(half, dtype=np.float32) * 2.0 / ROPE_DIMS).astype(np.float32)
    d = np.arange(128) % MOBA_HD
    freq = np.where(d < ROPE_DIMS, inv_freq[d % half], 0.0).astype(np.float32)[None, :]
    sign = np.where(d < half, -1.0, np.where(d < ROPE_DIMS, 1.0, 0.0)).astype(np.float32)[None, :]
    return pl.pallas_call(
        _rope_table_kernel,
        grid=(n // tm,),
        in_specs=[pl.BlockSpec((tm, 1), lambda i: (i, 0)),
                  pl.BlockSpec((1, 128), lambda i: (0, 0)),
                  pl.BlockSpec((1, 128), lambda i: (0, 0))],
        out_specs=[pl.BlockSpec((tm, 128), lambda i: (i, 0)),
                   pl.BlockSpec((tm, 128), lambda i: (i, 0))],
        out_shape=[jax.ShapeDtypeStruct((n, 128), F32), jax.ShapeDtypeStruct((n, 128), F32)],
        compiler_params=_cparams("parallel"),
        name="rope_tables",
    )(pos_f32, jnp.asarray(freq), jnp.asarray(sign))


def _moba_prep_kernel(q_ref, k_ref, c_ref, s_ref, qo_ref, ko_ref, kbar_ref):
    c = jnp.concatenate([c_ref[...], c_ref[...]], axis=1)
    s = jnp.concatenate([s_ref[...], s_ref[...]], axis=1)
    lane = lax.broadcasted_iota(jnp.int32, (MOBA_BLOCK, MOBA_W), 1)
    first_half = (lane % MOBA_HD) < (ROPE_DIMS // 2)

    def rope(x):
        partner = jnp.where(first_half,
                            pltpu.roll(x, MOBA_W - ROPE_DIMS // 2, 1),
                            pltpu.roll(x, ROPE_DIMS // 2, 1))
        return x * c + partner * s

    qo_ref[...] = rope(q_ref[...].astype(F32)).astype(BF16)
    k = rope(k_ref[...].astype(F32))
    ko_ref[...] = k.astype(BF16)
    kbar_ref[0] = jnp.mean(k, axis=0, keepdims=True)


def _moba_prep(proj, cos_t, sin_t):
    n = proj.shape[0]
    nblk = n // MOBA_BLOCK
    return pl.pallas_call(
        _moba_prep_kernel,
        grid=(nblk,),
        in_specs=[pl.BlockSpec((MOBA_BLOCK, MOBA_W), lambda r: (r, COL_MQ)),
                  pl.BlockSpec((MOBA_BLOCK, MOBA_W), lambda r: (r, COL_MK)),
                  pl.BlockSpec((MOBA_BLOCK, 128), lambda r: (r, 0)),
                  pl.BlockSpec((MOBA_BLOCK, 128), lambda r: (r, 0))],
        out_specs=[pl.BlockSpec((MOBA_BLOCK, MOBA_W), lambda r: (r, 0)),
                   pl.BlockSpec((MOBA_BLOCK, MOBA_W), lambda r: (r, 0)),
                   pl.BlockSpec((1, 1, MOBA_W), lambda r: (r, 0, 0))],
        out_shape=[jax.ShapeDtypeStruct((n, MOBA_W), BF16),
                   jax.ShapeDtypeStruct((n, MOBA_W), BF16),
                   jax.ShapeDtypeStruct((nblk, 1, MOBA_W), F32)],
        compiler_params=_cparams("parallel"),
        name="moba_prep",
    )(proj, proj, cos_t, sin_t)


def _moba_kernel(q_ref, k_ref, v_ref, kbar_ref, o_ref, *, nb):
    i = pl.program_id(1)
    scale = MOBA_HD ** -0.5
    blk = lax.broadcasted_iota(jnp.int32, (MOBA_BLOCK, nb), 1).astype(F32)
    row = lax.broadcasted_iota(jnp.int32, (MOBA_BLOCK, MOBA_BLOCK), 0)
    col = lax.broadcasted_iota(jnp.int32, (MOBA_BLOCK, MOBA_BLOCK), 1)
    causal = col <= row
    i_f = i.astype(F32)
    own = pl.multiple_of(i * MOBA_BLOCK, MOBA_BLOCK)

    for h in range(MOBA_HEADS):
        sl = slice(h * MOBA_HD, (h + 1) * MOBA_HD)
        qh = q_ref[:, sl]
        gate = _dot_nt(qh, kbar_ref[0, :, sl].astype(BF16))
        gate = jnp.where(blk < i_f, gate, -jnp.inf)
        sel = jnp.zeros((MOBA_BLOCK, nb), F32)
        for _ in range(MOBA_TOPK):
            m = jnp.max(gate, axis=1, keepdims=True)
            idx = jnp.min(jnp.where(gate == m, blk, float(nb)), axis=1, keepdims=True)
            pick = (blk == idx) & (m > -jnp.inf)
            sel = jnp.where(pick, 1.0, sel)
            gate = jnp.where(pick, -jnp.inf, gate)

        s = _dot_nt(qh, k_ref[pl.ds(own, MOBA_BLOCK), sl]) * scale
        s = jnp.where(causal, s, NEG_BIG)
        m_i = jnp.max(s, axis=1, keepdims=True)
        p = jnp.exp(s - m_i)
        l_i = jnp.sum(p, axis=1, keepdims=True)
        acc = _dot(p.astype(BF16), v_ref[pl.ds(own, MOBA_BLOCK), sl])

        def body(j, carry, qh=qh, sel=sel, sl=sl):
            m_i, l_i, acc = carry
            start = pl.multiple_of(j * MOBA_BLOCK, MOBA_BLOCK)
            s = _dot_nt(qh, k_ref[pl.ds(start, MOBA_BLOCK), sl]) * scale
            picked = jnp.sum(jnp.where(blk == j.astype(F32), sel, 0.0), axis=1, keepdims=True) > 0.5
            s = jnp.where(picked, s, NEG_BIG)
            m_new = jnp.maximum(m_i, jnp.max(s, axis=1, keepdims=True))
            alpha = jnp.exp(m_i - m_new)
            p = jnp.exp(s - m_new)
            l_new = alpha * l_i + jnp.sum(p, axis=1, keepdims=True)
            acc_new = alpha * acc + _dot(p.astype(BF16), v_ref[pl.ds(start, MOBA_BLOCK), sl])
            return m_new, l_new, acc_new

        m_i, l_i, acc = lax.fori_loop(0, i, body, (m_i, l_i, acc))
        o_ref[:, sl] = (acc / l_i).astype(BF16)


def _moba(q, k, proj, kbar, batch, seq):
    n = q.shape[0]
    nb = seq // MOBA_BLOCK
    return pl.pallas_call(
        functools.partial(_moba_kernel, nb=nb),
        grid=(batch, nb),
        in_specs=[pl.BlockSpec((MOBA_BLOCK, MOBA_W), lambda b, i: (b * nb + i, 0)),
                  pl.BlockSpec((seq, MOBA_W), lambda b, i: (b, 0)),
                  pl.BlockSpec((seq, MOBA_W), lambda b, i: (b, COL_MV)),
                  pl.BlockSpec((1, nb, MOBA_W), lambda b, i: (b, 0, 0))],
        out_specs=pl.BlockSpec((MOBA_BLOCK, MOBA_W), lambda b, i: (b * nb + i, 0)),
        out_shape=jax.ShapeDtypeStruct((n, MOBA_W), BF16),
        compiler_params=_cparams("parallel", "arbitrary"),
        name="moba",
    )(q, k, proj, kbar)


def _pool_kernel(p_ref, halo_ref, w_ref, scale_ref, o_ref, *, ts):
    i = pl.program_id(1)
    halo = jnp.where(i == 0, 0.0, halo_ref[...].astype(F32))
    p = p_ref[...].astype(F32)
    cur = jnp.concatenate([halo, p], axis=0)
    lane = lax.broadcasted_iota(jnp.int32, (ts, POOL_W), 1)
    t1 = (lax.broadcasted_iota(jnp.int32, (ts, POOL_W), 0) + i * ts + 1).astype(F32)
    total = jnp.zeros((ts, POOL_W), F32)
    count = jnp.ones((ts, POOL_W), F32)
    span = 1
    for gi, w in enumerate(POOL_WINDOWS):
        while span < w:
            cur = cur + pltpu.roll(cur, span, 0)
            span *= 2
        in_group = (lane >= gi * POOL_GROUP_W) & (lane < (gi + 1) * POOL_GROUP_W)
        total = jnp.where(in_group, cur[SEQ_HALO:], total)
        count = jnp.where(in_group, jnp.minimum(t1, float(w)), count)
    pooled = total / count - p
    o_ref[...] = (_dot(pooled.astype(BF16), w_ref[...]) * scale_ref[...]).astype(BF16)


def _pool(proj, w_blockdiag, scale, batch, seq, ts):
    n = proj.shape[0]
    nt = seq // ts
    hb = ts // SEQ_HALO
    return pl.pallas_call(
        functools.partial(_pool_kernel, ts=ts),
        grid=(batch, nt),
        in_specs=[pl.BlockSpec((ts, POOL_W), lambda b, i: (b * nt + i, COL_POOL)),
                  pl.BlockSpec((SEQ_HALO, POOL_W), lambda b, i: (jnp.maximum((b * nt + i) * hb - 1, 0), COL_POOL)),
                  pl.BlockSpec((POOL_W, POOL_W), lambda b, i: (0, 0)),
                  pl.BlockSpec((1, POOL_W), lambda b, i: (0, 0))],
        out_specs=pl.BlockSpec((ts, POOL_W), lambda b, i: (b * nt + i, 0)),
        out_shape=jax.ShapeDtypeStruct((n, POOL_W), BF16),
        compiler_params=_cparams("parallel", "parallel"),
        name="pool",
    )(proj, proj, w_blockdiag, scale)


def _deltanet_kernel(q_ref, k_ref, v_ref, qh_ref, kh_ref, vh_ref, z_ref, ba_ref, cw_ref, alog_ref, dtb_ref,
                     ng_ref, o_ref, state_ref):
    i = pl.program_id(1)
    C = DN_CHUNK

    @pl.when(i == 0)
    def _():
        state_ref[...] = jnp.zeros_like(state_ref)

    def conv(x_ref, halo_ref, w):
        halo = jnp.where(i == 0, 0.0, halo_ref[...].astype(F32))
        ext = jnp.concatenate([halo, x_ref[...].astype(F32)], axis=0)
        y = ext * w[DN_CONV - 1:DN_CONV]
        for lag in range(1, DN_CONV):
            y = y + pltpu.roll(ext, lag, 0) * w[DN_CONV - 1 - lag:DN_CONV - lag]
        return _silu(y[SEQ_HALO:])

    cw = cw_ref[...]
    qc = conv(q_ref, qh_ref, cw[0])
    kc = conv(k_ref, kh_ref, cw[1])
    vc = conv(v_ref, vh_ref, cw[2])

    ba = ba_ref[...]
    beta = jax.nn.sigmoid(ba)
    g = -jnp.exp(alog_ref[...]) * jax.nn.softplus(ba + dtb_ref[...])
    rows = lax.broadcasted_iota(jnp.int32, (C, BA_LANES), 0)
    G = g
    span = 1
    while span < C:
        G = G + jnp.where(rows >= span, pltpu.roll(G, span, 0), 0.0)
        span *= 2
    GT = G.T
    r_i = lax.broadcasted_iota(jnp.int32, (C, C), 0)
    c_i = lax.broadcasted_iota(jnp.int32, (C, C), 1)
    tril = c_i <= r_i
    eye = (c_i == r_i).astype(F32)
    levels = C.bit_length() - 1
    level_masks = [((r_i >> k) & 1 == 1) & ((c_i >> k) == (r_i >> k) - 1) for k in range(levels)]
    z = z_ref[...].astype(F32)
    ng = ng_ref[...]

    for h in range(DN_HEADS):
        sl = slice(h * DN_HD, (h + 1) * DN_HD)
        qh = qc[:, sl]
        kh = kc[:, sl]
        vh = vc[:, sl]
        qh = qh * lax.rsqrt(jnp.sum(qh * qh, axis=1, keepdims=True) + RMS_EPS) * (DN_HD ** -0.5)
        kh = kh * lax.rsqrt(jnp.sum(kh * kh, axis=1, keepdims=True) + RMS_EPS)
        b_c = beta[:, h:h + 1]
        G_c = G[:, DN_HEADS + h:DN_HEADS + h + 1]
        G_r = GT[DN_HEADS + h:DN_HEADS + h + 1, :]
        G_last = G_c[C - 1:C, :]
        eG = jnp.exp(G_c)
        kb = kh * b_c
        kh16 = kh.astype(BF16)
        gram = _dot_nt(jnp.concatenate([kb, qh], axis=0).astype(BF16), kh16)
        decay = jnp.exp(jnp.where(tril, G_c - G_r, -jnp.inf))
        A = (gram[:C] * decay).astype(BF16)
        aqk = gram[C:] * decay
        X = jnp.concatenate([vh * b_c, kb * eG], axis=1)
        L1 = jnp.where(level_masks[0], A, 0.0)
        Z = jnp.concatenate([eye - L1.astype(F32), X - _dot(L1, X.astype(BF16))], axis=1)
        for lvl in range(1, levels):
            Z16 = Z.astype(BF16)
            Lb = jnp.where(level_masks[lvl], A, 0.0)
            if lvl < levels - 1:
                Z = Z - _dot(Z16[:, :C], _dot(Lb, Z16).astype(BF16))
            else:
                X = Z[:, C:] - _dot(Z16[:, :C], _dot(Lb, Z16[:, C:]).astype(BF16))
        u = X[:, :DN_HD]
        w = X[:, DN_HD:]
        S = state_ref[h]
        ws = _dot(jnp.concatenate([w, qh * eG], axis=0).astype(BF16), S.astype(BF16))
        v_new = u - ws[:C]
        v16 = v_new.astype(BF16)
        o = ws[C:] + _dot(aqk.astype(BF16), v16)
        k_dec = kh * jnp.exp(G_last - G_c)
        state_ref[h] = S * jnp.exp(G_last) + _dot_tn(k_dec.astype(BF16), v16)
        o = o * lax.rsqrt(jnp.mean(o * o, axis=1, keepdims=True) + RMS_EPS) * ng
        o = o * _silu(z[:, sl])
        o_ref[:, sl] = o.astype(BF16)


def _deltanet(proj, ba, conv_w, alog_row, dtb_row, norm_g, batch, seq):
    n = proj.shape[0]
    C = DN_CHUNK
    nt = seq // C
    hb = C // SEQ_HALO

    def cur(col):
        return pl.BlockSpec((C, DN_W), lambda b, i: (b * nt + i, col))

    def halo(col):
        return pl.BlockSpec((SEQ_HALO, DN_W), lambda b, i: (jnp.maximum((b * nt + i) * hb - 1, 0), col))

    return pl.pallas_call(
        _deltanet_kernel,
        grid=(batch, nt),
        in_specs=[cur(COL_DQ), cur(COL_DQ + 1), cur(COL_DQ + 2),
                  halo(COL_DQ), halo(COL_DQ + 1), halo(COL_DQ + 2),
                  cur(COL_DZ),
                  pl.BlockSpec((C, BA_LANES), lambda b, i: (b * nt + i, 0)),
                  pl.BlockSpec((3, DN_CONV, DN_W), lambda b, i: (0, 0, 0)),
                  pl.BlockSpec((1, BA_LANES), lambda b, i: (0, 0)),
                  pl.BlockSpec((1, BA_LANES), lambda b, i: (0, 0)),
                  pl.BlockSpec((1, DN_HD), lambda b, i: (0, 0))],
        out_specs=pl.BlockSpec((C, DN_W), lambda b, i: (b * nt + i, 0)),
        out_shape=jax.ShapeDtypeStruct((n, DN_W), BF16),
        scratch_shapes=[pltpu.VMEM((DN_HEADS, DN_HD, DN_HD), F32)],
        compiler_params=_cparams("parallel", "arbitrary"),
        name="deltanet",
    )(proj, proj, proj, proj, proj, proj, proj, ba, conv_w, alog_row, dtb_row, norm_g)


def _merge_kernel(x_ref, yp_ref, ym_ref, yd_ref, g0_ref, g1_ref, g2_ref, wp_ref, wm_ref, wd_ref, wo_ref, o_ref):
    merged = (g0_ref[...].astype(F32) * _dot(yp_ref[...], wp_ref[...])
              + g1_ref[...].astype(F32) * _dot(ym_ref[...], wm_ref[...])
              + g2_ref[...].astype(F32) * _dot(yd_ref[...], wd_ref[...]))
    o_ref[...] = x_ref[...] + _dot(merged.astype(BF16), wo_ref[...])


def _merge(x, y_pool, y_moba, y_dn, proj, w_up_pool, w_up_moba, w_up_dn, w_out, tm):
    n = x.shape[0]

    def rows(width, col=0):
        return pl.BlockSpec((tm, width), lambda i: (i, col))

    def whole(shape):
        return pl.BlockSpec(shape, lambda i: (0, 0))

    return pl.pallas_call(
        _merge_kernel,
        grid=(n // tm,),
        in_specs=[rows(D_MODEL), rows(POOL_W), rows(MOBA_W), rows(DN_W),
                  rows(D_MODEL, COL_GATE0), rows(D_MODEL, COL_GATE0 + 1), rows(D_MODEL, COL_GATE0 + 2),
                  whole((POOL_W, D_MODEL)), whole((MOBA_W, D_MODEL)), whole((DN_W, D_MODEL)),
                  whole((D_MODEL, D_MODEL))],
        out_specs=rows(D_MODEL),
        out_shape=jax.ShapeDtypeStruct((n, D_MODEL), F32),
        compiler_params=_cparams("parallel"),
        name="merge",
    )(x, y_pool, y_moba, y_dn, proj, proj, proj, w_up_pool, w_up_moba, w_up_dn, w_out)


def _memkv_kernel(mem_ref, g_ref, w_ref, o_ref):
    o_ref[...] = _dot(_rms(mem_ref[...], g_ref[...]).astype(BF16), w_ref[...]).astype(BF16)


def _memkv(mem2d, g, wkv):
    m = mem2d.shape[0]
    tm = 256
    return pl.pallas_call(
        _memkv_kernel,
        grid=(m // tm,),
        in_specs=[pl.BlockSpec((tm, D_MODEL), lambda i: (i, 0)),
                  pl.BlockSpec((1, D_MODEL), lambda i: (0, 0)),
                  pl.BlockSpec((D_MODEL, 2 * XA_W), lambda i: (0, 0))],
        out_specs=pl.BlockSpec((tm, 2 * XA_W), lambda i: (i, 0)),
        out_shape=jax.ShapeDtypeStruct((m, 2 * XA_W), BF16),
        compiler_params=_cparams("parallel"),
        name="memkv",
    )(mem2d, g, wkv)


def _xattn_kernel(x_ref, g_ref, wq_ref, kv_ref, wo_ref, o_ref):
    x = x_ref[...]
    q = _dot(_rms(x, g_ref[...]).astype(BF16), wq_ref[...]).astype(BF16)
    scale = XA_HD ** -0.5
    outs = []
    for h in range(XA_HEADS):
        sl = slice(h * XA_HD, (h + 1) * XA_HD)
        k = kv_ref[:, h * XA_HD:(h + 1) * XA_HD]
        v = kv_ref[:, XA_W + h * XA_HD:XA_W + (h + 1) * XA_HD]
        s = _dot_nt(q[:, sl], k) * scale
        s = s - jnp.max(s, axis=1, keepdims=True)
        p = jnp.exp(s)
        p = p / jnp.sum(p, axis=1, keepdims=True)
        outs.append(_dot(p.astype(BF16), v).astype(BF16))
    o = jnp.concatenate(outs, axis=1)
    o_ref[...] = x + _dot(o, wo_ref[...])


def _xattn(x, g, wq, kv, wo, seq, mem_len, tm):
    n = x.shape[0]
    tiles_per_seq = seq // tm
    return pl.pallas_call(
        _xattn_kernel,
        grid=(n // tm,),
        in_specs=[pl.BlockSpec((tm, D_MODEL), lambda i: (i, 0)),
                  pl.BlockSpec((1, D_MODEL), lambda i: (0, 0)),
                  pl.BlockSpec((D_MODEL, XA_W), lambda i: (0, 0)),
                  pl.BlockSpec((mem_len, 2 * XA_W), lambda i: (i // tiles_per_seq, 0)),
                  pl.BlockSpec((XA_W, D_MODEL), lambda i: (0, 0))],
        out_specs=pl.BlockSpec((tm, D_MODEL), lambda i: (i, 0)),
        out_shape=jax.ShapeDtypeStruct((n, D_MODEL), F32),
        compiler_params=_cparams("parallel"),
        name="xattn",
    )(x, g, wq, kv, wo)


def _swiglu_chunk(h, wg_ref, wu_ref, wd_ref):
    a = _silu(_dot(h, wg_ref[...])) * _dot(h, wu_ref[...])
    return _dot(a.astype(BF16), wd_ref[...])


def _ffn_kernel(x_ref, g_ref, wg_ref, wu_ref, wd_ref, o_ref, h_ref, acc_ref):
    j = pl.program_id(1)

    @pl.when(j == 0)
    def _():
        h_ref[...] = _rms(x_ref[...], g_ref[...]).astype(BF16)
        acc_ref[...] = x_ref[...]

    acc_ref[...] += _swiglu_chunk(h_ref[...], wg_ref, wu_ref, wd_ref)

    @pl.when(j == pl.num_programs(1) - 1)
    def _():
        o_ref[...] = acc_ref[...]


def _ffn(x, g, w_gate_up, w_down, tm, tf):
    n = x.shape[0]
    ff = w_down.shape[0]
    nf = ff // tf
    return pl.pallas_call(
        _ffn_kernel,
        grid=(n // tm, nf),
        in_specs=[pl.BlockSpec((tm, D_MODEL), lambda i, j: (i, 0)),
                  pl.BlockSpec((1, D_MODEL), lambda i, j: (0, 0)),
                  pl.BlockSpec((D_MODEL, tf), lambda i, j: (0, j)),
                  pl.BlockSpec((D_MODEL, tf), lambda i, j: (0, nf + j)),
                  pl.BlockSpec((tf, D_MODEL), lambda i, j: (j, 0))],
        out_specs=pl.BlockSpec((tm, D_MODEL), lambda i, j: (i, 0)),
        out_shape=jax.ShapeDtypeStruct((n, D_MODEL), F32),
        scratch_shapes=[pltpu.VMEM((tm, D_MODEL), BF16), pltpu.VMEM((tm, D_MODEL), F32)],
        compiler_params=_cparams("parallel", "arbitrary"),
        name="ffn",
    )(x, g, w_gate_up, w_gate_up, w_down)


def _moe_kernel(x_ref, g_ref, r_ref, wg_ref, wu_ref, wd_ref, fg_ref, o_ref, h_ref, comb_ref, acc_ref, *, tm):
    e = pl.program_id(1)
    lane = lax.broadcasted_iota(jnp.int32, (tm, 128), 1).astype(F32)

    @pl.when(e == 0)
    def _():
        h = _rms(x_ref[...], g_ref[...]).astype(BF16)
        h_ref[...] = h
        acc_ref[...] = x_ref[...]
        logits = jnp.where(lane < float(N_EXPERTS), _dot(h, r_ref[...]), -jnp.inf)
        m1 = jnp.max(logits, axis=1, keepdims=True)
        i1 = jnp.min(jnp.where(logits == m1, lane, 128.0), axis=1, keepdims=True)
        rest = jnp.where(lane == i1, -jnp.inf, logits)
        m2 = jnp.max(rest, axis=1, keepdims=True)
        i2 = jnp.min(jnp.where(rest == m2, lane, 128.0), axis=1, keepdims=True)
        e2 = jnp.exp(m2 - m1)
        w1 = 1.0 / (1.0 + e2)
        w2 = e2 / (1.0 + e2)
        comb_ref[...] = jnp.where(lane == i1, w1, 0.0) + jnp.where(lane == i2, w2, 0.0)

    w_e = jnp.sum(jnp.where(lane == e.astype(F32), comb_ref[...], 0.0), axis=1, keepdims=True)
    acc_ref[...] += w_e * _swiglu_chunk(h_ref[...], wg_ref.at[0], wu_ref.at[0], wd_ref.at[0])

    @pl.when(e == pl.num_programs(1) - 1)
    def _():
        o_ref[...] = _rms(acc_ref[...], fg_ref[...])


def _moe_final(x, g, router, w_gate_up, w_down, final_g, tm):
    n = x.shape[0]
    ff = w_down.shape[1]
    return pl.pallas_call(
        functools.partial(_moe_kernel, tm=tm),
        grid=(n // tm, N_EXPERTS),
        in_specs=[pl.BlockSpec((tm, D_MODEL), lambda i, e: (i, 0)),
                  pl.BlockSpec((1, D_MODEL), lambda i, e: (0, 0)),
                  pl.BlockSpec((D_MODEL, 128), lambda i, e: (0, 0)),
                  pl.BlockSpec((1, D_MODEL, ff), lambda i, e: (e, 0, 0)),
                  pl.BlockSpec((1, D_MODEL, ff), lambda i, e: (e, 0, 1)),
                  pl.BlockSpec((1, ff, D_MODEL), lambda i, e: (e, 0, 0)),
                  pl.BlockSpec((1, D_MODEL), lambda i, e: (0, 0))],
        out_specs=pl.BlockSpec((tm, D_MODEL), lambda i, e: (i, 0)),
        out_shape=jax.ShapeDtypeStruct((n, D_MODEL), F32),
        scratch_shapes=[pltpu.VMEM((tm, D_MODEL), BF16), pltpu.VMEM((tm, 128), F32),
                        pltpu.VMEM((tm, D_MODEL), F32)],
        compiler_params=_cparams("parallel", "arbitrary"),
        name="moe",
    )(x, g, router, w_gate_up, w_gate_up, w_down, final_g)


def _blockdiag(pool_w):
    g, c, _ = pool_w.shape
    out = jnp.zeros((g * c, g * c), pool_w.dtype)
    for gi in range(g):
        out = out.at[gi * c:(gi + 1) * c, gi * c:(gi + 1) * c].set(pool_w[gi])
    return out


def _lane_row(v, offset):
    return jnp.zeros((1, BA_LANES), F32).at[0, offset:offset + v.shape[0]].set(v.astype(F32))


def kernel(x, mem, positions, mix_norm_g, w_in, pool_w, pool_scale, dn_conv_w, dn_a_log, dn_dt_bias, dn_norm_g,
           w_up_pool, w_up_moba, w_up_dn, w_out, xa_norm_g, mem_norm_g, xa_wq, xa_wkv, xa_wo, ffn_norm_g,
           dense_w_gate_up, dense_w_down, moe_router, moe_w_gate_up, moe_w_down, final_norm_g):
    batch, seq, d = x.shape
    depth = w_in.shape[0]
    mem_len = mem.shape[1]
    n = batch * seq
    assert d == D_MODEL and seq % MOBA_BLOCK == 0 and depth == 2
    tm = min(512, seq)

    xf = x.reshape(n, d)
    mem2d = mem.reshape(batch * mem_len, d)
    cos_t, sin_t = _rope_tables(positions.reshape(n, 1).astype(F32), tm)
    off_ba = POOL_W + 3 * MOBA_W + 3 * DN_W

    for layer in range(depth):
        wl = w_in[layer]
        w_main = jnp.concatenate([wl[:, :off_ba], wl[:, off_ba + 2 * DN_HEADS:]], axis=1).astype(BF16)
        w_ba = jnp.pad(wl[:, off_ba:off_ba + 2 * DN_HEADS], ((0, 0), (0, BA_LANES - 2 * DN_HEADS))).astype(BF16)
        proj, ba = _inproj(xf, mix_norm_g[layer][None, :], w_main, w_ba, tm)

        y_pool = _pool(proj, _blockdiag(pool_w[layer]).astype(BF16), pool_scale[layer][None, :], batch, seq, tm)
        mq, mk, kbar = _moba_prep(proj, cos_t, sin_t)
        y_moba = _moba(mq, mk, proj, kbar.reshape(batch, seq // MOBA_BLOCK, MOBA_W), batch, seq)
        conv_w = dn_conv_w[layer].reshape(DN_CONV, 3, DN_W).transpose(1, 0, 2)
        y_dn = _deltanet(proj, ba, conv_w, _lane_row(dn_a_log[layer], DN_HEADS),
                         _lane_row(dn_dt_bias[layer], DN_HEADS), dn_norm_g[layer][None, :], batch, seq)
        xf = _merge(xf, y_pool, y_moba, y_dn, proj, w_up_pool[layer].astype(BF16), w_up_moba[layer].astype(BF16),
                    w_up_dn[layer].astype(BF16), w_out[layer].astype(BF16), tm)

        kv = _memkv(mem2d, mem_norm_g[layer][None, :], xa_wkv[layer].astype(BF16))
        xf = _xattn(xf, xa_norm_g[layer][None, :], xa_wq[layer].astype(BF16), kv, xa_wo[layer].astype(BF16),
                    seq, mem_len, tm)

        if layer % 2 == 0:
            xf = _ffn(xf, ffn_norm_g[layer][None, :], dense_w_gate_up[layer // 2].astype(BF16),
                      dense_w_down[layer // 2].astype(BF16), tm, 256)
        else:
            router = jnp.pad(moe_router[layer // 2], ((0, 0), (0, 128 - N_EXPERTS))).astype(BF16)
            xf = _moe_final(xf, ffn_norm_g[layer][None, :], router, moe_w_gate_up[layer // 2].astype(BF16),
                            moe_w_down[layer // 2].astype(BF16), final_norm_g[None, :], tm)
    return xf.reshape(batch, seq, d)
```

```python
import functools

import numpy as np
import jax
import jax.numpy as jnp
from jax import lax
from jax.experimental import pallas as pl
from jax.experimental.pallas import tpu as pltpu

F32 = jnp.float32
BF16 = jnp.bfloat16

RMS_EPS = 1e-6
D_MODEL = 1024
POOL_W = 256
POOL_GROUP_W = 64
POOL_WINDOWS = (2, 4, 8, 16)
SEQ_HALO = 16
MOBA_HEADS = 4
MOBA_HD = 64
MOBA_W = 256
MOBA_AUG_W = 2 * MOBA_W
MOBA_BLOCK = 256
MOBA_TOPK = 3
ROPE_THETA = 500000.0
ROPE_DIMS = 16
DN_HEADS = 8
DN_HD = 64
DN_W = 512
DN_CONV = 4
DN_CHUNK = 128
XA_HEADS = 4
XA_HD = 128
XA_W = 512
N_EXPERTS = 8
MOE_TOPK = 2
COL_POOL, COL_MQ, COL_MK, COL_MV = 0, 1, 2, 3
COL_DQ, COL_DZ = 2, 5
COL_GATE0 = 3
PROJ_COLS = 6144
PROJ_TN = 512
PROJ_ID_BLOCKS = 6
BA_LANES = 128
VMEM_LIMIT = 56 * 1024 * 1024
NEG_BIG = -1e30


def _cparams(*sem):
    return pltpu.CompilerParams(dimension_semantics=sem, vmem_limit_bytes=VMEM_LIMIT)


def _rms(x, g):
    ms = jnp.mean(x * x, axis=-1, keepdims=True)
    return x * lax.rsqrt(ms + RMS_EPS) * g


def _silu(x):
    return x * jax.nn.sigmoid(x)


def _dot(a, b):
    return jnp.dot(a, b, preferred_element_type=F32)


def _dot_nt(a, b):
    return lax.dot_general(a, b, (((1,), (1,)), ((), ())), preferred_element_type=F32)


def _dot_tn(a, b):
    return lax.dot_general(a, b, (((0,), (0,)), ((), ())), preferred_element_type=F32)


def _inproj_kernel(x_ref, g_ref, w_ref, wba_ref, proj_ref, ba_ref, h_ref):
    j = pl.program_id(1)

    @pl.when(j == 0)
    def _():
        h = _rms(x_ref[...], g_ref[...]).astype(BF16)
        h_ref[...] = h
        ba_ref[...] = _dot(h, wba_ref[...])

    acc = _dot(h_ref[...], w_ref[...])

    @pl.when(j < PROJ_ID_BLOCKS)
    def _():
        proj_ref[...] = acc.astype(BF16)

    @pl.when(j >= PROJ_ID_BLOCKS)
    def _():
        proj_ref[...] = jax.nn.sigmoid(acc).astype(BF16)


def _inproj(x, g, w_main, w_ba, tm):
    n = x.shape[0]
    return pl.pallas_call(
        _inproj_kernel,
        grid=(n // tm, PROJ_COLS // PROJ_TN),
        in_specs=[
            pl.BlockSpec((tm, D_MODEL), lambda i, j: (i, 0)),
            pl.BlockSpec((1, D_MODEL), lambda i, j: (0, 0)),
            pl.BlockSpec((D_MODEL, PROJ_TN), lambda i, j: (0, j)),
            pl.BlockSpec((D_MODEL, BA_LANES), lambda i, j: (0, 0)),
        ],
        out_specs=[
            pl.BlockSpec((tm, PROJ_TN), lambda i, j: (i, j)),
            pl.BlockSpec((tm, BA_LANES), lambda i, j: (i, 0)),
        ],
        out_shape=[jax.ShapeDtypeStruct((n, PROJ_COLS), BF16),
                   jax.ShapeDtypeStruct((n, BA_LANES), F32)],
        scratch_shapes=[pltpu.VMEM((tm, D_MODEL), BF16)],
        compiler_params=_cparams("parallel", "arbitrary"),
        name="inproj",
    )(x, g, w_main, w_ba)


def _rope_table_kernel(pos_ref, freq_ref, sign_ref, c_ref, s_ref):
    ang = pos_ref[...] * freq_ref[...]
    rot = sign_ref[...]
    c_ref[...] = jnp.where(rot != 0.0, jnp.cos(ang), 1.0)
    s_ref[...] = jnp.sin(ang) * rot


def _rope_tables(pos_f32, tm):
    n = pos_f32.shape[0]
    half = ROPE_DIMS // 2
    inv_freq = np.power(ROPE_THETA, -np.arange(half, dtype=np.float32) * 2.0 / ROPE_DIMS).astype(np.float32)
    d = np.arange(128) % MOBA_HD
    freq = np.where(d < ROPE_DIMS, inv_freq[d % half], 0.0).astype(np.float32)[None, :]
    sign = np.where(d < half, -1.0, np.where(d < ROPE_DIMS, 1.0, 0.0)).astype(np.float32)[None, :]
    return pl.pallas_call(
        _rope_table_kernel,
        grid=(n // tm,),
        in_specs=[pl.BlockSpec((tm, 1), lambda i: (i, 0)),
                  pl.BlockSpec((1, 128), lambda i: (0, 0)),
                  pl.BlockSpec((1, 128), lambda i: (0, 0))],
        out_specs=[pl.BlockSpec((tm, 128), lambda i: (i, 0)),
                   pl.BlockSpec((tm, 128), lambda i: (i, 0))],
        out_shape=[jax.ShapeDtypeStruct((n, 128), F32), jax.ShapeDtypeStruct((n, 128), F32)],
        compiler_params=_cparams("parallel"),
        name="rope_tables",
    )(pos_f32, jnp.asarray(freq), jnp.asarray(sign))


def _moba_prep_kernel(q_ref, k_ref, v_ref, c_ref, s_ref, qo_ref, ko_ref, vo_ref, kbar_ref, *, nb):
    c = jnp.concatenate([c_ref[...], c_ref[...]], axis=1)
    s = jnp.concatenate([s_ref[...], s_ref[...]], axis=1)
    lane = lax.broadcasted_iota(jnp.int32, (MOBA_BLOCK, MOBA_W), 1)
    first_half = (lane % MOBA_HD) < (ROPE_DIMS // 2)

    def rope(x):
        partner = jnp.where(first_half,
                            pltpu.roll(x, MOBA_W - ROPE_DIMS // 2, 1),
                            pltpu.roll(x, ROPE_DIMS // 2, 1))
        return x * c + partner * s

    qo_ref[...] = (rope(q_ref[...].astype(F32)) * (MOBA_HD ** -0.5)).astype(BF16)
    k = rope(k_ref[...].astype(F32))
    kbar_ref[0] = jnp.mean(k, axis=0, keepdims=True)
    k16 = k.astype(BF16)
    v16 = v_ref[...]
    blk_lane = lax.broadcasted_iota(jnp.int32, (MOBA_BLOCK, MOBA_HD), 1)
    onehot = jnp.where(blk_lane == pl.program_id(0) % nb, 1.0, 0.0).astype(BF16)
    zeros = jnp.zeros((MOBA_BLOCK, MOBA_HD), BF16)
    k_parts, v_parts = [], []
    for h in range(MOBA_HEADS):
        sl = slice(h * MOBA_HD, (h + 1) * MOBA_HD)
        k_parts += [k16[:, sl], onehot]
        v_parts += [v16[:, sl], zeros]
    ko_ref[...] = jnp.concatenate(k_parts, axis=1)
    vo_ref[...] = jnp.concatenate(v_parts, axis=1)


def _moba_prep(proj, cos_t, sin_t, nb):
    n = proj.shape[0]
    nblk = n // MOBA_BLOCK
    return pl.pallas_call(
        functools.partial(_moba_prep_kernel, nb=nb),
        grid=(nblk,),
        in_specs=[pl.BlockSpec((MOBA_BLOCK, MOBA_W), lambda r: (r, COL_MQ)),
                  pl.BlockSpec((MOBA_BLOCK, MOBA_W), lambda r: (r, COL_MK)),
                  pl.BlockSpec((MOBA_BLOCK, MOBA_W), lambda r: (r, COL_MV)),
                  pl.BlockSpec((MOBA_BLOCK, 128), lambda r: (r, 0)),
                  pl.BlockSpec((MOBA_BLOCK, 128), lambda r: (r, 0))],
        out_specs=[pl.BlockSpec((MOBA_BLOCK, MOBA_W), lambda r: (r, 0)),
                   pl.BlockSpec((MOBA_BLOCK, MOBA_AUG_W), lambda r: (r, 0)),
                   pl.BlockSpec((MOBA_BLOCK, MOBA_AUG_W), lambda r: (r, 0)),
                   pl.BlockSpec((1, 1, MOBA_W), lambda r: (r, 0, 0))],
        out_shape=[jax.ShapeDtypeStruct((n, MOBA_W), BF16),
                   jax.ShapeDtypeStruct((n, MOBA_AUG_W), BF16),
                   jax.ShapeDtypeStruct((n, MOBA_AUG_W), BF16),
                   jax.ShapeDtypeStruct((nblk, 1, MOBA_W), F32)],
        compiler_params=_cparams("parallel"),
        name="moba_prep",
    )(proj, proj, proj, cos_t, sin_t)


def _moba_kernel(q_ref, k_ref, v_ref, kbar_ref, o_ref, qa_ref):
    i = pl.program_id(1)
    heads = range(MOBA_HEADS)
    blk = lax.broadcasted_iota(jnp.int32, (MOBA_BLOCK, MOBA_HD), 1).astype(F32)
    row = lax.broadcasted_iota(jnp.int32, (MOBA_BLOCK, MOBA_BLOCK), 0)
    col = lax.broadcasted_iota(jnp.int32, (MOBA_BLOCK, MOBA_BLOCK), 1)
    causal = col <= row
    i_f = i.astype(F32)
    own = pl.multiple_of(i * MOBA_BLOCK, MOBA_BLOCK)
    hs = [slice(h * 2 * MOBA_HD, (h + 1) * 2 * MOBA_HD) for h in heads]

    for h in heads:
        sl = slice(h * MOBA_HD, (h + 1) * MOBA_HD)
        qh = q_ref[:, sl]
        gate = _dot_nt(qh, kbar_ref[0, :, sl].astype(BF16))
        gate = jnp.where(blk < i_f, gate, -jnp.inf)
        keep = blk == i_f
        for _ in range(MOBA_TOPK):
            m = jnp.max(gate, axis=1, keepdims=True)
            idx = jnp.min(jnp.where(gate == m, blk, float(MOBA_HD)), axis=1, keepdims=True)
            pick = (blk == idx) & (m > -jnp.inf)
            keep = keep | pick
            gate = jnp.where(pick, -jnp.inf, gate)
        qa_ref[h] = jnp.concatenate([qh, jnp.where(keep, 0.0, NEG_BIG).astype(BF16)], axis=1)

    def scores(h, start):
        return _dot_nt(qa_ref[h], k_ref[pl.ds(start, MOBA_BLOCK), hs[h]])

    def fold(x):
        return x[:, :128], x[:, 128:]

    mx = []
    for h in heads:
        a, b = fold(jnp.where(causal, scores(h, own), NEG_BIG))
        mx.append(jnp.maximum(a, b))

    def body_max(j, mx):
        start = pl.multiple_of(j * MOBA_BLOCK, MOBA_BLOCK)
        out = []
        for h in heads:
            a, b = fold(scores(h, start))
            out.append(jnp.maximum(mx[h], jnp.maximum(a, b)))
        return tuple(out)

    mx = lax.fori_loop(0, i, body_max, tuple(mx))
    m_row = [jnp.max(mx[h], axis=1, keepdims=True) for h in heads]

    ls, acc = [], []
    for h in heads:
        p = jnp.exp(jnp.where(causal, scores(h, own), NEG_BIG) - m_row[h])
        a, b = fold(p)
        ls.append(a + b)
        acc.append(_dot(p.astype(BF16), v_ref[pl.ds(own, MOBA_BLOCK), hs[h]]))

    def body_acc(j, carry):
        ls, acc = carry
        start = pl.multiple_of(j * MOBA_BLOCK, MOBA_BLOCK)
        ls_out, acc_out = [], []
        for h in heads:
            p = jnp.exp(scores(h, start) - m_row[h])
            a, b = fold(p)
            ls_out.append(ls[h] + (a + b))
            acc_out.append(acc[h] + _dot(p.astype(BF16), v_ref[pl.ds(start, MOBA_BLOCK), hs[h]]))
        return tuple(ls_out), tuple(acc_out)

    ls, acc = lax.fori_loop(0, i, body_acc, (tuple(ls), tuple(acc)))
    for h in heads:
        o_ref[:, hs[h]] = (acc[h] / jnp.sum(ls[h], axis=1, keepdims=True)).astype(BF16)


def _moba(q, k_aug, v_aug, kbar, batch, seq):
    n = q.shape[0]
    nb = seq // MOBA_BLOCK
    return pl.pallas_call(
        _moba_kernel,
        grid=(batch, nb),
        in_specs=[pl.BlockSpec((MOBA_BLOCK, MOBA_W), lambda b, i: (b * nb + i, 0)),
                  pl.BlockSpec((seq, MOBA_AUG_W), lambda b, i: (b, 0)),
                  pl.BlockSpec((seq, MOBA_AUG_W), lambda b, i: (b, 0)),
                  pl.BlockSpec((1, MOBA_HD, MOBA_W), lambda b, i: (b, 0, 0))],
        out_specs=pl.BlockSpec((MOBA_BLOCK, MOBA_AUG_W), lambda b, i: (b * nb + i, 0)),
        out_shape=jax.ShapeDtypeStruct((n, MOBA_AUG_W), BF16),
        scratch_shapes=[pltpu.VMEM((MOBA_HEADS, MOBA_BLOCK, 2 * MOBA_HD), BF16)],
        compiler_params=_cparams("parallel", "arbitrary"),
        name="moba",
    )(q, k_aug, v_aug, kbar)


def _pool_kernel(p_ref, halo_ref, w_ref, scale_ref, o_ref, *, ts):
    i = pl.program_id(1)
    halo = jnp.where(i == 0, 0.0, halo_ref[...].astype(F32))
    p = p_ref[...].astype(F32)
    cur = jnp.concatenate([halo, p], axis=0)
    lane = lax.broadcasted_iota(jnp.int32, (ts, POOL_W), 1)
    t1 = (lax.broadcasted_iota(jnp.int32, (ts, POOL_W), 0) + i * ts + 1).astype(F32)
    total = jnp.zeros((ts, POOL_W), F32)
    count = jnp.ones((ts, POOL_W), F32)
    span = 1
    for gi, w in enumerate(POOL_WINDOWS):
        while span < w:
            cur = cur + pltpu.roll(cur, span, 0)
            span *= 2
        in_group = (lane >= gi * POOL_GROUP_W) & (lane < (gi + 1) * POOL_GROUP_W)
        total = jnp.where(in_group, cur[SEQ_HALO:], total)
        count = jnp.where(in_group, jnp.minimum(t1, float(w)), count)
    pooled = total / count - p
    o_ref[...] = (_dot(pooled.astype(BF16), w_ref[...]) * scale_ref[...]).astype(BF16)


def _pool(proj, w_blockdiag, scale, batch, seq, ts):
    n = proj.shape[0]
    nt = seq // ts
    hb = ts // SEQ_HALO
    return pl.pallas_call(
        functools.partial(_pool_kernel, ts=ts),
        grid=(batch, nt),
        in_specs=[pl.BlockSpec((ts, POOL_W), lambda b, i: (b * nt + i, COL_POOL)),
                  pl.BlockSpec((SEQ_HALO, POOL_W), lambda b, i: (jnp.maximum((b * nt + i) * hb - 1, 0), COL_POOL)),
                  pl.BlockSpec((POOL_W, POOL_W), lambda b, i: (0, 0)),
                  pl.BlockSpec((1, POOL_W), lambda b, i: (0, 0))],
        out_specs=pl.BlockSpec((ts, POOL_W), lambda b, i: (b * nt + i, 0)),
        out_shape=jax.ShapeDtypeStruct((n, POOL_W), BF16),
        compiler_params=_cparams("parallel", "parallel"),
        name="pool",
    )(proj, proj, w_blockdiag, scale)


def _deltanet_kernel(q_ref, k_ref, v_ref, qh_ref, kh_ref, vh_ref, z_ref, ba_ref, cw_ref, alog_ref, dtb_ref,
                     ng_ref, o_ref, state_ref):
    i = pl.program_id(1)
    C = DN_CHUNK

    @pl.when(i == 0)
    def _():
        state_ref[...] = jnp.zeros_like(state_ref)

    def conv(x_ref, halo_ref, w):
        halo = jnp.where(i == 0, 0.0, halo_ref[...].astype(F32))
        ext = jnp.concatenate([halo, x_ref[...].astype(F32)], axis=0)
        y = ext * w[DN_CONV - 1:DN_CONV]
        for lag in range(1, DN_CONV):
            y = y + pltpu.roll(ext, lag, 0) * w[DN_CONV - 1 - lag:DN_CONV - lag]
        return _silu(y[SEQ_HALO:])

    cw = cw_ref[...]
    qc = conv(q_ref, qh_ref, cw[0])
    kc = conv(k_ref, kh_ref, cw[1])
    vc = conv(v_ref, vh_ref, cw[2])

    ba = ba_ref[...]
    beta = jax.nn.sigmoid(ba)
    g = -jnp.exp(alog_ref[...]) * jax.nn.softplus(ba + dtb_ref[...])
    rows = lax.broadcasted_iota(jnp.int32, (C, BA_LANES), 0)
    G = g
    span = 1
    while span < C:
        G = G + jnp.where(rows >= span, pltpu.roll(G, span, 0), 0.0)
        span *= 2
    GT = G.T
    r_i = lax.broadcasted_iota(jnp.int32, (C, C), 0)
    c_i = lax.broadcasted_iota(jnp.int32, (C, C), 1)
    tril = c_i <= r_i
    eye = (c_i == r_i).astype(F32)
    levels = C.bit_length() - 1
    level_masks = [((r_i >> k) & 1 == 1) & ((c_i >> k) == (r_i >> k) - 1) for k in range(levels)]
    z = z_ref[...].astype(F32)
    ng = ng_ref[...]

    heads = range(DN_HEADS)
    sls = [slice(h * DN_HD, (h + 1) * DN_HD) for h in heads]
    qn, kn, G_c, eG, kb, A, aqk, Z = [], [], [], [], [], [], [], []
    for h in heads:
        qh = qc[:, sls[h]]
        kh = kc[:, sls[h]]
        qn.append(qh * lax.rsqrt(jnp.sum(qh * qh, axis=1, keepdims=True) + RMS_EPS) * (DN_HD ** -0.5))
        kn.append(kh * lax.rsqrt(jnp.sum(kh * kh, axis=1, keepdims=True) + RMS_EPS))
        G_c.append(G[:, DN_HEADS + h:DN_HEADS + h + 1])
        eG.append(jnp.exp(G_c[h]))
        kb.append(kn[h] * beta[:, h:h + 1])
    for h in heads:
        gram = _dot_nt(jnp.concatenate([kb[h], qn[h]], axis=0).astype(BF16), kn[h].astype(BF16))
        G_r = GT[DN_HEADS + h:DN_HEADS + h + 1, :]
        decay = jnp.exp(jnp.where(tril, G_c[h] - G_r, -jnp.inf))
        A.append((gram[:C] * decay).astype(BF16))
        aqk.append((gram[C:] * decay).astype(BF16))
    for h in heads:
        X = jnp.concatenate([vc[:, sls[h]] * beta[:, h:h + 1], kb[h] * eG[h]], axis=1)
        L1 = jnp.where(level_masks[0], A[h], 0.0)
        Z.append(jnp.concatenate([eye - L1.astype(F32), X - _dot(L1, X.astype(BF16))], axis=1))
    for lvl in range(1, levels):
        Z16 = [Z[h].astype(BF16) for h in heads]
        cols = slice(0, C + 2 * DN_HD) if lvl < levels - 1 else slice(C, C + 2 * DN_HD)
        Y = [_dot(jnp.where(level_masks[lvl], A[h], 0.0), Z16[h][:, cols]).astype(BF16) for h in heads]
        Z = [Z[h][:, cols] - _dot(Z16[h][:, :C], Y[h]) for h in heads]
    S = [state_ref[h] for h in heads]
    ws = [_dot(jnp.concatenate([Z[h][:, DN_HD:], qn[h] * eG[h]], axis=0).astype(BF16), S[h].astype(BF16))
          for h in heads]
    v16 = [(Z[h][:, :DN_HD] - ws[h][:C]).astype(BF16) for h in heads]
    for h in heads:
        G_last = G_c[h][C - 1:C, :]
        k_dec = kn[h] * jnp.exp(G_last - G_c[h])
        state_ref[h] = S[h] * jnp.exp(G_last) + _dot_tn(k_dec.astype(BF16), v16[h])
    for h in heads:
        o = ws[h][C:] + _dot(aqk[h], v16[h])
        o = o * lax.rsqrt(jnp.mean(o * o, axis=1, keepdims=True) + RMS_EPS) * ng
        o = o * _silu(z[:, sls[h]])
        o_ref[:, sls[h]] = o.astype(BF16)


def _deltanet(proj, ba, conv_w, alog_row, dtb_row, norm_g, batch, seq):
    n = proj.shape[0]
    C = DN_CHUNK
    nt = seq // C
    hb = C // SEQ_HALO

    def cur(col):
        return pl.BlockSpec((C, DN_W), lambda b, i: (b * nt + i, col))

    def halo(col):
        return pl.BlockSpec((SEQ_HALO, DN_W), lambda b, i: (jnp.maximum((b * nt + i) * hb - 1, 0), col))

    return pl.pallas_call(
        _deltanet_kernel,
        grid=(batch, nt),
        in_specs=[cur(COL_DQ), cur(COL_DQ + 1), cur(COL_DQ + 2),
                  halo(COL_DQ), halo(COL_DQ + 1), halo(COL_DQ + 2),
                  cur(COL_DZ),
                  pl.BlockSpec((C, BA_LANES), lambda b, i: (b * nt + i, 0)),
                  pl.BlockSpec((3, DN_CONV, DN_W), lambda b, i: (0, 0, 0)),
                  pl.BlockSpec((1, BA_LANES), lambda b, i: (0, 0)),
                  pl.BlockSpec((1, BA_LANES), lambda b, i: (0, 0)),
                  pl.BlockSpec((1, DN_HD), lambda b, i: (0, 0))],
        out_specs=pl.BlockSpec((C, DN_W), lambda b, i: (b * nt + i, 0)),
        out_shape=jax.ShapeDtypeStruct((n, DN_W), BF16),
        scratch_shapes=[pltpu.VMEM((DN_HEADS, DN_HD, DN_HD), F32)],
        compiler_params=_cparams("parallel", "arbitrary"),
        name="deltanet",
    )(proj, proj, proj, proj, proj, proj, proj, ba, conv_w, alog_row, dtb_row, norm_g)


def _merge_kernel(x_ref, yp_ref, ym_ref, yd_ref, g0_ref, g1_ref, g2_ref, wp_ref, wm_ref, wd_ref, wo_ref, o_ref):
    merged = (g0_ref[...].astype(F32) * _dot(yp_ref[...], wp_ref[...])
              + g1_ref[...].astype(F32) * _dot(ym_ref[...], wm_ref[...])
              + g2_ref[...].astype(F32) * _dot(yd_ref[...], wd_ref[...]))
    o_ref[...] = x_ref[...] + _dot(merged.astype(BF16), wo_ref[...])


def _merge(x, y_pool, y_moba, y_dn, proj, w_up_pool, w_up_moba, w_up_dn, w_out, tm):
    n = x.shape[0]

    def rows(width, col=0):
        return pl.BlockSpec((tm, width), lambda i: (i, col))

    def whole(shape):
        return pl.BlockSpec(shape, lambda i: (0, 0))

    return pl.pallas_call(
        _merge_kernel,
        grid=(n // tm,),
        in_specs=[rows(D_MODEL), rows(POOL_W), rows(MOBA_AUG_W), rows(DN_W),
                  rows(D_MODEL, COL_GATE0), rows(D_MODEL, COL_GATE0 + 1), rows(D_MODEL, COL_GATE0 + 2),
                  whole((POOL_W, D_MODEL)), whole((MOBA_AUG_W, D_MODEL)), whole((DN_W, D_MODEL)),
                  whole((D_MODEL, D_MODEL))],
        out_specs=rows(D_MODEL),
        out_shape=jax.ShapeDtypeStruct((n, D_MODEL), F32),
        compiler_params=_cparams("parallel"),
        name="merge",
    )(x, y_pool, y_moba, y_dn, proj, proj, proj, w_up_pool, w_up_moba, w_up_dn, w_out)


def _memkv_kernel(mem_ref, g_ref, w_ref, o_ref):
    o_ref[...] = _dot(_rms(mem_ref[...], g_ref[...]).astype(BF16), w_ref[...]).astype(BF16)


def _memkv(mem2d, g, wkv):
    m = mem2d.shape[0]
    tm = 256
    return pl.pallas_call(
        _memkv_kernel,
        grid=(m // tm,),
        in_specs=[pl.BlockSpec((tm, D_MODEL), lambda i: (i, 0)),
                  pl.BlockSpec((1, D_MODEL), lambda i: (0, 0)),
                  pl.BlockSpec((D_MODEL, 2 * XA_W), lambda i: (0, 0))],
        out_specs=pl.BlockSpec((tm, 2 * XA_W), lambda i: (i, 0)),
        out_shape=jax.ShapeDtypeStruct((m, 2 * XA_W), BF16),
        compiler_params=_cparams("parallel"),
        name="memkv",
    )(mem2d, g, wkv)


def _xattn_kernel(x_ref, g_ref, wq_ref, kv_ref, wo_ref, o_ref):
    x = x_ref[...]
    q = _dot(_rms(x, g_ref[...]).astype(BF16), wq_ref[...]).astype(BF16)
    scale = XA_HD ** -0.5
    outs = []
    for h in range(XA_HEADS):
        sl = slice(h * XA_HD, (h + 1) * XA_HD)
        k = kv_ref[:, h * XA_HD:(h + 1) * XA_HD]
        v = kv_ref[:, XA_W + h * XA_HD:XA_W + (h + 1) * XA_HD]
        s = _dot_nt(q[:, sl], k) * scale
        s = s - jnp.max(s, axis=1, keepdims=True)
        p = jnp.exp(s)
        p = p / jnp.sum(p, axis=1, keepdims=True)
        outs.append(_dot(p.astype(BF16), v).astype(BF16))
    o = jnp.concatenate(outs, axis=1)
    o_ref[...] = x + _dot(o, wo_ref[...])


def _xattn(x, g, wq, kv, wo, seq, mem_len, tm):
    n = x.shape[0]
    tiles_per_seq = seq // tm
    return pl.pallas_call(
        _xattn_kernel,
        grid=(n // tm,),
        in_specs=[pl.BlockSpec((tm, D_MODEL), lambda i: (i, 0)),
                  pl.BlockSpec((1, D_MODEL), lambda i: (0, 0)),
                  pl.BlockSpec((D_MODEL, XA_W), lambda i: (0, 0)),
                  pl.BlockSpec((mem_len, 2 * XA_W), lambda i: (i // tiles_per_seq, 0)),
                  pl.BlockSpec((XA_W, D_MODEL), lambda i: (0, 0))],
        out_specs=pl.BlockSpec((tm, D_MODEL), lambda i: (i, 0)),
        out_shape=jax.ShapeDtypeStruct((n, D_MODEL), F32),
        compiler_params=_cparams("parallel"),
        name="xattn",
    )(x, g, wq, kv, wo)


def _swiglu_chunk(h, wg_ref, wu_ref, wd_ref):
    a = _silu(_dot(h, wg_ref[...])) * _dot(h, wu_ref[...])
    return _dot(a.astype(BF16), wd_ref[...])


def _ffn_kernel(x_ref, g_ref, wg_ref, wu_ref, wd_ref, o_ref, h_ref, acc_ref):
    j = pl.program_id(1)

    @pl.when(j == 0)
    def _():
        h_ref[...] = _rms(x_ref[...], g_ref[...]).astype(BF16)
        acc_ref[...] = x_ref[...]

    acc_ref[...] += _swiglu_chunk(h_ref[...], wg_ref, wu_ref, wd_ref)

    @pl.when(j == pl.num_programs(1) - 1)
    def _():
        o_ref[...] = acc_ref[...]


def _ffn(x, g, w_gate_up, w_down, tm, tf):
    n = x.shape[0]
    ff = w_down.shape[0]
    nf = ff // tf
    return pl.pallas_call(
        _ffn_kernel,
        grid=(n // tm, nf),
        in_specs=[pl.BlockSpec((tm, D_MODEL), lambda i, j: (i, 0)),
                  pl.BlockSpec((1, D_MODEL), lambda i, j: (0, 0)),
                  pl.BlockSpec((D_MODEL, tf), lambda i, j: (0, j)),
                  pl.BlockSpec((D_MODEL, tf), lambda i, j: (0, nf + j)),
                  pl.BlockSpec((tf, D_MODEL), lambda i, j: (j, 0))],
        out_specs=pl.BlockSpec((tm, D_MODEL), lambda i, j: (i, 0)),
        out_shape=jax.ShapeDtypeStruct((n, D_MODEL), F32),
        scratch_shapes=[pltpu.VMEM((tm, D_MODEL), BF16), pltpu.VMEM((tm, D_MODEL), F32)],
        compiler_params=_cparams("parallel", "arbitrary"),
        name="ffn",
    )(x, g, w_gate_up, w_gate_up, w_down)


def _moe_kernel(x_ref, g_ref, r_ref, wg_ref, wu_ref, wd_ref, fg_ref, o_ref, h_ref, comb_ref, acc_ref, *, tm):
    e = pl.program_id(1)
    lane = lax.broadcasted_iota(jnp.int32, (tm, 128), 1).astype(F32)

    @pl.when(e == 0)
    def _():
        h = _rms(x_ref[...], g_ref[...]).astype(BF16)
        h_ref[...] = h
        acc_ref[...] = x_ref[...]
        logits = jnp.where(lane < float(N_EXPERTS), _dot(h, r_ref[...]), -jnp.inf)
        m1 = jnp.max(logits, axis=1, keepdims=True)
        i1 = jnp.min(jnp.where(logits == m1, lane, 128.0), axis=1, keepdims=True)
        rest = jnp.where(lane == i1, -jnp.inf, logits)
        m2 = jnp.max(rest, axis=1, keepdims=True)
        i2 = jnp.min(jnp.where(rest == m2, lane, 128.0), axis=1, keepdims=True)
        e2 = jnp.exp(m2 - m1)
        w1 = 1.0 / (1.0 + e2)
        w2 = e2 / (1.0 + e2)
        comb_ref[...] = jnp.where(lane == i1, w1, 0.0) + jnp.where(lane == i2, w2, 0.0)

    w_e = jnp.sum(jnp.where(lane == e.astype(F32), comb_ref[...], 0.0), axis=1, keepdims=True)
    acc_ref[...] += w_e * _swiglu_chunk(h_ref[...], wg_ref.at[0], wu_ref.at[0], wd_ref.at[0])

    @pl.when(e == pl.num_programs(1) - 1)
    def _():
        o_ref[...] = _rms(acc_ref[...], fg_ref[...])


def _moe_final(x, g, router, w_gate_up, w_down, final_g, tm):
    n = x.shape[0]
    ff = w_down.shape[1]
    return pl.pallas_call(
        functools.partial(_moe_kernel, tm=tm),
        grid=(n // tm, N_EXPERTS),
        in_specs=[pl.BlockSpec((tm, D_MODEL), lambda i, e: (i, 0)),
                  pl.BlockSpec((1, D_MODEL), lambda i, e: (0, 0)),
                  pl.BlockSpec((D_MODEL, 128), lambda i, e: (0, 0)),
                  pl.BlockSpec((1, D_MODEL, ff), lambda i, e: (e, 0, 0)),
                  pl.BlockSpec((1, D_MODEL, ff), lambda i, e: (e, 0, 1)),
                  pl.BlockSpec((1, ff, D_MODEL), lambda i, e: (e, 0, 0)),
                  pl.BlockSpec((1, D_MODEL), lambda i, e: (0, 0))],
        out_specs=pl.BlockSpec((tm, D_MODEL), lambda i, e: (i, 0)),
        out_shape=jax.ShapeDtypeStruct((n, D_MODEL), F32),
        scratch_shapes=[pltpu.VMEM((tm, D_MODEL), BF16), pltpu.VMEM((tm, 128), F32),
                        pltpu.VMEM((tm, D_MODEL), F32)],
        compiler_params=_cparams("parallel", "arbitrary"),
        name="moe",
    )(x, g, router, w_gate_up, w_gate_up, w_down, final_g)


def _blockdiag(pool_w):
    g, c, _ = pool_w.shape
    out = jnp.zeros((g * c, g * c), pool_w.dtype)
    for gi in range(g):
        out = out.at[gi * c:(gi + 1) * c, gi * c:(gi + 1) * c].set(pool_w[gi])
    return out


def _lane_row(v, offset):
    return jnp.zeros((1, BA_LANES), F32).at[0, offset:offset + v.shape[0]].set(v.astype(F32))


def kernel(x, mem, positions, mix_norm_g, w_in, pool_w, pool_scale, dn_conv_w, dn_a_log, dn_dt_bias, dn_norm_g,
           w_up_pool, w_up_moba, w_up_dn, w_out, xa_norm_g, mem_norm_g, xa_wq, xa_wkv, xa_wo, ffn_norm_g,
           dense_w_gate_up, dense_w_down, moe_router, moe_w_gate_up, moe_w_down, final_norm_g):
    batch, seq, d = x.shape
    depth = w_in.shape[0]
    mem_len = mem.shape[1]
    n = batch * seq
    nb = seq // MOBA_BLOCK
    assert d == D_MODEL and seq % MOBA_BLOCK == 0 and nb <= MOBA_HD and depth == 2
    tm = min(512, seq)

    xf = x.reshape(n, d)
    mem2d = mem.reshape(batch * mem_len, d)
    cos_t, sin_t = _rope_tables(positions.reshape(n, 1).astype(F32), tm)
    off_ba = POOL_W + 3 * MOBA_W + 3 * DN_W

    for layer in range(depth):
        wl = w_in[layer]
        w_main = jnp.concatenate([wl[:, :off_ba], wl[:, off_ba + 2 * DN_HEADS:]], axis=1).astype(BF16)
        w_ba = jnp.pad(wl[:, off_ba:off_ba + 2 * DN_HEADS], ((0, 0), (0, BA_LANES - 2 * DN_HEADS))).astype(BF16)
        proj, ba = _inproj(xf, mix_norm_g[layer][None, :], w_main, w_ba, tm)

        y_pool = _pool(proj, _blockdiag(pool_w[layer]).astype(BF16), pool_scale[layer][None, :], batch, seq, tm)
        mq, mk, mv, kbar = _moba_prep(proj, cos_t, sin_t, nb)
        kbar = jnp.pad(kbar.reshape(batch, nb, MOBA_W), ((0, 0), (0, MOBA_HD - nb), (0, 0)))
        y_moba = _moba(mq, mk, mv, kbar, batch, seq)
        w_moba = jnp.pad(w_up_moba[layer].reshape(MOBA_HEADS, MOBA_HD, d), ((0, 0), (0, MOBA_HD), (0, 0)))
        w_moba = w_moba.reshape(MOBA_AUG_W, d).astype(BF16)
        conv_w = dn_conv_w[layer].reshape(DN_CONV, 3, DN_W).transpose(1, 0, 2)
        y_dn = _deltanet(proj, ba, conv_w, _lane_row(dn_a_log[layer], DN_HEADS),
                         _lane_row(dn_dt_bias[layer], DN_HEADS), dn_norm_g[layer][None, :], batch, seq)
        xf = _merge(xf, y_pool, y_moba, y_dn, proj, w_up_pool[layer].astype(BF16), w_moba,
                    w_up_dn[layer].astype(BF16), w_out[layer].astype(BF16), tm)

        kv = _memkv(mem2d, mem_norm_g[layer][None, :], xa_wkv[layer].astype(BF16))
        xf = _xattn(xf, xa_norm_g[layer][None, :], xa_wq[layer].astype(BF16), kv, xa_wo[layer].astype(BF16),
                    seq, mem_len, tm)

        if layer % 2 == 0:
            xf = _ffn(xf, ffn_norm_g[layer][None, :], dense_w_gate_up[layer // 2].astype(BF16),
                      dense_w_down[layer // 2].astype(BF16), tm, 256)
        else:
            router = jnp.pad(moe_router[layer // 2], ((0, 0), (0, 128 - N_EXPERTS))).astype(BF16)
            xf = _moe_final(xf, ffn_norm_g[layer][None, :], router, moe_w_gate_up[layer // 2].astype(BF16),
                            moe_w_down[layer // 2].astype(BF16), final_norm_g[None, :], tm)
    return xf.reshape(batch, seq, d)
```

```python
import functools

import numpy as np
import jax
import jax.numpy as jnp
from jax import lax
from jax.experimental import pallas as pl
from jax.experimental.pallas import tpu as pltpu

F32 = jnp.float32
BF16 = jnp.bfloat16

RMS_EPS = 1e-6
D_MODEL = 1024
POOL_W = 256
POOL_GROUP_W = 64
POOL_WINDOWS = (2, 4, 8, 16)
SEQ_HALO = 16
MOBA_HEADS = 4
MOBA_HD = 64
MOBA_W = 256
MOBA_AUG_W = 2 * MOBA_W
MOBA_BLOCK = 256
MOBA_TOPK = 3
MOBA_GROUP = 4
ROPE_THETA = 500000.0
ROPE_DIMS = 16
DN_HEADS = 8
DN_HD = 64
DN_W = 512
DN_CONV = 4
DN_CHUNK = 128
XA_HEADS = 4
XA_HD = 128
XA_W = 512
N_EXPERTS = 8
MOE_TOPK = 2
COL_POOL, COL_MQ, COL_MK, COL_MV = 0, 1, 2, 3
COL_DQ, COL_DZ = 2, 5
COL_GATE0 = 3
PROJ_COLS = 6144
PROJ_TN = 1024
PROJ_ID_BLOCKS = 3
BA_LANES = 128
VMEM_LIMIT = 56 * 1024 * 1024
NEG_BIG = -1e30


def _cparams(*sem):
    return pltpu.CompilerParams(dimension_semantics=sem, vmem_limit_bytes=VMEM_LIMIT)


def _rms(x, g):
    ms = jnp.mean(x * x, axis=-1, keepdims=True)
    return x * lax.rsqrt(ms + RMS_EPS) * g


def _silu(x):
    return x * jax.nn.sigmoid(x)


def _dot(a, b):
    return jnp.dot(a, b, preferred_element_type=F32)


def _dot_nt(a, b):
    return lax.dot_general(a, b, (((1,), (1,)), ((), ())), preferred_element_type=F32)


def _dot_tn(a, b):
    return lax.dot_general(a, b, (((0,), (0,)), ((), ())), preferred_element_type=F32)


def _inproj_kernel(x_ref, g_ref, w_ref, wba_ref, proj_ref, ba_ref, h_ref):
    j = pl.program_id(1)

    @pl.when(j == 0)
    def _():
        h = _rms(x_ref[...], g_ref[...]).astype(BF16)
        h_ref[...] = h
        ba_ref[...] = _dot(h, wba_ref[...])

    acc = _dot(h_ref[...], w_ref[...])

    @pl.when(j < PROJ_ID_BLOCKS)
    def _():
        proj_ref[...] = acc.astype(BF16)

    @pl.when(j >= PROJ_ID_BLOCKS)
    def _():
        proj_ref[...] = jax.nn.sigmoid(acc).astype(BF16)


def _inproj(x, g, w_main, w_ba, tm):
    n = x.shape[0]
    return pl.pallas_call(
        _inproj_kernel,
        grid=(n // tm, PROJ_COLS // PROJ_TN),
        in_specs=[
            pl.BlockSpec((tm, D_MODEL), lambda i, j: (i, 0)),
            pl.BlockSpec((1, D_MODEL), lambda i, j: (0, 0)),
            pl.BlockSpec((D_MODEL, PROJ_TN), lambda i, j: (0, j)),
            pl.BlockSpec((D_MODEL, BA_LANES), lambda i, j: (0, 0)),
        ],
        out_specs=[
            pl.BlockSpec((tm, PROJ_TN), lambda i, j: (i, j)),
            pl.BlockSpec((tm, BA_LANES), lambda i, j: (i, 0)),
        ],
        out_shape=[jax.ShapeDtypeStruct((n, PROJ_COLS), BF16),
                   jax.ShapeDtypeStruct((n, BA_LANES), F32)],
        scratch_shapes=[pltpu.VMEM((tm, D_MODEL), BF16)],
        compiler_params=_cparams("parallel", "arbitrary"),
        name="inproj",
    )(x, g, w_main, w_ba)


def _rope_table_kernel(pos_ref, freq_ref, sign_ref, c_ref, s_ref):
    ang = pos_ref[...] * freq_ref[...]
    rot = sign_ref[...]
    c_ref[...] = jnp.where(rot != 0.0, jnp.cos(ang), 1.0)
    s_ref[...] = jnp.sin(ang) * rot


def _rope_tables(pos_f32, tm):
    n = pos_f32.shape[0]
    half = ROPE_DIMS // 2
    inv_freq = np.power(ROPE_THETA, -np.arange(half, dtype=np.float32) * 2.0 / ROPE_DIMS).astype(np.float32)
    d = np.arange(128) % MOBA_HD
    freq = np.where(d < ROPE_DIMS, inv_freq[d % half], 0.0).astype(np.float32)[None, :]
    sign = np.where(d < half, -1.0, np.where(d < ROPE_DIMS, 1.0, 0.0)).astype(np.float32)[None, :]
    return pl.pallas_call(
        _rope_table_kernel,
        grid=(n // tm,),
        in_specs=[pl.BlockSpec((tm, 1), lambda i: (i, 0)),
                  pl.BlockSpec((1, 128), lambda i: (0, 0)),
                  pl.BlockSpec((1, 128), lambda i: (0, 0))],
        out_specs=[pl.BlockSpec((tm, 128), lambda i: (i, 0)),
                   pl.BlockSpec((tm, 128), lambda i: (i, 0))],
        out_shape=[jax.ShapeDtypeStruct((n, 128), F32), jax.ShapeDtypeStruct((n, 128), F32)],
        compiler_params=_cparams("parallel"),
        name="rope_tables",
    )(pos_f32, jnp.asarray(freq), jnp.asarray(sign))


def _moba_prep_kernel(q_ref, k_ref, v_ref, c_ref, s_ref, qo_ref, ko_ref, vo_ref, kbar_ref, *, nb):
    c = jnp.concatenate([c_ref[...], c_ref[...]], axis=1)
    s = jnp.concatenate([s_ref[...], s_ref[...]], axis=1)
    lane = lax.broadcasted_iota(jnp.int32, (MOBA_BLOCK, MOBA_W), 1)
    first_half = (lane % MOBA_HD) < (ROPE_DIMS // 2)

    def rope(x):
        partner = jnp.where(first_half,
                            pltpu.roll(x, MOBA_W - ROPE_DIMS // 2, 1),
                            pltpu.roll(x, ROPE_DIMS // 2, 1))
        return x * c + partner * s

    qo_ref[...] = (rope(q_ref[...].astype(F32)) * (MOBA_HD ** -0.5)).astype(BF16)
    k = rope(k_ref[...].astype(F32))
    kbar_ref[0] = jnp.mean(k, axis=0, keepdims=True)
    k16 = k.astype(BF16)
    v16 = v_ref[...]
    blk_lane = lax.broadcasted_iota(jnp.int32, (MOBA_BLOCK, MOBA_HD), 1)
    onehot = jnp.where(blk_lane == pl.program_id(0) % nb, 1.0, 0.0).astype(BF16)
    zeros = jnp.zeros((MOBA_BLOCK, MOBA_HD), BF16)
    k_parts, v_parts = [], []
    for h in range(MOBA_HEADS):
        sl = slice(h * MOBA_HD, (h + 1) * MOBA_HD)
        k_parts += [k16[:, sl], onehot]
        v_parts += [v16[:, sl], zeros]
    ko_ref[...] = jnp.concatenate(k_parts, axis=1)
    vo_ref[...] = jnp.concatenate(v_parts, axis=1)


def _moba_prep(proj, cos_t, sin_t, nb):
    n = proj.shape[0]
    nblk = n // MOBA_BLOCK
    return pl.pallas_call(
        functools.partial(_moba_prep_kernel, nb=nb),
        grid=(nblk,),
        in_specs=[pl.BlockSpec((MOBA_BLOCK, MOBA_W), lambda r: (r, COL_MQ)),
                  pl.BlockSpec((MOBA_BLOCK, MOBA_W), lambda r: (r, COL_MK)),
                  pl.BlockSpec((MOBA_BLOCK, MOBA_W), lambda r: (r, COL_MV)),
                  pl.BlockSpec((MOBA_BLOCK, 128), lambda r: (r, 0)),
                  pl.BlockSpec((MOBA_BLOCK, 128), lambda r: (r, 0))],
        out_specs=[pl.BlockSpec((MOBA_BLOCK, MOBA_W), lambda r: (r, 0)),
                   pl.BlockSpec((MOBA_BLOCK, MOBA_AUG_W), lambda r: (r, 0)),
                   pl.BlockSpec((MOBA_BLOCK, MOBA_AUG_W), lambda r: (r, 0)),
                   pl.BlockSpec((1, 1, MOBA_W), lambda r: (r, 0, 0))],
        out_shape=[jax.ShapeDtypeStruct((n, MOBA_W), BF16),
                   jax.ShapeDtypeStruct((n, MOBA_AUG_W), BF16),
                   jax.ShapeDtypeStruct((n, MOBA_AUG_W), BF16),
                   jax.ShapeDtypeStruct((nblk, 1, MOBA_W), F32)],
        compiler_params=_cparams("parallel"),
        name="moba_prep",
    )(proj, proj, proj, cos_t, sin_t)


def _moba_kernel(q_ref, k_ref, v_ref, kbar_ref, o_ref, qa_ref, m_ref, ls_ref, acc_ref, s_ref):
    i = pl.program_id(1)
    heads = range(MOBA_HEADS)
    blk = lax.broadcasted_iota(jnp.int32, (MOBA_BLOCK, MOBA_HD), 1).astype(F32)
    row = lax.broadcasted_iota(jnp.int32, (MOBA_BLOCK, MOBA_BLOCK), 0)
    col = lax.broadcasted_iota(jnp.int32, (MOBA_BLOCK, MOBA_BLOCK), 1)
    causal = col <= row
    i_f = i.astype(F32)
    own = pl.multiple_of(i * MOBA_BLOCK, MOBA_BLOCK)
    hs = [slice(h * 2 * MOBA_HD, (h + 1) * 2 * MOBA_HD) for h in heads]

    for h in heads:
        sl = slice(h * MOBA_HD, (h + 1) * MOBA_HD)
        qh = q_ref[:, sl]
        gate = _dot_nt(qh, kbar_ref[0, :, sl].astype(BF16))
        gate = jnp.where(blk < i_f, gate, -jnp.inf)
        keep = jnp.zeros((MOBA_BLOCK, MOBA_HD), jnp.bool_)
        for _ in range(MOBA_TOPK):
            m = jnp.max(gate, axis=1, keepdims=True)
            idx = jnp.min(jnp.where(gate == m, blk, float(MOBA_HD)), axis=1, keepdims=True)
            pick = (blk == idx) & (m > -jnp.inf)
            keep = keep | pick
            gate = jnp.where(pick, -jnp.inf, gate)
        qa_ref[h] = jnp.concatenate([qh, jnp.where(keep, 0.0, NEG_BIG).astype(BF16)], axis=1)
        qa_ref[MOBA_HEADS + h] = jnp.concatenate([qh, jnp.zeros_like(qh)], axis=1)

    def scores(h, start):
        return _dot_nt(qa_ref[h], k_ref[pl.ds(start, MOBA_BLOCK), hs[h]])

    def own_scores(h):
        s = _dot_nt(qa_ref[MOBA_HEADS + h], k_ref[pl.ds(own, MOBA_BLOCK), hs[h]])
        return jnp.where(causal, s, NEG_BIG)

    def fold(x):
        return x[:, :128], x[:, 128:]

    n_groups = (i + (MOBA_GROUP - 1)) // MOBA_GROUP

    def group_blocks(g):
        return [pl.multiple_of((g * MOBA_GROUP + t) * MOBA_BLOCK, MOBA_BLOCK) for t in range(MOBA_GROUP)]

    for h in heads:
        s = own_scores(h)
        m = jnp.max(s, axis=1, keepdims=True)
        a, b = fold(jnp.exp(s - m))
        ls_ref[h] = a + b
        acc_ref[h] = _dot(jnp.concatenate([a, b], axis=1).astype(BF16), v_ref[pl.ds(own, MOBA_BLOCK), hs[h]])
        m_ref[h] = jnp.broadcast_to(m, (MOBA_BLOCK, 128))

    @pl.loop(0, n_groups)
    def _(g):
        starts = group_blocks(g)
        gmax = [None] * MOBA_HEADS
        for t, start in enumerate(starts):
            for h in heads:
                s = scores(h, start)
                s_ref[t, h] = s
                a, b = fold(s)
                ab = jnp.maximum(a, b)
                gmax[h] = ab if gmax[h] is None else jnp.maximum(gmax[h], ab)
        m_new = []
        for h in heads:
            m_old = m_ref[h]
            m_new.append(jnp.maximum(m_old, jnp.max(gmax[h], axis=1, keepdims=True)))
            alpha = jnp.exp(m_old - m_new[h])
            ls_ref[h] = ls_ref[h] * alpha
            acc_ref[h] = acc_ref[h] * alpha
            m_ref[h] = m_new[h]
        for t, start in enumerate(starts):
            for h in heads:
                a = jnp.exp(s_ref[t, h, :, :128] - m_new[h])
                b = jnp.exp(s_ref[t, h, :, 128:] - m_new[h])
                ls_ref[h] += a + b
                acc_ref[h] += _dot(jnp.concatenate([a, b], axis=1).astype(BF16),
                                   v_ref[pl.ds(start, MOBA_BLOCK), hs[h]])

    for h in heads:
        o_ref[:, hs[h]] = (acc_ref[h] / jnp.sum(ls_ref[h], axis=1, keepdims=True)).astype(BF16)


def _moba(q, k_aug, v_aug, kbar, batch, seq):
    n = q.shape[0]
    nb = seq // MOBA_BLOCK
    return pl.pallas_call(
        _moba_kernel,
        grid=(batch, nb),
        in_specs=[pl.BlockSpec((MOBA_BLOCK, MOBA_W), lambda b, i: (b * nb + i, 0)),
                  pl.BlockSpec((seq, MOBA_AUG_W), lambda b, i: (b, 0)),
                  pl.BlockSpec((seq, MOBA_AUG_W), lambda b, i: (b, 0)),
                  pl.BlockSpec((1, MOBA_HD, MOBA_W), lambda b, i: (b, 0, 0))],
        out_specs=pl.BlockSpec((MOBA_BLOCK, MOBA_AUG_W), lambda b, i: (b * nb + i, 0)),
        out_shape=jax.ShapeDtypeStruct((n, MOBA_AUG_W), BF16),
        scratch_shapes=[pltpu.VMEM((2 * MOBA_HEADS, MOBA_BLOCK, 2 * MOBA_HD), BF16)]
        + [pltpu.VMEM((MOBA_HEADS, MOBA_BLOCK, 2 * MOBA_HD), F32)] * 3
        + [pltpu.VMEM((MOBA_GROUP, MOBA_HEADS, MOBA_BLOCK, MOBA_BLOCK), F32)],
        compiler_params=_cparams("parallel", "arbitrary"),
        name="moba",
    )(q, k_aug, v_aug, kbar)


def _pool_kernel(p_ref, halo_ref, w_ref, scale_ref, o_ref, *, ts):
    i = pl.program_id(1)
    halo = jnp.where(i == 0, 0.0, halo_ref[...].astype(F32))
    p = p_ref[...].astype(F32)
    cur = jnp.concatenate([halo, p], axis=0)
    lane = lax.broadcasted_iota(jnp.int32, (ts, POOL_W), 1)
    t1 = (lax.broadcasted_iota(jnp.int32, (ts, POOL_W), 0) + i * ts + 1).astype(F32)
    total = jnp.zeros((ts, POOL_W), F32)
    count = jnp.ones((ts, POOL_W), F32)
    span = 1
    for gi, w in enumerate(POOL_WINDOWS):
        while span < w:
            cur = cur + pltpu.roll(cur, span, 0)
            span *= 2
        in_group = (lane >= gi * POOL_GROUP_W) & (lane < (gi + 1) * POOL_GROUP_W)
        total = jnp.where(in_group, cur[SEQ_HALO:], total)
        count = jnp.where(in_group, jnp.minimum(t1, float(w)), count)
    pooled = total / count - p
    o_ref[...] = (_dot(pooled.astype(BF16), w_ref[...]) * scale_ref[...]).astype(BF16)


def _pool(proj, w_blockdiag, scale, batch, seq, ts):
    n = proj.shape[0]
    nt = seq // ts
    hb = ts // SEQ_HALO
    return pl.pallas_call(
        functools.partial(_pool_kernel, ts=ts),
        grid=(batch, nt),
        in_specs=[pl.BlockSpec((ts, POOL_W), lambda b, i: (b * nt + i, COL_POOL)),
                  pl.BlockSpec((SEQ_HALO, POOL_W), lambda b, i: (jnp.maximum((b * nt + i) * hb - 1, 0), COL_POOL)),
                  pl.BlockSpec((POOL_W, POOL_W), lambda b, i: (0, 0)),
                  pl.BlockSpec((1, POOL_W), lambda b, i: (0, 0))],
        out_specs=pl.BlockSpec((ts, POOL_W), lambda b, i: (b * nt + i, 0)),
        out_shape=jax.ShapeDtypeStruct((n, POOL_W), BF16),
        compiler_params=_cparams("parallel", "parallel"),
        name="pool",
    )(proj, proj, w_blockdiag, scale)


def _deltanet_kernel(q_ref, k_ref, v_ref, qh_ref, kh_ref, vh_ref, z_ref, ba_ref, cw_ref, alog_ref, dtb_ref,
                     ng_ref, o_ref, state_ref):
    i = pl.program_id(1)
    C = DN_CHUNK

    @pl.when(i == 0)
    def _():
        state_ref[...] = jnp.zeros_like(state_ref)

    def conv(x_ref, halo_ref, w):
        halo = jnp.where(i == 0, 0.0, halo_ref[...].astype(F32))
        ext = jnp.concatenate([halo, x_ref[...].astype(F32)], axis=0)
        y = ext * w[DN_CONV - 1:DN_CONV]
        for lag in range(1, DN_CONV):
            y = y + pltpu.roll(ext, lag, 0) * w[DN_CONV - 1 - lag:DN_CONV - lag]
        return _silu(y[SEQ_HALO:])

    cw = cw_ref[...]
    qc = conv(q_ref, qh_ref, cw[0])
    kc = conv(k_ref, kh_ref, cw[1])
    vc = conv(v_ref, vh_ref, cw[2])

    ba = ba_ref[...]
    beta = jax.nn.sigmoid(ba)
    g = -jnp.exp(alog_ref[...]) * jax.nn.softplus(ba + dtb_ref[...])
    rows = lax.broadcasted_iota(jnp.int32, (C, BA_LANES), 0)
    G = g
    span = 1
    while span < C:
        G = G + jnp.where(rows >= span, pltpu.roll(G, span, 0), 0.0)
        span *= 2
    GT = G.T
    r_i = lax.broadcasted_iota(jnp.int32, (C, C), 0)
    c_i = lax.broadcasted_iota(jnp.int32, (C, C), 1)
    tril = c_i <= r_i
    eye = (c_i == r_i).astype(F32)
    levels = C.bit_length() - 1
    level_masks = [((r_i >> k) & 1 == 1) & ((c_i >> k) == (r_i >> k) - 1) for k in range(levels)]
    z = z_ref[...].astype(F32)
    ng = ng_ref[...]

    heads = range(DN_HEADS)
    sls = [slice(h * DN_HD, (h + 1) * DN_HD) for h in heads]
    qn, kn, G_c, eG, kb, A, aqk, Z = [], [], [], [], [], [], [], []
    for h in heads:
        qh = qc[:, sls[h]]
        kh = kc[:, sls[h]]
        qn.append(qh * lax.rsqrt(jnp.sum(qh * qh, axis=1, keepdims=True) + RMS_EPS) * (DN_HD ** -0.5))
        kn.append(kh * lax.rsqrt(jnp.sum(kh * kh, axis=1, keepdims=True) + RMS_EPS))
        G_c.append(G[:, DN_HEADS + h:DN_HEADS + h + 1])
        eG.append(jnp.exp(G_c[h]))
        kb.append(kn[h] * beta[:, h:h + 1])
    for h in heads:
        gram = _dot_nt(jnp.concatenate([kb[h], qn[h]], axis=0).astype(BF16), kn[h].astype(BF16))
        G_r = GT[DN_HEADS + h:DN_HEADS + h + 1, :]
        decay = jnp.exp(jnp.where(tril, G_c[h] - G_r, -jnp.inf))
        A.append((gram[:C] * decay).astype(BF16))
        aqk.append((gram[C:] * decay).astype(BF16))
    for h in heads:
        X = jnp.concatenate([vc[:, sls[h]] * beta[:, h:h + 1], kb[h] * eG[h]], axis=1)
        L1 = jnp.where(level_masks[0], A[h], 0.0)
        Z.append(jnp.concatenate([eye - L1.astype(F32), X - _dot(L1, X.astype(BF16))], axis=1))
    for lvl in range(1, levels):
        Z16 = [Z[h].astype(BF16) for h in heads]
        cols = slice(0, C + 2 * DN_HD) if lvl < levels - 1 else slice(C, C + 2 * DN_HD)
        Y = [_dot(jnp.where(level_masks[lvl], A[h], 0.0), Z16[h][:, cols]).astype(BF16) for h in heads]
        Z = [Z[h][:, cols] - _dot(Z16[h][:, :C], Y[h]) for h in heads]
    S = [state_ref[h] for h in heads]
    ws = [_dot(jnp.concatenate([Z[h][:, DN_HD:], qn[h] * eG[h]], axis=0).astype(BF16), S[h].astype(BF16))
          for h in heads]
    v16 = [(Z[h][:, :DN_HD] - ws[h][:C]).astype(BF16) for h in heads]
    for h in heads:
        G_last = G_c[h][C - 1:C, :]
        k_dec = kn[h] * jnp.exp(G_last - G_c[h])
        state_ref[h] = S[h] * jnp.exp(G_last) + _dot_tn(k_dec.astype(BF16), v16[h])
    for h in heads:
        o = ws[h][C:] + _dot(aqk[h], v16[h])
        o = o * lax.rsqrt(jnp.mean(o * o, axis=1, keepdims=True) + RMS_EPS) * ng
        o = o * _silu(z[:, sls[h]])
        o_ref[:, sls[h]] = o.astype(BF16)


def _deltanet(proj, ba, conv_w, alog_row, dtb_row, norm_g, batch, seq):
    n = proj.shape[0]
    C = DN_CHUNK
    nt = seq // C
    hb = C // SEQ_HALO

    def cur(col):
        return pl.BlockSpec((C, DN_W), lambda b, i: (b * nt + i, col))

    def halo(col):
        return pl.BlockSpec((SEQ_HALO, DN_W), lambda b, i: (jnp.maximum((b * nt + i) * hb - 1, 0), col))

    return pl.pallas_call(
        _deltanet_kernel,
        grid=(batch, nt),
        in_specs=[cur(COL_DQ), cur(COL_DQ + 1), cur(COL_DQ + 2),
                  halo(COL_DQ), halo(COL_DQ + 1), halo(COL_DQ + 2),
                  cur(COL_DZ),
                  pl.BlockSpec((C, BA_LANES), lambda b, i: (b * nt + i, 0)),
                  pl.BlockSpec((3, DN_CONV, DN_W), lambda b, i: (0, 0, 0)),
                  pl.BlockSpec((1, BA_LANES), lambda b, i: (0, 0)),
                  pl.BlockSpec((1, BA_LANES), lambda b, i: (0, 0)),
                  pl.BlockSpec((1, DN_HD), lambda b, i: (0, 0))],
        out_specs=pl.BlockSpec((C, DN_W), lambda b, i: (b * nt + i, 0)),
        out_shape=jax.ShapeDtypeStruct((n, DN_W), BF16),
        scratch_shapes=[pltpu.VMEM((DN_HEADS, DN_HD, DN_HD), F32)],
        compiler_params=_cparams("parallel", "arbitrary"),
        name="deltanet",
    )(proj, proj, proj, proj, proj, proj, proj, ba, conv_w, alog_row, dtb_row, norm_g)


def _merge_kernel(x_ref, yp_ref, ym_ref, yd_ref, g0_ref, g1_ref, g2_ref, wp_ref, wm_ref, wd_ref, wo_ref, o_ref):
    merged = (g0_ref[...].astype(F32) * _dot(yp_ref[...], wp_ref[...])
              + g1_ref[...].astype(F32) * _dot(ym_ref[...], wm_ref[...])
              + g2_ref[...].astype(F32) * _dot(yd_ref[...], wd_ref[...]))
    o_ref[...] = x_ref[...] + _dot(merged.astype(BF16), wo_ref[...])


def _merge(x, y_pool, y_moba, y_dn, proj, w_up_pool, w_up_moba, w_up_dn, w_out, tm):
    n = x.shape[0]

    def rows(width, col=0):
        return pl.BlockSpec((tm, width), lambda i: (i, col))

    def whole(shape):
        return pl.BlockSpec(shape, lambda i: (0, 0))

    return pl.pallas_call(
        _merge_kernel,
        grid=(n // tm,),
        in_specs=[rows(D_MODEL), rows(POOL_W), rows(MOBA_AUG_W), rows(DN_W),
                  rows(D_MODEL, COL_GATE0), rows(D_MODEL, COL_GATE0 + 1), rows(D_MODEL, COL_GATE0 + 2),
                  whole((POOL_W, D_MODEL)), whole((MOBA_AUG_W, D_MODEL)), whole((DN_W, D_MODEL)),
                  whole((D_MODEL, D_MODEL))],
        out_specs=rows(D_MODEL),
        out_shape=jax.ShapeDtypeStruct((n, D_MODEL), F32),
        compiler_params=_cparams("parallel"),
        name="merge",
    )(x, y_pool, y_moba, y_dn, proj, proj, proj, w_up_pool, w_up_moba, w_up_dn, w_out)


def _memkv_kernel(mem_ref, g_ref, w_ref, o_ref):
    o_ref[...] = _dot(_rms(mem_ref[...], g_ref[...]).astype(BF16), w_ref[...]).astype(BF16)


def _memkv(mem2d, g, wkv):
    m = mem2d.shape[0]
    tm = 256
    return pl.pallas_call(
        _memkv_kernel,
        grid=(m // tm,),
        in_specs=[pl.BlockSpec((tm, D_MODEL), lambda i: (i, 0)),
                  pl.BlockSpec((1, D_MODEL), lambda i: (0, 0)),
                  pl.BlockSpec((D_MODEL, 2 * XA_W), lambda i: (0, 0))],
        out_specs=pl.BlockSpec((tm, 2 * XA_W), lambda i: (i, 0)),
        out_shape=jax.ShapeDtypeStruct((m, 2 * XA_W), BF16),
        compiler_params=_cparams("parallel"),
        name="memkv",
    )(mem2d, g, wkv)


def _xattn_kernel(x_ref, g_ref, wq_ref, kv_ref, wo_ref, o_ref):
    x = x_ref[...]
    q = _dot(_rms(x, g_ref[...]).astype(BF16), wq_ref[...]).astype(BF16)
    scale = XA_HD ** -0.5
    outs = []
    for h in range(XA_HEADS):
        sl = slice(h * XA_HD, (h + 1) * XA_HD)
        k = kv_ref[:, h * XA_HD:(h + 1) * XA_HD]
        v = kv_ref[:, XA_W + h * XA_HD:XA_W + (h + 1) * XA_HD]
        s = _dot_nt(q[:, sl], k) * scale
        s = s - jnp.max(s, axis=1, keepdims=True)
        p = jnp.exp(s)
        p = p / jnp.sum(p, axis=1, keepdims=True)
        outs.append(_dot(p.astype(BF16), v).astype(BF16))
    o = jnp.concatenate(outs, axis=1)
    o_ref[...] = x + _dot(o, wo_ref[...])


def _xattn(x, g, wq, kv, wo, seq, mem_len, tm):
    n = x.shape[0]
    tiles_per_seq = seq // tm
    return pl.pallas_call(
        _xattn_kernel,
        grid=(n // tm,),
        in_specs=[pl.BlockSpec((tm, D_MODEL), lambda i: (i, 0)),
                  pl.BlockSpec((1, D_MODEL), lambda i: (0, 0)),
                  pl.BlockSpec((D_MODEL, XA_W), lambda i: (0, 0)),
                  pl.BlockSpec((mem_len, 2 * XA_W), lambda i: (i // tiles_per_seq, 0)),
                  pl.BlockSpec((XA_W, D_MODEL), lambda i: (0, 0))],
        out_specs=pl.BlockSpec((tm, D_MODEL), lambda i: (i, 0)),
        out_shape=jax.ShapeDtypeStruct((n, D_MODEL), F32),
        compiler_params=_cparams("parallel"),
        name="xattn",
    )(x, g, wq, kv, wo)


def _swiglu_chunk(h, wg_ref, wu_ref, wd_ref):
    a = _silu(_dot(h, wg_ref[...])) * _dot(h, wu_ref[...])
    return _dot(a.astype(BF16), wd_ref[...])


def _ffn_kernel(x_ref, g_ref, wg_ref, wu_ref, wd_ref, o_ref, h_ref, acc_ref):
    j = pl.program_id(1)

    @pl.when(j == 0)
    def _():
        h_ref[...] = _rms(x_ref[...], g_ref[...]).astype(BF16)
        acc_ref[...] = x_ref[...]

    acc_ref[...] += _swiglu_chunk(h_ref[...], wg_ref, wu_ref, wd_ref)

    @pl.when(j == pl.num_programs(1) - 1)
    def _():
        o_ref[...] = acc_ref[...]


def _ffn(x, g, w_gate_up, w_down, tm, tf):
    n = x.shape[0]
    ff = w_down.shape[0]
    nf = ff // tf
    return pl.pallas_call(
        _ffn_kernel,
        grid=(n // tm, nf),
        in_specs=[pl.BlockSpec((tm, D_MODEL), lambda i, j: (i, 0)),
                  pl.BlockSpec((1, D_MODEL), lambda i, j: (0, 0)),
                  pl.BlockSpec((D_MODEL, tf), lambda i, j: (0, j)),
                  pl.BlockSpec((D_MODEL, tf), lambda i, j: (0, nf + j)),
                  pl.BlockSpec((tf, D_MODEL), lambda i, j: (j, 0))],
        out_specs=pl.BlockSpec((tm, D_MODEL), lambda i, j: (i, 0)),
        out_shape=jax.ShapeDtypeStruct((n, D_MODEL), F32),
        scratch_shapes=[pltpu.VMEM((tm, D_MODEL), BF16), pltpu.VMEM((tm, D_MODEL), F32)],
        compiler_params=_cparams("parallel", "arbitrary"),
        name="ffn",
    )(x, g, w_gate_up, w_gate_up, w_down)


def _moe_kernel(x_ref, g_ref, r_ref, wg_ref, wu_ref, wd_ref, fg_ref, o_ref, h_ref, comb_ref, acc_ref, *, tm):
    e = pl.program_id(1)
    lane = lax.broadcasted_iota(jnp.int32, (tm, 128), 1).astype(F32)

    @pl.when(e == 0)
    def _():
        h = _rms(x_ref[...], g_ref[...]).astype(BF16)
        h_ref[...] = h
        acc_ref[...] = x_ref[...]
        logits = jnp.where(lane < float(N_EXPERTS), _dot(h, r_ref[...]), -jnp.inf)
        m1 = jnp.max(logits, axis=1, keepdims=True)
        i1 = jnp.min(jnp.where(logits == m1, lane, 128.0), axis=1, keepdims=True)
        rest = jnp.where(lane == i1, -jnp.inf, logits)
        m2 = jnp.max(rest, axis=1, keepdims=True)
        i2 = jnp.min(jnp.where(rest == m2, lane, 128.0), axis=1, keepdims=True)
        e2 = jnp.exp(m2 - m1)
        w1 = 1.0 / (1.0 + e2)
        w2 = e2 / (1.0 + e2)
        comb_ref[...] = jnp.where(lane == i1, w1, 0.0) + jnp.where(lane == i2, w2, 0.0)

    w_e = jnp.sum(jnp.where(lane == e.astype(F32), comb_ref[...], 0.0), axis=1, keepdims=True)
    acc_ref[...] += w_e * _swiglu_chunk(h_ref[...], wg_ref.at[0], wu_ref.at[0], wd_ref.at[0])

    @pl.when(e == pl.num_programs(1) - 1)
    def _():
        o_ref[...] = _rms(acc_ref[...], fg_ref[...])


def _moe_final(x, g, router, w_gate_up, w_down, final_g, tm):
    n = x.shape[0]
    ff = w_down.shape[1]
    return pl.pallas_call(
        functools.partial(_moe_kernel, tm=tm),
        grid=(n // tm, N_EXPERTS),
        in_specs=[pl.BlockSpec((tm, D_MODEL), lambda i, e: (i, 0)),
                  pl.BlockSpec((1, D_MODEL), lambda i, e: (0, 0)),
                  pl.BlockSpec((D_MODEL, 128), lambda i, e: (0, 0)),
                  pl.BlockSpec((1, D_MODEL, ff), lambda i, e: (e, 0, 0)),
                  pl.BlockSpec((1, D_MODEL, ff), lambda i, e: (e, 0, 1)),
                  pl.BlockSpec((1, ff, D_MODEL), lambda i, e: (e, 0, 0)),
                  pl.BlockSpec((1, D_MODEL), lambda i, e: (0, 0))],
        out_specs=pl.BlockSpec((tm, D_MODEL), lambda i, e: (i, 0)),
        out_shape=jax.ShapeDtypeStruct((n, D_MODEL), F32),
        scratch_shapes=[pltpu.VMEM((tm, D_MODEL), BF16), pltpu.VMEM((tm, 128), F32),
                        pltpu.VMEM((tm, D_MODEL), F32)],
        compiler_params=_cparams("parallel", "arbitrary"),
        name="moe",
    )(x, g, router, w_gate_up, w_gate_up, w_down, final_g)


def _blockdiag(pool_w):
    g, c, _ = pool_w.shape
    out = jnp.zeros((g * c, g * c), pool_w.dtype)
    for gi in range(g):
        out = out.at[gi * c:(gi + 1) * c, gi * c:(gi + 1) * c].set(pool_w[gi])
    return out


def _lane_row(v, offset):
    return jnp.zeros((1, BA_LANES), F32).at[0, offset:offset + v.shape[0]].set(v.astype(F32))


def kernel(x, mem, positions, mix_norm_g, w_in, pool_w, pool_scale, dn_conv_w, dn_a_log, dn_dt_bias, dn_norm_g,
           w_up_pool, w_up_moba, w_up_dn, w_out, xa_norm_g, mem_norm_g, xa_wq, xa_wkv, xa_wo, ffn_norm_g,
           dense_w_gate_up, dense_w_down, moe_router, moe_w_gate_up, moe_w_down, final_norm_g):
    batch, seq, d = x.shape
    depth = w_in.shape[0]
    mem_len = mem.shape[1]
    n = batch * seq
    nb = seq // MOBA_BLOCK
    assert d == D_MODEL and seq % MOBA_BLOCK == 0 and nb <= MOBA_HD and nb % MOBA_GROUP == 0 and depth == 2
    tm = min(512, seq)

    xf = x.reshape(n, d)
    mem2d = mem.reshape(batch * mem_len, d)
    cos_t, sin_t = _rope_tables(positions.reshape(n, 1).astype(F32), tm)
    off_ba = POOL_W + 3 * MOBA_W + 3 * DN_W

    for layer in range(depth):
        wl = w_in[layer]
        w_main = jnp.concatenate([wl[:, :off_ba], wl[:, off_ba + 2 * DN_HEADS:]], axis=1).astype(BF16)
        w_ba = jnp.pad(wl[:, off_ba:off_ba + 2 * DN_HEADS], ((0, 0), (0, BA_LANES - 2 * DN_HEADS))).astype(BF16)
        proj, ba = _inproj(xf, mix_norm_g[layer][None, :], w_main, w_ba, min(1024, seq))

        y_pool = _pool(proj, _blockdiag(pool_w[layer]).astype(BF16), pool_scale[layer][None, :], batch, seq, tm)
        mq, mk, mv, kbar = _moba_prep(proj, cos_t, sin_t, nb)
        kbar = jnp.pad(kbar.reshape(batch, nb, MOBA_W), ((0, 0), (0, MOBA_HD - nb), (0, 0)))
        y_moba = _moba(mq, mk, mv, kbar, batch, seq)
        w_moba = jnp.pad(w_up_moba[layer].reshape(MOBA_HEADS, MOBA_HD, d), ((0, 0), (0, MOBA_HD), (0, 0)))
        w_moba = w_moba.reshape(MOBA_AUG_W, d).astype(BF16)
        conv_w = dn_conv_w[layer].reshape(DN_CONV, 3, DN_W).transpose(1, 0, 2)
        y_dn = _deltanet(proj, ba, conv_w, _lane_row(dn_a_log[layer], DN_HEADS),
                         _lane_row(dn_dt_bias[layer], DN_HEADS), dn_norm_g[layer][None, :], batch, seq)
        xf = _merge(xf, y_pool, y_moba, y_dn, proj, w_up_pool[layer].astype(BF16), w_moba,
                    w_up_dn[layer].astype(BF16), w_out[layer].astype(BF16), tm)

        kv = _memkv(mem2d, mem_norm_g[layer][None, :], xa_wkv[layer].astype(BF16))
        xf = _xattn(xf, xa_norm_g[layer][None, :], xa_wq[layer].astype(BF16), kv, xa_wo[layer].astype(BF16),
                    seq, mem_len, tm)

        if layer % 2 == 0:
            xf = _ffn(xf, ffn_norm_g[layer][None, :], dense_w_gate_up[layer // 2].astype(BF16),
                      dense_w_down[layer // 2].astype(BF16), tm, dense_w_down.shape[1] // 2)
        else:
            router = jnp.pad(moe_router[layer // 2], ((0, 0), (0, 128 - N_EXPERTS))).astype(BF16)
            xf = _moe_final(xf, ffn_norm_g[layer][None, :], router, moe_w_gate_up[layer // 2].astype(BF16),
                            moe_w_down[layer // 2].astype(BF16), final_norm_g[None, :], tm)
    return xf.reshape(batch, seq, d)
```

```python
import functools

import numpy as np
import jax
import jax.numpy as jnp
from jax import lax
from jax.experimental import pallas as pl
from jax.experimental.pallas import tpu as pltpu

F32 = jnp.float32
BF16 = jnp.bfloat16

RMS_EPS = 1e-6
D_MODEL = 1024
POOL_W = 256
POOL_GROUP_W = 64
POOL_WINDOWS = (2, 4, 8, 16)
SEQ_HALO = 16
MOBA_HEADS = 4
MOBA_HD = 64
MOBA_W = 256
MOBA_AUG_W = 2 * MOBA_W
MOBA_BLOCK = 256
MOBA_TOPK = 3
MOBA_GROUP = 4
ROPE_THETA = 500000.0
ROPE_DIMS = 16
DN_HEADS = 8
DN_HD = 64
DN_W = 512
DN_CONV = 4
DN_CHUNK = 128
DN_STEP_CHUNKS = 2
XA_HEADS = 4
XA_HD = 128
XA_W = 512
N_EXPERTS = 8
MOE_TOPK = 2
COL_POOL, COL_MQ, COL_MK, COL_MV = 0, 1, 2, 3
COL_DQ, COL_DZ = 2, 5
COL_GATE0 = 3
PROJ_COLS = 6144
PROJ_TN = 1024
PROJ_ID_BLOCKS = 3
BA_LANES = 128
VMEM_LIMIT = 56 * 1024 * 1024
NEG_BIG = -1e30


def _cparams(*sem):
    return pltpu.CompilerParams(dimension_semantics=sem, vmem_limit_bytes=VMEM_LIMIT)


def _rms(x, g):
    ms = jnp.mean(x * x, axis=-1, keepdims=True)
    return x * lax.rsqrt(ms + RMS_EPS) * g


def _silu(x):
    return x * jax.nn.sigmoid(x)


def _dot(a, b):
    return jnp.dot(a, b, preferred_element_type=F32)


def _dot_nt(a, b):
    return lax.dot_general(a, b, (((1,), (1,)), ((), ())), preferred_element_type=F32)


def _dot_tn(a, b):
    return lax.dot_general(a, b, (((0,), (0,)), ((), ())), preferred_element_type=F32)


def _inproj_kernel(x_ref, g_ref, w_ref, wba_ref, proj_ref, ba_ref, h_ref):
    j = pl.program_id(1)

    @pl.when(j == 0)
    def _():
        h = _rms(x_ref[...], g_ref[...]).astype(BF16)
        h_ref[...] = h
        ba_ref[...] = _dot(h, wba_ref[...])

    acc = _dot(h_ref[...], w_ref[...])

    @pl.when(j < PROJ_ID_BLOCKS)
    def _():
        proj_ref[...] = acc.astype(BF16)

    @pl.when(j >= PROJ_ID_BLOCKS)
    def _():
        proj_ref[...] = jax.nn.sigmoid(acc).astype(BF16)


def _inproj(x, g, w_main, w_ba, tm):
    n = x.shape[0]
    return pl.pallas_call(
        _inproj_kernel,
        grid=(n // tm, PROJ_COLS // PROJ_TN),
        in_specs=[
            pl.BlockSpec((tm, D_MODEL), lambda i, j: (i, 0)),
            pl.BlockSpec((1, D_MODEL), lambda i, j: (0, 0)),
            pl.BlockSpec((D_MODEL, PROJ_TN), lambda i, j: (0, j)),
            pl.BlockSpec((D_MODEL, BA_LANES), lambda i, j: (0, 0)),
        ],
        out_specs=[
            pl.BlockSpec((tm, PROJ_TN), lambda i, j: (i, j)),
            pl.BlockSpec((tm, BA_LANES), lambda i, j: (i, 0)),
        ],
        out_shape=[jax.ShapeDtypeStruct((n, PROJ_COLS), BF16),
                   jax.ShapeDtypeStruct((n, BA_LANES), F32)],
        scratch_shapes=[pltpu.VMEM((tm, D_MODEL), BF16)],
        compiler_params=_cparams("parallel", "arbitrary"),
        name="inproj",
    )(x, g, w_main, w_ba)


def _rope_table_kernel(pos_ref, freq_ref, sign_ref, c_ref, s_ref):
    ang = pos_ref[...] * freq_ref[...]
    rot = sign_ref[...]
    c_ref[...] = jnp.where(rot != 0.0, jnp.cos(ang), 1.0)
    s_ref[...] = jnp.sin(ang) * rot


def _rope_tables(pos_f32, tm):
    n = pos_f32.shape[0]
    half = ROPE_DIMS // 2
    inv_freq = np.power(ROPE_THETA, -np.arange(half, dtype=np.float32) * 2.0 / ROPE_DIMS).astype(np.float32)
    d = np.arange(128) % MOBA_HD
    freq = np.where(d < ROPE_DIMS, inv_freq[d % half], 0.0).astype(np.float32)[None, :]
    sign = np.where(d < half, -1.0, np.where(d < ROPE_DIMS, 1.0, 0.0)).astype(np.float32)[None, :]
    return pl.pallas_call(
        _rope_table_kernel,
        grid=(n // tm,),
        in_specs=[pl.BlockSpec((tm, 1), lambda i: (i, 0)),
                  pl.BlockSpec((1, 128), lambda i: (0, 0)),
                  pl.BlockSpec((1, 128), lambda i: (0, 0))],
        out_specs=[pl.BlockSpec((tm, 128), lambda i: (i, 0)),
                   pl.BlockSpec((tm, 128), lambda i: (i, 0))],
        out_shape=[jax.ShapeDtypeStruct((n, 128), F32), jax.ShapeDtypeStruct((n, 128), F32)],
        compiler_params=_cparams("parallel"),
        name="rope_tables",
    )(pos_f32, jnp.asarray(freq), jnp.asarray(sign))


def _moba_prep_kernel(q_ref, k_ref, v_ref, c_ref, s_ref, qo_ref, ko_ref, vo_ref, kbar_ref, *, nb):
    c = jnp.concatenate([c_ref[...], c_ref[...]], axis=1)
    s = jnp.concatenate([s_ref[...], s_ref[...]], axis=1)
    lane = lax.broadcasted_iota(jnp.int32, (MOBA_BLOCK, MOBA_W), 1)
    first_half = (lane % MOBA_HD) < (ROPE_DIMS // 2)

    def rope(x):
        partner = jnp.where(first_half,
                            pltpu.roll(x, MOBA_W - ROPE_DIMS // 2, 1),
                            pltpu.roll(x, ROPE_DIMS // 2, 1))
        return x * c + partner * s

    qo_ref[...] = (rope(q_ref[...].astype(F32)) * (MOBA_HD ** -0.5)).astype(BF16)
    k = rope(k_ref[...].astype(F32))
    kbar_ref[0] = jnp.mean(k, axis=0, keepdims=True)
    k16 = k.astype(BF16)
    v16 = v_ref[...]
    blk_lane = lax.broadcasted_iota(jnp.int32, (MOBA_BLOCK, MOBA_HD), 1)
    onehot = jnp.where(blk_lane == pl.program_id(0) % nb, 1.0, 0.0).astype(BF16)
    zeros = jnp.zeros((MOBA_BLOCK, MOBA_HD), BF16)
    k_parts, v_parts = [], []
    for h in range(MOBA_HEADS):
        sl = slice(h * MOBA_HD, (h + 1) * MOBA_HD)
        k_parts += [k16[:, sl], onehot]
        v_parts += [v16[:, sl], zeros]
    ko_ref[...] = jnp.concatenate(k_parts, axis=1)
    vo_ref[...] = jnp.concatenate(v_parts, axis=1)


def _moba_prep(proj, cos_t, sin_t, nb):
    n = proj.shape[0]
    nblk = n // MOBA_BLOCK
    return pl.pallas_call(
        functools.partial(_moba_prep_kernel, nb=nb),
        grid=(nblk,),
        in_specs=[pl.BlockSpec((MOBA_BLOCK, MOBA_W), lambda r: (r, COL_MQ)),
                  pl.BlockSpec((MOBA_BLOCK, MOBA_W), lambda r: (r, COL_MK)),
                  pl.BlockSpec((MOBA_BLOCK, MOBA_W), lambda r: (r, COL_MV)),
                  pl.BlockSpec((MOBA_BLOCK, 128), lambda r: (r, 0)),
                  pl.BlockSpec((MOBA_BLOCK, 128), lambda r: (r, 0))],
        out_specs=[pl.BlockSpec((MOBA_BLOCK, MOBA_W), lambda r: (r, 0)),
                   pl.BlockSpec((MOBA_BLOCK, MOBA_AUG_W), lambda r: (r, 0)),
                   pl.BlockSpec((MOBA_BLOCK, MOBA_AUG_W), lambda r: (r, 0)),
                   pl.BlockSpec((1, 1, MOBA_W), lambda r: (r, 0, 0))],
        out_shape=[jax.ShapeDtypeStruct((n, MOBA_W), BF16),
                   jax.ShapeDtypeStruct((n, MOBA_AUG_W), BF16),
                   jax.ShapeDtypeStruct((n, MOBA_AUG_W), BF16),
                   jax.ShapeDtypeStruct((nblk, 1, MOBA_W), F32)],
        compiler_params=_cparams("parallel"),
        name="moba_prep",
    )(proj, proj, proj, cos_t, sin_t)


def _moba_kernel(q_ref, k_ref, v_ref, kbar_ref, o_ref, qa_ref, m_ref, ls_ref, acc_ref, s_ref):
    i = pl.program_id(1)
    heads = range(MOBA_HEADS)
    blk = lax.broadcasted_iota(jnp.int32, (MOBA_BLOCK, MOBA_HD), 1).astype(F32)
    row = lax.broadcasted_iota(jnp.int32, (MOBA_BLOCK, MOBA_BLOCK), 0)
    col = lax.broadcasted_iota(jnp.int32, (MOBA_BLOCK, MOBA_BLOCK), 1)
    causal = col <= row
    i_f = i.astype(F32)
    own = pl.multiple_of(i * MOBA_BLOCK, MOBA_BLOCK)
    hs = [slice(h * 2 * MOBA_HD, (h + 1) * 2 * MOBA_HD) for h in heads]

    for h in heads:
        sl = slice(h * MOBA_HD, (h + 1) * MOBA_HD)
        qh = q_ref[:, sl]
        gate = _dot_nt(qh, kbar_ref[0, :, sl].astype(BF16))
        gate = jnp.where(blk < i_f, gate, -jnp.inf)
        keep = jnp.zeros((MOBA_BLOCK, MOBA_HD), jnp.bool_)
        for _ in range(MOBA_TOPK):
            m = jnp.max(gate, axis=1, keepdims=True)
            idx = jnp.min(jnp.where(gate == m, blk, float(MOBA_HD)), axis=1, keepdims=True)
            pick = (blk == idx) & (m > -jnp.inf)
            keep = keep | pick
            gate = jnp.where(pick, -jnp.inf, gate)
        qa_ref[h] = jnp.concatenate([qh, jnp.where(keep, 0.0, NEG_BIG).astype(BF16)], axis=1)
        qa_ref[MOBA_HEADS + h] = jnp.concatenate([qh, jnp.zeros_like(qh)], axis=1)

    def scores(h, start):
        return _dot_nt(qa_ref[h], k_ref[pl.ds(start, MOBA_BLOCK), hs[h]])

    def own_scores(h):
        s = _dot_nt(qa_ref[MOBA_HEADS + h], k_ref[pl.ds(own, MOBA_BLOCK), hs[h]])
        return jnp.where(causal, s, NEG_BIG)

    def fold(x):
        return x[:, :128], x[:, 128:]

    n_groups = (i + (MOBA_GROUP - 1)) // MOBA_GROUP

    def group_blocks(g):
        return [pl.multiple_of((g * MOBA_GROUP + t) * MOBA_BLOCK, MOBA_BLOCK) for t in range(MOBA_GROUP)]

    for h in heads:
        s = own_scores(h)
        m = jnp.max(s, axis=1, keepdims=True)
        a, b = fold(jnp.exp(s - m))
        ls_ref[h] = a + b
        acc_ref[h] = _dot(jnp.concatenate([a, b], axis=1).astype(BF16), v_ref[pl.ds(own, MOBA_BLOCK), hs[h]])
        m_ref[h] = jnp.broadcast_to(m, (MOBA_BLOCK, 128))

    @pl.loop(0, n_groups)
    def _(g):
        starts = group_blocks(g)
        gmax = [None] * MOBA_HEADS
        for t, start in enumerate(starts):
            for h in heads:
                s = scores(h, start)
                s_ref[t, h] = s
                a, b = fold(s)
                ab = jnp.maximum(a, b)
                gmax[h] = ab if gmax[h] is None else jnp.maximum(gmax[h], ab)
        m_new = []
        for h in heads:
            m_old = m_ref[h]
            m_new.append(jnp.maximum(m_old, jnp.max(gmax[h], axis=1, keepdims=True)))
            alpha = jnp.exp(m_old - m_new[h])
            ls_ref[h] = ls_ref[h] * alpha
            acc_ref[h] = acc_ref[h] * alpha
            m_ref[h] = m_new[h]
        for t, start in enumerate(starts):
            for h in heads:
                a = jnp.exp(s_ref[t, h, :, :128] - m_new[h])
                b = jnp.exp(s_ref[t, h, :, 128:] - m_new[h])
                ls_ref[h] += a + b
                acc_ref[h] += _dot(jnp.concatenate([a, b], axis=1).astype(BF16),
                                   v_ref[pl.ds(start, MOBA_BLOCK), hs[h]])

    for h in heads:
        o_ref[:, hs[h]] = (acc_ref[h] / jnp.sum(ls_ref[h], axis=1, keepdims=True)).astype(BF16)


def _moba(q, k_aug, v_aug, kbar, batch, seq):
    n = q.shape[0]
    nb = seq // MOBA_BLOCK
    return pl.pallas_call(
        _moba_kernel,
        grid=(batch, nb),
        in_specs=[pl.BlockSpec((MOBA_BLOCK, MOBA_W), lambda b, i: (b * nb + i, 0)),
                  pl.BlockSpec((seq, MOBA_AUG_W), lambda b, i: (b, 0)),
                  pl.BlockSpec((seq, MOBA_AUG_W), lambda b, i: (b, 0)),
                  pl.BlockSpec((1, MOBA_HD, MOBA_W), lambda b, i: (b, 0, 0))],
        out_specs=pl.BlockSpec((MOBA_BLOCK, MOBA_AUG_W), lambda b, i: (b * nb + i, 0)),
        out_shape=jax.ShapeDtypeStruct((n, MOBA_AUG_W), BF16),
        scratch_shapes=[pltpu.VMEM((2 * MOBA_HEADS, MOBA_BLOCK, 2 * MOBA_HD), BF16)]
        + [pltpu.VMEM((MOBA_HEADS, MOBA_BLOCK, 2 * MOBA_HD), F32)] * 3
        + [pltpu.VMEM((MOBA_GROUP, MOBA_HEADS, MOBA_BLOCK, MOBA_BLOCK), F32)],
        compiler_params=_cparams("parallel", "arbitrary"),
        name="moba",
    )(q, k_aug, v_aug, kbar)


def _pool_kernel(p_ref, halo_ref, w_ref, scale_ref, o_ref, *, ts):
    i = pl.program_id(1)
    halo = jnp.where(i == 0, 0.0, halo_ref[...].astype(F32))
    p = p_ref[...].astype(F32)
    cur = jnp.concatenate([halo, p], axis=0)
    lane = lax.broadcasted_iota(jnp.int32, (ts, POOL_W), 1)
    t1 = (lax.broadcasted_iota(jnp.int32, (ts, POOL_W), 0) + i * ts + 1).astype(F32)
    total = jnp.zeros((ts, POOL_W), F32)
    count = jnp.ones((ts, POOL_W), F32)
    span = 1
    for gi, w in enumerate(POOL_WINDOWS):
        while span < w:
            cur = cur + pltpu.roll(cur, span, 0)
            span *= 2
        in_group = (lane >= gi * POOL_GROUP_W) & (lane < (gi + 1) * POOL_GROUP_W)
        total = jnp.where(in_group, cur[SEQ_HALO:], total)
        count = jnp.where(in_group, jnp.minimum(t1, float(w)), count)
    pooled = total / count - p
    o_ref[...] = (_dot(pooled.astype(BF16), w_ref[...]) * scale_ref[...]).astype(BF16)


def _pool(proj, w_blockdiag, scale, batch, seq, ts):
    n = proj.shape[0]
    nt = seq // ts
    hb = ts // SEQ_HALO
    return pl.pallas_call(
        functools.partial(_pool_kernel, ts=ts),
        grid=(batch, nt),
        in_specs=[pl.BlockSpec((ts, POOL_W), lambda b, i: (b * nt + i, COL_POOL)),
                  pl.BlockSpec((SEQ_HALO, POOL_W), lambda b, i: (jnp.maximum((b * nt + i) * hb - 1, 0), COL_POOL)),
                  pl.BlockSpec((POOL_W, POOL_W), lambda b, i: (0, 0)),
                  pl.BlockSpec((1, POOL_W), lambda b, i: (0, 0))],
        out_specs=pl.BlockSpec((ts, POOL_W), lambda b, i: (b * nt + i, 0)),
        out_shape=jax.ShapeDtypeStruct((n, POOL_W), BF16),
        compiler_params=_cparams("parallel", "parallel"),
        name="pool",
    )(proj, proj, w_blockdiag, scale)


def _deltanet_kernel(q_ref, k_ref, v_ref, qh_ref, kh_ref, vh_ref, z_ref, ba_ref, cw_ref, alog_ref, dtb_ref,
                     ng_ref, expb_ref, expg_ref, o_ref, state_ref, oraw_ref):
    i = pl.program_id(1)
    C = DN_CHUNK
    TS = DN_STEP_CHUNKS * C

    @pl.when(i == 0)
    def _():
        state_ref[...] = jnp.zeros_like(state_ref)

    def conv(x_ref, halo_ref, w):
        halo = jnp.where(i == 0, 0.0, halo_ref[...].astype(F32))
        ext = jnp.concatenate([halo, x_ref[...].astype(F32)], axis=0)
        y = ext * w[DN_CONV - 1:DN_CONV]
        for lag in range(1, DN_CONV):
            y = y + pltpu.roll(ext, lag, 0) * w[DN_CONV - 1 - lag:DN_CONV - lag]
        return _silu(y[SEQ_HALO:])

    cw = cw_ref[...]
    qc = conv(q_ref, qh_ref, cw[0])
    kc = conv(k_ref, kh_ref, cw[1])
    vc = conv(v_ref, vh_ref, cw[2])

    ba = ba_ref[...]
    beta = jax.nn.sigmoid(ba)
    g = -jnp.exp(alog_ref[...]) * jax.nn.softplus(ba + dtb_ref[...])
    rows = lax.broadcasted_iota(jnp.int32, (TS, BA_LANES), 0) & (C - 1)
    G = g
    span = 1
    while span < C:
        G = G + jnp.where(rows >= span, pltpu.roll(G, span, 0), 0.0)
        span *= 2
    GT = G.T
    r_i = lax.broadcasted_iota(jnp.int32, (C, C), 0)
    c_i = lax.broadcasted_iota(jnp.int32, (C, C), 1)
    tril = c_i <= r_i
    eye = (c_i == r_i).astype(F32)
    levels = C.bit_length() - 1
    level_masks = [((r_i >> k) & 1 == 1) & ((c_i >> k) == (r_i >> k) - 1) for k in range(levels)]
    expb = expb_ref[...]
    expg = expg_ref[...]

    def per_head(x, e):
        hi = x.astype(BF16)
        lo = (x - hi.astype(F32)).astype(BF16)
        return _dot(hi, e) + _dot(lo, e)

    def inv_norm(x):
        ss = _dot_nt((x * x).astype(BF16), expb)
        return per_head(lax.rsqrt(ss + RMS_EPS), expb)

    qn_all = qc * (inv_norm(qc) * (DN_HD ** -0.5))
    kn_all = kc * inv_norm(kc)
    beta_all = per_head(beta, expb)
    eG_all = per_head(jnp.exp(G), expg)
    G_last_rows = jnp.concatenate(
        [jnp.broadcast_to(G[(c + 1) * C - 1:(c + 1) * C, :], (C, BA_LANES)) for c in range(DN_STEP_CHUNKS)], axis=0)
    kb_all = kn_all * beta_all
    vb_all = vc * beta_all
    kbe_all = kb_all * eG_all
    qn16 = qn_all.astype(BF16)
    kn16 = kn_all.astype(BF16)
    kb16 = kb_all.astype(BF16)
    qe16 = (qn_all * eG_all).astype(BF16)
    kdec16 = (kn_all * per_head(jnp.exp(G_last_rows - G), expg)).astype(BF16)

    heads = range(DN_HEADS)
    units = [(c, h) for c in range(DN_STEP_CHUNKS) for h in heads]
    rs = [slice(c * C, (c + 1) * C) for c in range(DN_STEP_CHUNKS)]
    sls = [slice(h * DN_HD, (h + 1) * DN_HD) for h in heads]
    G_c, A, aqk, Z = {}, {}, {}, {}
    for u in units:
        c, h = u
        G_c[u] = G[rs[c], DN_HEADS + h:DN_HEADS + h + 1]
        gram = _dot_nt(jnp.concatenate([kb16[rs[c], sls[h]], qn16[rs[c], sls[h]]], axis=0), kn16[rs[c], sls[h]])
        G_r = GT[DN_HEADS + h:DN_HEADS + h + 1, rs[c]]
        decay = jnp.exp(jnp.where(tril, G_c[u] - G_r, -jnp.inf))
        A[u] = (gram[:C] * decay).astype(BF16)
        aqk[u] = (gram[C:] * decay).astype(BF16)
    for u in units:
        c, h = u
        X = jnp.concatenate([vb_all[rs[c], sls[h]], kbe_all[rs[c], sls[h]]], axis=1)
        L1 = jnp.where(level_masks[0], A[u], 0.0)
        Z[u] = jnp.concatenate([eye - L1.astype(F32), X - _dot(L1, X.astype(BF16))], axis=1)
    for lvl in range(1, levels):
        Z16 = {u: Z[u].astype(BF16) for u in units}
        cols = slice(0, C + 2 * DN_HD) if lvl < levels - 1 else slice(C, C + 2 * DN_HD)
        Y = {u: _dot(jnp.where(level_masks[lvl], A[u], 0.0), Z16[u][:, cols]).astype(BF16) for u in units}
        Z = {u: Z[u][:, cols] - _dot(Z16[u][:, :C], Y[u]) for u in units}
    S = [state_ref[h] for h in heads]
    for c in range(DN_STEP_CHUNKS):
        us = [(c, h) for h in heads]
        ws = [_dot(jnp.concatenate([Z[u][:, DN_HD:].astype(BF16), qe16[rs[c], sls[u[1]]]], axis=0),
                   S[u[1]].astype(BF16)) for u in us]
        v16 = [(Z[u][:, :DN_HD] - ws[u[1]][:C]).astype(BF16) for u in us]
        for u in us:
            h = u[1]
            G_last = G_c[u][C - 1:C, :]
            S[h] = S[h] * jnp.exp(G_last) + _dot_tn(kdec16[rs[c], sls[h]], v16[h])
        for u in us:
            h = u[1]
            oraw_ref[rs[c], sls[h]] = ws[h][C:] + _dot(aqk[u], v16[h])
    for h in heads:
        state_ref[h] = S[h]
    o = oraw_ref[...]
    ms = _dot_nt((o * o).astype(BF16), expb) * (1.0 / DN_HD)
    o = o * per_head(lax.rsqrt(ms + RMS_EPS), expb) * ng_ref[...]
    o_ref[...] = (o * _silu(z_ref[...].astype(F32))).astype(BF16)


def _deltanet(proj, ba, conv_w, alog_row, dtb_row, norm_g, batch, seq):
    n = proj.shape[0]
    C = DN_STEP_CHUNKS * DN_CHUNK
    nt = seq // C
    hb = C // SEQ_HALO

    def head_lanes(first_row):
        m = np.zeros((BA_LANES, DN_W), np.float32)
        for h in range(DN_HEADS):
            m[first_row + h, h * DN_HD:(h + 1) * DN_HD] = 1.0
        return m

    def cur(col):
        return pl.BlockSpec((C, DN_W), lambda b, i: (b * nt + i, col))

    def halo(col):
        return pl.BlockSpec((SEQ_HALO, DN_W), lambda b, i: (jnp.maximum((b * nt + i) * hb - 1, 0), col))

    return pl.pallas_call(
        _deltanet_kernel,
        grid=(batch, nt),
        in_specs=[cur(COL_DQ), cur(COL_DQ + 1), cur(COL_DQ + 2),
                  halo(COL_DQ), halo(COL_DQ + 1), halo(COL_DQ + 2),
                  cur(COL_DZ),
                  pl.BlockSpec((C, BA_LANES), lambda b, i: (b * nt + i, 0)),
                  pl.BlockSpec((3, DN_CONV, DN_W), lambda b, i: (0, 0, 0)),
                  pl.BlockSpec((1, BA_LANES), lambda b, i: (0, 0)),
                  pl.BlockSpec((1, BA_LANES), lambda b, i: (0, 0)),
                  pl.BlockSpec((1, DN_W), lambda b, i: (0, 0)),
                  pl.BlockSpec((BA_LANES, DN_W), lambda b, i: (0, 0)),
                  pl.BlockSpec((BA_LANES, DN_W), lambda b, i: (0, 0))],
        out_specs=pl.BlockSpec((C, DN_W), lambda b, i: (b * nt + i, 0)),
        out_shape=jax.ShapeDtypeStruct((n, DN_W), BF16),
        scratch_shapes=[pltpu.VMEM((DN_HEADS, DN_HD, DN_HD), F32), pltpu.VMEM((C, DN_W), F32)],
        compiler_params=_cparams("parallel", "arbitrary"),
        name="deltanet",
    )(proj, proj, proj, proj, proj, proj, proj, ba, conv_w, alog_row, dtb_row, norm_g,
      jnp.asarray(head_lanes(0), BF16), jnp.asarray(head_lanes(DN_HEADS), BF16))


def _merge_kernel(x_ref, yp_ref, ym_ref, yd_ref, g0_ref, g1_ref, g2_ref, wp_ref, wm_ref, wd_ref, wo_ref, o_ref):
    merged = (g0_ref[...].astype(F32) * _dot(yp_ref[...], wp_ref[...])
              + g1_ref[...].astype(F32) * _dot(ym_ref[...], wm_ref[...])
              + g2_ref[...].astype(F32) * _dot(yd_ref[...], wd_ref[...]))
    o_ref[...] = x_ref[...] + _dot(merged.astype(BF16), wo_ref[...])


def _merge(x, y_pool, y_moba, y_dn, proj, w_up_pool, w_up_moba, w_up_dn, w_out, tm):
    n = x.shape[0]

    def rows(width, col=0):
        return pl.BlockSpec((tm, width), lambda i: (i, col))

    def whole(shape):
        return pl.BlockSpec(shape, lambda i: (0, 0))

    return pl.pallas_call(
        _merge_kernel,
        grid=(n // tm,),
        in_specs=[rows(D_MODEL), rows(POOL_W), rows(MOBA_AUG_W), rows(DN_W),
                  rows(D_MODEL, COL_GATE0), rows(D_MODEL, COL_GATE0 + 1), rows(D_MODEL, COL_GATE0 + 2),
                  whole((POOL_W, D_MODEL)), whole((MOBA_AUG_W, D_MODEL)), whole((DN_W, D_MODEL)),
                  whole((D_MODEL, D_MODEL))],
        out_specs=rows(D_MODEL),
        out_shape=jax.ShapeDtypeStruct((n, D_MODEL), F32),
        compiler_params=_cparams("parallel"),
        name="merge",
    )(x, y_pool, y_moba, y_dn, proj, proj, proj, w_up_pool, w_up_moba, w_up_dn, w_out)


def _memkv_kernel(mem_ref, g_ref, w_ref, o_ref):
    o_ref[...] = _dot(_rms(mem_ref[...], g_ref[...]).astype(BF16), w_ref[...]).astype(BF16)


def _memkv(mem2d, g, wkv):
    m = mem2d.shape[0]
    tm = 256
    return pl.pallas_call(
        _memkv_kernel,
        grid=(m // tm,),
        in_specs=[pl.BlockSpec((tm, D_MODEL), lambda i: (i, 0)),
                  pl.BlockSpec((1, D_MODEL), lambda i: (0, 0)),
                  pl.BlockSpec((D_MODEL, 2 * XA_W), lambda i: (0, 0))],
        out_specs=pl.BlockSpec((tm, 2 * XA_W), lambda i: (i, 0)),
        out_shape=jax.ShapeDtypeStruct((m, 2 * XA_W), BF16),
        compiler_params=_cparams("parallel"),
        name="memkv",
    )(mem2d, g, wkv)


def _xattn_kernel(x_ref, g_ref, wq_ref, kv_ref, wo_ref, o_ref):
    x = x_ref[...]
    q = _dot(_rms(x, g_ref[...]).astype(BF16), wq_ref[...]).astype(BF16)
    scale = XA_HD ** -0.5
    outs = []
    for h in range(XA_HEADS):
        sl = slice(h * XA_HD, (h + 1) * XA_HD)
        k = kv_ref[:, h * XA_HD:(h + 1) * XA_HD]
        v = kv_ref[:, XA_W + h * XA_HD:XA_W + (h + 1) * XA_HD]
        s = _dot_nt(q[:, sl], k) * scale
        s = s - jnp.max(s, axis=1, keepdims=True)
        p = jnp.exp(s)
        p = p / jnp.sum(p, axis=1, keepdims=True)
        outs.append(_dot(p.astype(BF16), v).astype(BF16))
    o = jnp.concatenate(outs, axis=1)
    o_ref[...] = x + _dot(o, wo_ref[...])


def _xattn(x, g, wq, kv, wo, seq, mem_len, tm):
    n = x.shape[0]
    tiles_per_seq = seq // tm
    return pl.pallas_call(
        _xattn_kernel,
        grid=(n // tm,),
        in_specs=[pl.BlockSpec((tm, D_MODEL), lambda i: (i, 0)),
                  pl.BlockSpec((1, D_MODEL), lambda i: (0, 0)),
                  pl.BlockSpec((D_MODEL, XA_W), lambda i: (0, 0)),
                  pl.BlockSpec((mem_len, 2 * XA_W), lambda i: (i // tiles_per_seq, 0)),
                  pl.BlockSpec((XA_W, D_MODEL), lambda i: (0, 0))],
        out_specs=pl.BlockSpec((tm, D_MODEL), lambda i: (i, 0)),
        out_shape=jax.ShapeDtypeStruct((n, D_MODEL), F32),
        compiler_params=_cparams("parallel"),
        name="xattn",
    )(x, g, wq, kv, wo)


def _swiglu_chunk(h, wg_ref, wu_ref, wd_ref):
    a = _silu(_dot(h, wg_ref[...])) * _dot(h, wu_ref[...])
    return _dot(a.astype(BF16), wd_ref[...])


def _ffn_kernel(x_ref, g_ref, wg_ref, wu_ref, wd_ref, o_ref, h_ref, acc_ref):
    j = pl.program_id(1)

    @pl.when(j == 0)
    def _():
        h_ref[...] = _rms(x_ref[...], g_ref[...]).astype(BF16)
        acc_ref[...] = x_ref[...]

    acc_ref[...] += _swiglu_chunk(h_ref[...], wg_ref, wu_ref, wd_ref)

    @pl.when(j == pl.num_programs(1) - 1)
    def _():
        o_ref[...] = acc_ref[...]


def _ffn(x, g, w_gate_up, w_down, tm, tf):
    n = x.shape[0]
    ff = w_down.shape[0]
    nf = ff // tf
    return pl.pallas_call(
        _ffn_kernel,
        grid=(n // tm, nf),
        in_specs=[pl.BlockSpec((tm, D_MODEL), lambda i, j: (i, 0)),
                  pl.BlockSpec((1, D_MODEL), lambda i, j: (0, 0)),
                  pl.BlockSpec((D_MODEL, tf), lambda i, j: (0, j)),
                  pl.BlockSpec((D_MODEL, tf), lambda i, j: (0, nf + j)),
                  pl.BlockSpec((tf, D_MODEL), lambda i, j: (j, 0))],
        out_specs=pl.BlockSpec((tm, D_MODEL), lambda i, j: (i, 0)),
        out_shape=jax.ShapeDtypeStruct((n, D_MODEL), F32),
        scratch_shapes=[pltpu.VMEM((tm, D_MODEL), BF16), pltpu.VMEM((tm, D_MODEL), F32)],
        compiler_params=_cparams("parallel", "arbitrary"),
        name="ffn",
    )(x, g, w_gate_up, w_gate_up, w_down)


def _moe_kernel(x_ref, g_ref, r_ref, wg_ref, wu_ref, wd_ref, fg_ref, o_ref, h_ref, comb_ref, acc_ref, *, tm):
    e = pl.program_id(1)
    lane = lax.broadcasted_iota(jnp.int32, (tm, 128), 1).astype(F32)

    @pl.when(e == 0)
    def _():
        h = _rms(x_ref[...], g_ref[...]).astype(BF16)
        h_ref[...] = h
        acc_ref[...] = x_ref[...]
        logits = jnp.where(lane < float(N_EXPERTS), _dot(h, r_ref[...]), -jnp.inf)
        m1 = jnp.max(logits, axis=1, keepdims=True)
        i1 = jnp.min(jnp.where(logits == m1, lane, 128.0), axis=1, keepdims=True)
        rest = jnp.where(lane == i1, -jnp.inf, logits)
        m2 = jnp.max(rest, axis=1, keepdims=True)
        i2 = jnp.min(jnp.where(rest == m2, lane, 128.0), axis=1, keepdims=True)
        e2 = jnp.exp(m2 - m1)
        w1 = 1.0 / (1.0 + e2)
        w2 = e2 / (1.0 + e2)
        comb_ref[...] = jnp.where(lane == i1, w1, 0.0) + jnp.where(lane == i2, w2, 0.0)

    w_e = jnp.sum(jnp.where(lane == e.astype(F32), comb_ref[...], 0.0), axis=1, keepdims=True)
    acc_ref[...] += w_e * _swiglu_chunk(h_ref[...], wg_ref.at[0], wu_ref.at[0], wd_ref.at[0])

    @pl.when(e == pl.num_programs(1) - 1)
    def _():
        o_ref[...] = _rms(acc_ref[...], fg_ref[...])


def _moe_final(x, g, router, w_gate_up, w_down, final_g, tm):
    n = x.shape[0]
    ff = w_down.shape[1]
    return pl.pallas_call(
        functools.partial(_moe_kernel, tm=tm),
        grid=(n // tm, N_EXPERTS),
        in_specs=[pl.BlockSpec((tm, D_MODEL), lambda i, e: (i, 0)),
                  pl.BlockSpec((1, D_MODEL), lambda i, e: (0, 0)),
                  pl.BlockSpec((D_MODEL, 128), lambda i, e: (0, 0)),
                  pl.BlockSpec((1, D_MODEL, ff), lambda i, e: (e, 0, 0)),
                  pl.BlockSpec((1, D_MODEL, ff), lambda i, e: (e, 0, 1)),
                  pl.BlockSpec((1, ff, D_MODEL), lambda i, e: (e, 0, 0)),
                  pl.BlockSpec((1, D_MODEL), lambda i, e: (0, 0))],
        out_specs=pl.BlockSpec((tm, D_MODEL), lambda i, e: (i, 0)),
        out_shape=jax.ShapeDtypeStruct((n, D_MODEL), F32),
        scratch_shapes=[pltpu.VMEM((tm, D_MODEL), BF16), pltpu.VMEM((tm, 128), F32),
                        pltpu.VMEM((tm, D_MODEL), F32)],
        compiler_params=_cparams("parallel", "arbitrary"),
        name="moe",
    )(x, g, router, w_gate_up, w_gate_up, w_down, final_g)


def _blockdiag(pool_w):
    g, c, _ = pool_w.shape
    out = jnp.zeros((g * c, g * c), pool_w.dtype)
    for gi in range(g):
        out = out.at[gi * c:(gi + 1) * c, gi * c:(gi + 1) * c].set(pool_w[gi])
    return out


def _lane_row(v, offset):
    return jnp.zeros((1, BA_LANES), F32).at[0, offset:offset + v.shape[0]].set(v.astype(F32))


def kernel(x, mem, positions, mix_norm_g, w_in, pool_w, pool_scale, dn_conv_w, dn_a_log, dn_dt_bias, dn_norm_g,
           w_up_pool, w_up_moba, w_up_dn, w_out, xa_norm_g, mem_norm_g, xa_wq, xa_wkv, xa_wo, ffn_norm_g,
           dense_w_gate_up, dense_w_down, moe_router, moe_w_gate_up, moe_w_down, final_norm_g):
    batch, seq, d = x.shape
    depth = w_in.shape[0]
    mem_len = mem.shape[1]
    n = batch * seq
    nb = seq // MOBA_BLOCK
    assert d == D_MODEL and seq % MOBA_BLOCK == 0 and nb <= MOBA_HD and nb % MOBA_GROUP == 0 and depth == 2
    tm = min(512, seq)

    xf = x.reshape(n, d)
    mem2d = mem.reshape(batch * mem_len, d)
    cos_t, sin_t = _rope_tables(positions.reshape(n, 1).astype(F32), tm)
    off_ba = POOL_W + 3 * MOBA_W + 3 * DN_W

    for layer in range(depth):
        wl = w_in[layer]
        w_main = jnp.concatenate([wl[:, :off_ba], wl[:, off_ba + 2 * DN_HEADS:]], axis=1).astype(BF16)
        w_ba = jnp.pad(wl[:, off_ba:off_ba + 2 * DN_HEADS], ((0, 0), (0, BA_LANES - 2 * DN_HEADS))).astype(BF16)
        proj, ba = _inproj(xf, mix_norm_g[layer][None, :], w_main, w_ba, min(1024, seq))

        y_pool = _pool(proj, _blockdiag(pool_w[layer]).astype(BF16), pool_scale[layer][None, :], batch, seq, tm)
        mq, mk, mv, kbar = _moba_prep(proj, cos_t, sin_t, nb)
        kbar = jnp.pad(kbar.reshape(batch, nb, MOBA_W), ((0, 0), (0, MOBA_HD - nb), (0, 0)))
        y_moba = _moba(mq, mk, mv, kbar, batch, seq)
        w_moba = jnp.pad(w_up_moba[layer].reshape(MOBA_HEADS, MOBA_HD, d), ((0, 0), (0, MOBA_HD), (0, 0)))
        w_moba = w_moba.reshape(MOBA_AUG_W, d).astype(BF16)
        conv_w = dn_conv_w[layer].reshape(DN_CONV, 3, DN_W).transpose(1, 0, 2)
        y_dn = _deltanet(proj, ba, conv_w, _lane_row(dn_a_log[layer], DN_HEADS),
                         _lane_row(dn_dt_bias[layer], DN_HEADS), jnp.tile(dn_norm_g[layer], DN_HEADS)[None, :], batch, seq)
        xf = _merge(xf, y_pool, y_moba, y_dn, proj, w_up_pool[layer].astype(BF16), w_moba,
                    w_up_dn[layer].astype(BF16), w_out[layer].astype(BF16), tm)

        kv = _memkv(mem2d, mem_norm_g[layer][None, :], xa_wkv[layer].astype(BF16))
        xf = _xattn(xf, xa_norm_g[layer][None, :], xa_wq[layer].astype(BF16), kv, xa_wo[layer].astype(BF16),
                    seq, mem_len, tm)

        if layer % 2 == 0:
            xf = _ffn(xf, ffn_norm_g[layer][None, :], dense_w_gate_up[layer // 2].astype(BF16),
                      dense_w_down[layer // 2].astype(BF16), tm, dense_w_down.shape[1] // 2)
        else:
            router = jnp.pad(moe_router[layer // 2], ((0, 0), (0, 128 - N_EXPERTS))).astype(BF16)
            xf = _moe_final(xf, ffn_norm_g[layer][None, :], router, moe_w_gate_up[layer // 2].astype(BF16),
                            moe_w_down[layer // 2].astype(BF16), final_norm_g[None, :], tm)
    return xf.reshape(batch, seq, d)
```

```python
import functools

import numpy as np
import jax
import jax.numpy as jnp
from jax import lax
from jax.experimental import pallas as pl
from jax.experimental.pallas import tpu as pltpu

F32 = jnp.float32
BF16 = jnp.bfloat16

RMS_EPS = 1e-6
D_MODEL = 1024
POOL_W = 256
POOL_GROUP_W = 64
POOL_WINDOWS = (2, 4, 8, 16)
SEQ_HALO = 16
MOBA_HEADS = 4
MOBA_HD = 64
MOBA_W = 256
MOBA_AUG_W = 2 * MOBA_W
MOBA_BLOCK = 256
MOBA_TOPK = 3
MOBA_GROUP = 4
ROPE_THETA = 500000.0
ROPE_DIMS = 16
DN_HEADS = 8
DN_HD = 64
DN_W = 512
DN_CONV = 4
DN_CHUNK = 128
DN_STEP_CHUNKS = 2
XA_HEADS = 4
XA_HD = 128
XA_W = 512
N_EXPERTS = 8
MOE_TOPK = 2
ROW_TILE = 8
COL_POOL, COL_MQ, COL_MK, COL_MV = 0, 1, 2, 3
COL_DQ, COL_DZ = 2, 5
COL_GATE0 = 3
PROJ_COLS = 6144
PROJ_TN = 1024
PROJ_ID_BLOCKS = 3
BA_LANES = 128
VMEM_LIMIT = 56 * 1024 * 1024
NEG_BIG = -1e30


def _cparams(*sem):
    return pltpu.CompilerParams(dimension_semantics=sem, vmem_limit_bytes=VMEM_LIMIT)


def _rms(x, g):
    ms = jnp.mean(x * x, axis=-1, keepdims=True)
    return x * lax.rsqrt(ms + RMS_EPS) * g


def _silu(x):
    return x * jax.nn.sigmoid(x)


def _dot(a, b):
    return jnp.dot(a, b, preferred_element_type=F32)


def _dot_nt(a, b):
    return lax.dot_general(a, b, (((1,), (1,)), ((), ())), preferred_element_type=F32)


def _dot_tn(a, b):
    return lax.dot_general(a, b, (((0,), (0,)), ((), ())), preferred_element_type=F32)


def _inproj_kernel(x_ref, g_ref, w_ref, wba_ref, proj_ref, ba_ref, h_ref):
    j = pl.program_id(1)

    @pl.when(j == 0)
    def _():
        h = _rms(x_ref[...], g_ref[...]).astype(BF16)
        h_ref[...] = h
        ba_ref[...] = _dot(h, wba_ref[...])

    acc = _dot(h_ref[...], w_ref[...])

    @pl.when(j < PROJ_ID_BLOCKS)
    def _():
        proj_ref[...] = acc.astype(BF16)

    @pl.when(j >= PROJ_ID_BLOCKS)
    def _():
        proj_ref[...] = jax.nn.sigmoid(acc).astype(BF16)


def _inproj(x, g, w_main, w_ba, tm):
    n = x.shape[0]
    return pl.pallas_call(
        _inproj_kernel,
        grid=(n // tm, PROJ_COLS // PROJ_TN),
        in_specs=[
            pl.BlockSpec((tm, D_MODEL), lambda i, j: (i, 0)),
            pl.BlockSpec((1, D_MODEL), lambda i, j: (0, 0)),
            pl.BlockSpec((D_MODEL, PROJ_TN), lambda i, j: (0, j)),
            pl.BlockSpec((D_MODEL, BA_LANES), lambda i, j: (0, 0)),
        ],
        out_specs=[
            pl.BlockSpec((tm, PROJ_TN), lambda i, j: (i, j)),
            pl.BlockSpec((tm, BA_LANES), lambda i, j: (i, 0)),
        ],
        out_shape=[jax.ShapeDtypeStruct((n, PROJ_COLS), BF16),
                   jax.ShapeDtypeStruct((n, BA_LANES), F32)],
        scratch_shapes=[pltpu.VMEM((tm, D_MODEL), BF16)],
        compiler_params=_cparams("parallel", "arbitrary"),
        name="inproj",
    )(x, g, w_main, w_ba)


def _rope_table_kernel(pos_ref, freq_ref, sign_ref, c_ref, s_ref):
    ang = pos_ref[...] * freq_ref[...]
    rot = sign_ref[...]
    c_ref[...] = jnp.where(rot != 0.0, jnp.cos(ang), 1.0)
    s_ref[...] = jnp.sin(ang) * rot


def _rope_tables(pos_f32, tm):
    n = pos_f32.shape[0]
    half = ROPE_DIMS // 2
    inv_freq = np.power(ROPE_THETA, -np.arange(half, dtype=np.float32) * 2.0 / ROPE_DIMS).astype(np.float32)
    d = np.arange(128) % MOBA_HD
    freq = np.where(d < ROPE_DIMS, inv_freq[d % half], 0.0).astype(np.float32)[None, :]
    sign = np.where(d < half, -1.0, np.where(d < ROPE_DIMS, 1.0, 0.0)).astype(np.float32)[None, :]
    return pl.pallas_call(
        _rope_table_kernel,
        grid=(n // tm,),
        in_specs=[pl.BlockSpec((tm, 1), lambda i: (i, 0)),
                  pl.BlockSpec((1, 128), lambda i: (0, 0)),
                  pl.BlockSpec((1, 128), lambda i: (0, 0))],
        out_specs=[pl.BlockSpec((tm, 128), lambda i: (i, 0)),
                   pl.BlockSpec((tm, 128), lambda i: (i, 0))],
        out_shape=[jax.ShapeDtypeStruct((n, 128), F32), jax.ShapeDtypeStruct((n, 128), F32)],
        compiler_params=_cparams("parallel"),
        name="rope_tables",
    )(pos_f32, jnp.asarray(freq), jnp.asarray(sign))


def _moba_prep_kernel(q_ref, k_ref, v_ref, c_ref, s_ref, qo_ref, ko_ref, vo_ref, kbar_ref, *, nb):
    c = jnp.concatenate([c_ref[...], c_ref[...]], axis=1)
    s = jnp.concatenate([s_ref[...], s_ref[...]], axis=1)
    lane = lax.broadcasted_iota(jnp.int32, (MOBA_BLOCK, MOBA_W), 1)
    first_half = (lane % MOBA_HD) < (ROPE_DIMS // 2)

    def rope(x):
        partner = jnp.where(first_half,
                            pltpu.roll(x, MOBA_W - ROPE_DIMS // 2, 1),
                            pltpu.roll(x, ROPE_DIMS // 2, 1))
        return x * c + partner * s

    qo_ref[...] = (rope(q_ref[...].astype(F32)) * (MOBA_HD ** -0.5)).astype(BF16)
    k = rope(k_ref[...].astype(F32))
    kbar_ref[0] = jnp.mean(k, axis=0, keepdims=True)
    k16 = k.astype(BF16)
    v16 = v_ref[...]
    blk_lane = lax.broadcasted_iota(jnp.int32, (MOBA_BLOCK, MOBA_HD), 1)
    onehot = jnp.where(blk_lane == pl.program_id(0) % nb, 1.0, 0.0).astype(BF16)
    zeros = jnp.zeros((MOBA_BLOCK, MOBA_HD), BF16)
    k_parts, v_parts = [], []
    for h in range(MOBA_HEADS):
        sl = slice(h * MOBA_HD, (h + 1) * MOBA_HD)
        k_parts += [k16[:, sl], onehot]
        v_parts += [v16[:, sl], zeros]
    ko_ref[...] = jnp.concatenate(k_parts, axis=1)
    vo_ref[...] = jnp.concatenate(v_parts, axis=1)


def _moba_prep(proj, cos_t, sin_t, nb):
    n = proj.shape[0]
    nblk = n // MOBA_BLOCK
    return pl.pallas_call(
        functools.partial(_moba_prep_kernel, nb=nb),
        grid=(nblk,),
        in_specs=[pl.BlockSpec((MOBA_BLOCK, MOBA_W), lambda r: (r, COL_MQ)),
                  pl.BlockSpec((MOBA_BLOCK, MOBA_W), lambda r: (r, COL_MK)),
                  pl.BlockSpec((MOBA_BLOCK, MOBA_W), lambda r: (r, COL_MV)),
                  pl.BlockSpec((MOBA_BLOCK, 128), lambda r: (r, 0)),
                  pl.BlockSpec((MOBA_BLOCK, 128), lambda r: (r, 0))],
        out_specs=[pl.BlockSpec((MOBA_BLOCK, MOBA_W), lambda r: (r, 0)),
                   pl.BlockSpec((MOBA_BLOCK, MOBA_AUG_W), lambda r: (r, 0)),
                   pl.BlockSpec((MOBA_BLOCK, MOBA_AUG_W), lambda r: (r, 0)),
                   pl.BlockSpec((1, 1, MOBA_W), lambda r: (r, 0, 0))],
        out_shape=[jax.ShapeDtypeStruct((n, MOBA_W), BF16),
                   jax.ShapeDtypeStruct((n, MOBA_AUG_W), BF16),
                   jax.ShapeDtypeStruct((n, MOBA_AUG_W), BF16),
                   jax.ShapeDtypeStruct((nblk, 1, MOBA_W), F32)],
        compiler_params=_cparams("parallel"),
        name="moba_prep",
    )(proj, proj, proj, cos_t, sin_t)


def _moba_kernel(q_ref, k_ref, v_ref, kbar_ref, o_ref, qa_ref, m_ref, ls_ref, acc_ref, s_ref):
    i = pl.program_id(1)
    heads = range(MOBA_HEADS)
    blk = lax.broadcasted_iota(jnp.int32, (MOBA_BLOCK, MOBA_HD), 1).astype(F32)
    row = lax.broadcasted_iota(jnp.int32, (MOBA_BLOCK, MOBA_BLOCK), 0)
    col = lax.broadcasted_iota(jnp.int32, (MOBA_BLOCK, MOBA_BLOCK), 1)
    causal = col <= row
    i_f = i.astype(F32)
    own = pl.multiple_of(i * MOBA_BLOCK, MOBA_BLOCK)
    hs = [slice(h * 2 * MOBA_HD, (h + 1) * 2 * MOBA_HD) for h in heads]

    for h in heads:
        sl = slice(h * MOBA_HD, (h + 1) * MOBA_HD)
        qh = q_ref[:, sl]
        gate = _dot_nt(qh, kbar_ref[0, :, sl].astype(BF16))
        gate = jnp.where(blk < i_f, gate, -jnp.inf)
        keep = jnp.zeros((MOBA_BLOCK, MOBA_HD), jnp.bool_)
        for _ in range(MOBA_TOPK):
            m = jnp.max(gate, axis=1, keepdims=True)
            idx = jnp.min(jnp.where(gate == m, blk, float(MOBA_HD)), axis=1, keepdims=True)
            pick = (blk == idx) & (m > -jnp.inf)
            keep = keep | pick
            gate = jnp.where(pick, -jnp.inf, gate)
        qa_ref[h] = jnp.concatenate([qh, jnp.where(keep, 0.0, NEG_BIG).astype(BF16)], axis=1)
        qa_ref[MOBA_HEADS + h] = jnp.concatenate([qh, jnp.zeros_like(qh)], axis=1)

    def scores(h, start):
        return _dot_nt(qa_ref[h], k_ref[pl.ds(start, MOBA_BLOCK), hs[h]])

    def own_scores(h):
        s = _dot_nt(qa_ref[MOBA_HEADS + h], k_ref[pl.ds(own, MOBA_BLOCK), hs[h]])
        return jnp.where(causal, s, NEG_BIG)

    def fold(x):
        return x[:, :128], x[:, 128:]

    n_groups = (i + (MOBA_GROUP - 1)) // MOBA_GROUP

    def group_blocks(g):
        return [pl.multiple_of((g * MOBA_GROUP + t) * MOBA_BLOCK, MOBA_BLOCK) for t in range(MOBA_GROUP)]

    for h in heads:
        s = own_scores(h)
        m = jnp.max(s, axis=1, keepdims=True)
        a, b = fold(jnp.exp(s - m))
        ls_ref[h] = a + b
        acc_ref[h] = _dot(jnp.concatenate([a, b], axis=1).astype(BF16), v_ref[pl.ds(own, MOBA_BLOCK), hs[h]])
        m_ref[h] = jnp.broadcast_to(m, (MOBA_BLOCK, 128))

    @pl.loop(0, n_groups)
    def _(g):
        starts = group_blocks(g)
        gmax = [None] * MOBA_HEADS
        for t, start in enumerate(starts):
            for h in heads:
                s = scores(h, start)
                s_ref[t, h] = s
                a, b = fold(s)
                ab = jnp.maximum(a, b)
                gmax[h] = ab if gmax[h] is None else jnp.maximum(gmax[h], ab)
        m_new = []
        for h in heads:
            m_old = m_ref[h]
            m_new.append(jnp.maximum(m_old, jnp.max(gmax[h], axis=1, keepdims=True)))
            alpha = jnp.exp(m_old - m_new[h])
            ls_ref[h] = ls_ref[h] * alpha
            acc_ref[h] = acc_ref[h] * alpha
            m_ref[h] = m_new[h]
        for t, start in enumerate(starts):
            for h in heads:
                a = jnp.exp(s_ref[t, h, :, :128] - m_new[h])
                b = jnp.exp(s_ref[t, h, :, 128:] - m_new[h])
                ls_ref[h] += a + b
                acc_ref[h] += _dot(jnp.concatenate([a, b], axis=1).astype(BF16),
                                   v_ref[pl.ds(start, MOBA_BLOCK), hs[h]])

    for h in heads:
        o_ref[:, hs[h]] = (acc_ref[h] / jnp.sum(ls_ref[h], axis=1, keepdims=True)).astype(BF16)


def _moba(q, k_aug, v_aug, kbar, batch, seq):
    n = q.shape[0]
    nb = seq // MOBA_BLOCK
    return pl.pallas_call(
        _moba_kernel,
        grid=(batch, nb),
        in_specs=[pl.BlockSpec((MOBA_BLOCK, MOBA_W), lambda b, i: (b * nb + i, 0)),
                  pl.BlockSpec((seq, MOBA_AUG_W), lambda b, i: (b, 0)),
                  pl.BlockSpec((seq, MOBA_AUG_W), lambda b, i: (b, 0)),
                  pl.BlockSpec((1, MOBA_HD, MOBA_W), lambda b, i: (b, 0, 0))],
        out_specs=pl.BlockSpec((MOBA_BLOCK, MOBA_AUG_W), lambda b, i: (b * nb + i, 0)),
        out_shape=jax.ShapeDtypeStruct((n, MOBA_AUG_W), BF16),
        scratch_shapes=[pltpu.VMEM((2 * MOBA_HEADS, MOBA_BLOCK, 2 * MOBA_HD), BF16)]
        + [pltpu.VMEM((MOBA_HEADS, MOBA_BLOCK, 2 * MOBA_HD), F32)] * 3
        + [pltpu.VMEM((MOBA_GROUP, MOBA_HEADS, MOBA_BLOCK, MOBA_BLOCK), F32)],
        compiler_params=_cparams("parallel", "arbitrary"),
        name="moba",
    )(q, k_aug, v_aug, kbar)


def _pool_kernel(p_ref, halo_ref, w_ref, scale_ref, o_ref, *, ts):
    i = pl.program_id(1)
    halo = jnp.where(i == 0, 0.0, halo_ref[...].astype(F32))
    p = p_ref[...].astype(F32)
    cur = jnp.concatenate([halo, p], axis=0)
    lane = lax.broadcasted_iota(jnp.int32, (ts, POOL_W), 1)
    t1 = (lax.broadcasted_iota(jnp.int32, (ts, POOL_W), 0) + i * ts + 1).astype(F32)
    total = jnp.zeros((ts, POOL_W), F32)
    count = jnp.ones((ts, POOL_W), F32)
    span = 1
    for gi, w in enumerate(POOL_WINDOWS):
        while span < w:
            cur = cur + pltpu.roll(cur, span, 0)
            span *= 2
        in_group = (lane >= gi * POOL_GROUP_W) & (lane < (gi + 1) * POOL_GROUP_W)
        total = jnp.where(in_group, cur[SEQ_HALO:], total)
        count = jnp.where(in_group, jnp.minimum(t1, float(w)), count)
    pooled = total / count - p
    o_ref[...] = (_dot(pooled.astype(BF16), w_ref[...]) * scale_ref[...]).astype(BF16)


def _pool(proj, w_blockdiag, scale, batch, seq, ts):
    n = proj.shape[0]
    nt = seq // ts
    hb = ts // SEQ_HALO
    return pl.pallas_call(
        functools.partial(_pool_kernel, ts=ts),
        grid=(batch, nt),
        in_specs=[pl.BlockSpec((ts, POOL_W), lambda b, i: (b * nt + i, COL_POOL)),
                  pl.BlockSpec((SEQ_HALO, POOL_W), lambda b, i: (jnp.maximum((b * nt + i) * hb - 1, 0), COL_POOL)),
                  pl.BlockSpec((POOL_W, POOL_W), lambda b, i: (0, 0)),
                  pl.BlockSpec((1, POOL_W), lambda b, i: (0, 0))],
        out_specs=pl.BlockSpec((ts, POOL_W), lambda b, i: (b * nt + i, 0)),
        out_shape=jax.ShapeDtypeStruct((n, POOL_W), BF16),
        compiler_params=_cparams("parallel", "parallel"),
        name="pool",
    )(proj, proj, w_blockdiag, scale)


def _deltanet_kernel(q_ref, k_ref, v_ref, qh_ref, kh_ref, vh_ref, z_ref, ba_ref, cw_ref, alog_ref, dtb_ref,
                     ng_ref, expb_ref, expg_ref, o_ref, state_ref, oraw_ref):
    i = pl.program_id(1)
    C = DN_CHUNK
    TS = DN_STEP_CHUNKS * C

    @pl.when(i == 0)
    def _():
        state_ref[...] = jnp.zeros_like(state_ref)

    def conv(x_ref, halo_ref, w):
        halo = jnp.where(i == 0, 0.0, halo_ref[...].astype(F32))
        ext = jnp.concatenate([halo, x_ref[...].astype(F32)], axis=0)
        y = ext * w[DN_CONV - 1:DN_CONV]
        for lag in range(1, DN_CONV):
            y = y + pltpu.roll(ext, lag, 0) * w[DN_CONV - 1 - lag:DN_CONV - lag]
        return _silu(y[SEQ_HALO:])

    cw = cw_ref[...]
    qc = conv(q_ref, qh_ref, cw[0])
    kc = conv(k_ref, kh_ref, cw[1])
    vc = conv(v_ref, vh_ref, cw[2])

    ba = ba_ref[...]
    beta = jax.nn.sigmoid(ba)
    g = -jnp.exp(alog_ref[...]) * jax.nn.softplus(ba + dtb_ref[...])
    rows = lax.broadcasted_iota(jnp.int32, (TS, BA_LANES), 0) & (C - 1)
    G = g
    span = 1
    while span < C:
        G = G + jnp.where(rows >= span, pltpu.roll(G, span, 0), 0.0)
        span *= 2
    GT = G.T
    r_i = lax.broadcasted_iota(jnp.int32, (C, C), 0)
    c_i = lax.broadcasted_iota(jnp.int32, (C, C), 1)
    tril = c_i <= r_i
    eye = (c_i == r_i).astype(F32)
    levels = C.bit_length() - 1
    level_masks = [((r_i >> k) & 1 == 1) & ((c_i >> k) == (r_i >> k) - 1) for k in range(levels)]
    expb = expb_ref[...]
    expg = expg_ref[...]

    def per_head(x, e):
        hi = x.astype(BF16)
        lo = (x - hi.astype(F32)).astype(BF16)
        return _dot(hi, e) + _dot(lo, e)

    def inv_norm(x):
        ss = _dot_nt((x * x).astype(BF16), expb)
        return per_head(lax.rsqrt(ss + RMS_EPS), expb)

    qn_all = qc * (inv_norm(qc) * (DN_HD ** -0.5))
    kn_all = kc * inv_norm(kc)
    beta_all = per_head(beta, expb)
    eG_all = per_head(jnp.exp(G), expg)
    G_last_rows = jnp.concatenate(
        [jnp.broadcast_to(G[(c + 1) * C - 1:(c + 1) * C, :], (C, BA_LANES)) for c in range(DN_STEP_CHUNKS)], axis=0)
    kb_all = kn_all * beta_all
    vb_all = vc * beta_all
    kbe_all = kb_all * eG_all
    qn16 = qn_all.astype(BF16)
    kn16 = kn_all.astype(BF16)
    kb16 = kb_all.astype(BF16)
    qe16 = (qn_all * eG_all).astype(BF16)
    kdec16 = (kn_all * per_head(jnp.exp(G_last_rows - G), expg)).astype(BF16)

    heads = range(DN_HEADS)
    units = [(c, h) for c in range(DN_STEP_CHUNKS) for h in heads]
    rs = [slice(c * C, (c + 1) * C) for c in range(DN_STEP_CHUNKS)]
    sls = [slice(h * DN_HD, (h + 1) * DN_HD) for h in heads]
    G_c, A, aqk, Z = {}, {}, {}, {}
    for u in units:
        c, h = u
        G_c[u] = G[rs[c], DN_HEADS + h:DN_HEADS + h + 1]
        gram = _dot_nt(jnp.concatenate([kb16[rs[c], sls[h]], qn16[rs[c], sls[h]]], axis=0), kn16[rs[c], sls[h]])
        G_r = GT[DN_HEADS + h:DN_HEADS + h + 1, rs[c]]
        decay = jnp.exp(jnp.where(tril, G_c[u] - G_r, -jnp.inf))
        A[u] = (gram[:C] * decay).astype(BF16)
        aqk[u] = (gram[C:] * decay).astype(BF16)
    for u in units:
        c, h = u
        X = jnp.concatenate([vb_all[rs[c], sls[h]], kbe_all[rs[c], sls[h]]], axis=1)
        L1 = jnp.where(level_masks[0], A[u], 0.0)
        Z[u] = jnp.concatenate([eye - L1.astype(F32), X - _dot(L1, X.astype(BF16))], axis=1)
    for lvl in range(1, levels):
        Z16 = {u: Z[u].astype(BF16) for u in units}
        cols = slice(0, C + 2 * DN_HD) if lvl < levels - 1 else slice(C, C + 2 * DN_HD)
        Y = {u: _dot(jnp.where(level_masks[lvl], A[u], 0.0), Z16[u][:, cols]).astype(BF16) for u in units}
        Z = {u: Z[u][:, cols] - _dot(Z16[u][:, :C], Y[u]) for u in units}
    S = [state_ref[h] for h in heads]
    for c in range(DN_STEP_CHUNKS):
        us = [(c, h) for h in heads]
        ws = [_dot(jnp.concatenate([Z[u][:, DN_HD:].astype(BF16), qe16[rs[c], sls[u[1]]]], axis=0),
                   S[u[1]].astype(BF16)) for u in us]
        v16 = [(Z[u][:, :DN_HD] - ws[u[1]][:C]).astype(BF16) for u in us]
        for u in us:
            h = u[1]
            G_last = G_c[u][C - 1:C, :]
            S[h] = S[h] * jnp.exp(G_last) + _dot_tn(kdec16[rs[c], sls[h]], v16[h])
        for u in us:
            h = u[1]
            oraw_ref[rs[c], sls[h]] = ws[h][C:] + _dot(aqk[u], v16[h])
    for h in heads:
        state_ref[h] = S[h]
    o = oraw_ref[...]
    ms = _dot_nt((o * o).astype(BF16), expb) * (1.0 / DN_HD)
    o = o * per_head(lax.rsqrt(ms + RMS_EPS), expb) * ng_ref[...]
    o_ref[...] = (o * _silu(z_ref[...].astype(F32))).astype(BF16)


def _deltanet(proj, ba, conv_w, alog_row, dtb_row, norm_g, batch, seq):
    n = proj.shape[0]
    C = DN_STEP_CHUNKS * DN_CHUNK
    nt = seq // C
    hb = C // SEQ_HALO

    def head_lanes(first_row):
        m = np.zeros((BA_LANES, DN_W), np.float32)
        for h in range(DN_HEADS):
            m[first_row + h, h * DN_HD:(h + 1) * DN_HD] = 1.0
        return m

    def cur(col):
        return pl.BlockSpec((C, DN_W), lambda b, i: (b * nt + i, col))

    def halo(col):
        return pl.BlockSpec((SEQ_HALO, DN_W), lambda b, i: (jnp.maximum((b * nt + i) * hb - 1, 0), col))

    return pl.pallas_call(
        _deltanet_kernel,
        grid=(batch, nt),
        in_specs=[cur(COL_DQ), cur(COL_DQ + 1), cur(COL_DQ + 2),
                  halo(COL_DQ), halo(COL_DQ + 1), halo(COL_DQ + 2),
                  cur(COL_DZ),
                  pl.BlockSpec((C, BA_LANES), lambda b, i: (b * nt + i, 0)),
                  pl.BlockSpec((3, DN_CONV, DN_W), lambda b, i: (0, 0, 0)),
                  pl.BlockSpec((1, BA_LANES), lambda b, i: (0, 0)),
                  pl.BlockSpec((1, BA_LANES), lambda b, i: (0, 0)),
                  pl.BlockSpec((1, DN_W), lambda b, i: (0, 0)),
                  pl.BlockSpec((BA_LANES, DN_W), lambda b, i: (0, 0)),
                  pl.BlockSpec((BA_LANES, DN_W), lambda b, i: (0, 0))],
        out_specs=pl.BlockSpec((C, DN_W), lambda b, i: (b * nt + i, 0)),
        out_shape=jax.ShapeDtypeStruct((n, DN_W), BF16),
        scratch_shapes=[pltpu.VMEM((DN_HEADS, DN_HD, DN_HD), F32), pltpu.VMEM((C, DN_W), F32)],
        compiler_params=_cparams("parallel", "arbitrary"),
        name="deltanet",
    )(proj, proj, proj, proj, proj, proj, proj, ba, conv_w, alog_row, dtb_row, norm_g,
      jnp.asarray(head_lanes(0), BF16), jnp.asarray(head_lanes(DN_HEADS), BF16))


def _merge_kernel(x_ref, yp_ref, ym_ref, yd_ref, g0_ref, g1_ref, g2_ref, wp_ref, wm_ref, wd_ref, wo_ref, o_ref):
    merged = (g0_ref[...].astype(F32) * _dot(yp_ref[...], wp_ref[...])
              + g1_ref[...].astype(F32) * _dot(ym_ref[...], wm_ref[...])
              + g2_ref[...].astype(F32) * _dot(yd_ref[...], wd_ref[...]))
    o_ref[...] = x_ref[...] + _dot(merged.astype(BF16), wo_ref[...])


def _merge(x, y_pool, y_moba, y_dn, proj, w_up_pool, w_up_moba, w_up_dn, w_out, tm):
    n = x.shape[0]

    def rows(width, col=0):
        return pl.BlockSpec((tm, width), lambda i: (i, col))

    def whole(shape):
        return pl.BlockSpec(shape, lambda i: (0, 0))

    return pl.pallas_call(
        _merge_kernel,
        grid=(n // tm,),
        in_specs=[rows(D_MODEL), rows(POOL_W), rows(MOBA_AUG_W), rows(DN_W),
                  rows(D_MODEL, COL_GATE0), rows(D_MODEL, COL_GATE0 + 1), rows(D_MODEL, COL_GATE0 + 2),
                  whole((POOL_W, D_MODEL)), whole((MOBA_AUG_W, D_MODEL)), whole((DN_W, D_MODEL)),
                  whole((D_MODEL, D_MODEL))],
        out_specs=rows(D_MODEL),
        out_shape=jax.ShapeDtypeStruct((n, D_MODEL), F32),
        compiler_params=_cparams("parallel"),
        name="merge",
    )(x, y_pool, y_moba, y_dn, proj, proj, proj, w_up_pool, w_up_moba, w_up_dn, w_out)


def _memkv_kernel(mem_ref, g_ref, w_ref, o_ref):
    o_ref[...] = _dot(_rms(mem_ref[...], g_ref[...]).astype(BF16), w_ref[...]).astype(BF16)


def _memkv(mem2d, g, wkv):
    m = mem2d.shape[0]
    tm = 256
    return pl.pallas_call(
        _memkv_kernel,
        grid=(m // tm,),
        in_specs=[pl.BlockSpec((tm, D_MODEL), lambda i: (i, 0)),
                  pl.BlockSpec((1, D_MODEL), lambda i: (0, 0)),
                  pl.BlockSpec((D_MODEL, 2 * XA_W), lambda i: (0, 0))],
        out_specs=pl.BlockSpec((tm, 2 * XA_W), lambda i: (i, 0)),
        out_shape=jax.ShapeDtypeStruct((m, 2 * XA_W), BF16),
        compiler_params=_cparams("parallel"),
        name="memkv",
    )(mem2d, g, wkv)


def _xattn_kernel(x_ref, g_ref, wq_ref, kv_ref, wo_ref, o_ref):
    x = x_ref[...]
    q = _dot(_rms(x, g_ref[...]).astype(BF16), wq_ref[...]).astype(BF16)
    scale = XA_HD ** -0.5
    outs = []
    for h in range(XA_HEADS):
        sl = slice(h * XA_HD, (h + 1) * XA_HD)
        k = kv_ref[:, h * XA_HD:(h + 1) * XA_HD]
        v = kv_ref[:, XA_W + h * XA_HD:XA_W + (h + 1) * XA_HD]
        s = _dot_nt(q[:, sl], k) * scale
        s = s - jnp.max(s, axis=1, keepdims=True)
        p = jnp.exp(s)
        p = p / jnp.sum(p, axis=1, keepdims=True)
        outs.append(_dot(p.astype(BF16), v).astype(BF16))
    o = jnp.concatenate(outs, axis=1)
    o_ref[...] = x + _dot(o, wo_ref[...])


def _xattn(x, g, wq, kv, wo, seq, mem_len, tm):
    n = x.shape[0]
    tiles_per_seq = seq // tm
    return pl.pallas_call(
        _xattn_kernel,
        grid=(n // tm,),
        in_specs=[pl.BlockSpec((tm, D_MODEL), lambda i: (i, 0)),
                  pl.BlockSpec((1, D_MODEL), lambda i: (0, 0)),
                  pl.BlockSpec((D_MODEL, XA_W), lambda i: (0, 0)),
                  pl.BlockSpec((mem_len, 2 * XA_W), lambda i: (i // tiles_per_seq, 0)),
                  pl.BlockSpec((XA_W, D_MODEL), lambda i: (0, 0))],
        out_specs=pl.BlockSpec((tm, D_MODEL), lambda i: (i, 0)),
        out_shape=jax.ShapeDtypeStruct((n, D_MODEL), F32),
        compiler_params=_cparams("parallel"),
        name="xattn",
    )(x, g, wq, kv, wo)


def _swiglu_chunk(h, wg_ref, wu_ref, wd_ref):
    a = _silu(_dot(h, wg_ref[...])) * _dot(h, wu_ref[...])
    return _dot(a.astype(BF16), wd_ref[...])


def _ffn_kernel(x_ref, g_ref, wg_ref, wu_ref, wd_ref, o_ref, h_ref, acc_ref):
    j = pl.program_id(1)

    @pl.when(j == 0)
    def _():
        h_ref[...] = _rms(x_ref[...], g_ref[...]).astype(BF16)
        acc_ref[...] = x_ref[...]

    acc_ref[...] += _swiglu_chunk(h_ref[...], wg_ref, wu_ref, wd_ref)

    @pl.when(j == pl.num_programs(1) - 1)
    def _():
        o_ref[...] = acc_ref[...]


def _ffn(x, g, w_gate_up, w_down, tm, tf):
    n = x.shape[0]
    ff = w_down.shape[0]
    nf = ff // tf
    return pl.pallas_call(
        _ffn_kernel,
        grid=(n // tm, nf),
        in_specs=[pl.BlockSpec((tm, D_MODEL), lambda i, j: (i, 0)),
                  pl.BlockSpec((1, D_MODEL), lambda i, j: (0, 0)),
                  pl.BlockSpec((D_MODEL, tf), lambda i, j: (0, j)),
                  pl.BlockSpec((D_MODEL, tf), lambda i, j: (0, nf + j)),
                  pl.BlockSpec((tf, D_MODEL), lambda i, j: (j, 0))],
        out_specs=pl.BlockSpec((tm, D_MODEL), lambda i, j: (i, 0)),
        out_shape=jax.ShapeDtypeStruct((n, D_MODEL), F32),
        scratch_shapes=[pltpu.VMEM((tm, D_MODEL), BF16), pltpu.VMEM((tm, D_MODEL), F32)],
        compiler_params=_cparams("parallel", "arbitrary"),
        name="ffn",
    )(x, g, w_gate_up, w_gate_up, w_down)


def _moe_kernel(x_ref, g_ref, r_ref, wg_ref, wu_ref, wd_ref, fg_ref, o_ref, h_ref, comb_ref, acc_ref, *, tm):
    e = pl.program_id(1)
    lane = lax.broadcasted_iota(jnp.int32, (tm, 128), 1).astype(F32)

    @pl.when(e == 0)
    def _():
        h = _rms(x_ref[...], g_ref[...]).astype(BF16)
        h_ref[...] = h
        acc_ref[...] = x_ref[...]
        logits = jnp.where(lane < float(N_EXPERTS), _dot(h, r_ref[...]), -jnp.inf)
        m1 = jnp.max(logits, axis=1, keepdims=True)
        i1 = jnp.min(jnp.where(logits == m1, lane, 128.0), axis=1, keepdims=True)
        rest = jnp.where(lane == i1, -jnp.inf, logits)
        m2 = jnp.max(rest, axis=1, keepdims=True)
        i2 = jnp.min(jnp.where(rest == m2, lane, 128.0), axis=1, keepdims=True)
        e2 = jnp.exp(m2 - m1)
        w1 = 1.0 / (1.0 + e2)
        w2 = e2 / (1.0 + e2)
        comb_ref[...] = jnp.where(lane == i1, w1, 0.0) + jnp.where(lane == i2, w2, 0.0)

    w_e = jnp.sum(jnp.where(lane == e.astype(F32), comb_ref[...], 0.0), axis=1, keepdims=True)
    acc_ref[...] += w_e * _swiglu_chunk(h_ref[...], wg_ref.at[0], wu_ref.at[0], wd_ref.at[0])

    @pl.when(e == pl.num_programs(1) - 1)
    def _():
        o_ref[...] = _rms(acc_ref[...], fg_ref[...])


def _moe_final(x, g, router, w_gate_up, w_down, final_g, tm):
    n = x.shape[0]
    ff = w_down.shape[1]
    return pl.pallas_call(
        functools.partial(_moe_kernel, tm=tm),
        grid=(n // tm, N_EXPERTS),
        in_specs=[pl.BlockSpec((tm, D_MODEL), lambda i, e: (i, 0)),
                  pl.BlockSpec((1, D_MODEL), lambda i, e: (0, 0)),
                  pl.BlockSpec((D_MODEL, 128), lambda i, e: (0, 0)),
                  pl.BlockSpec((1, D_MODEL, ff), lambda i, e: (e, 0, 0)),
                  pl.BlockSpec((1, D_MODEL, ff), lambda i, e: (e, 0, 1)),
                  pl.BlockSpec((1, ff, D_MODEL), lambda i, e: (e, 0, 0)),
                  pl.BlockSpec((1, D_MODEL), lambda i, e: (0, 0))],
        out_specs=pl.BlockSpec((tm, D_MODEL), lambda i, e: (i, 0)),
        out_shape=jax.ShapeDtypeStruct((n, D_MODEL), F32),
        scratch_shapes=[pltpu.VMEM((tm, D_MODEL), BF16), pltpu.VMEM((tm, 128), F32),
                        pltpu.VMEM((tm, D_MODEL), F32)],
        compiler_params=_cparams("parallel", "arbitrary"),
        name="moe",
    )(x, g, router, w_gate_up, w_gate_up, w_down, final_g)


def _route_kernel(x_ref, g_ref, r_ref, o_ref, *, tm):
    lane = lax.broadcasted_iota(jnp.int32, (tm, 128), 1).astype(F32)
    h = _rms(x_ref[...], g_ref[...]).astype(BF16)
    logits = jnp.where(lane < float(N_EXPERTS), _dot(h, r_ref[...]), -jnp.inf)
    m1 = jnp.max(logits, axis=1, keepdims=True)
    i1 = jnp.min(jnp.where(logits == m1, lane, 128.0), axis=1, keepdims=True)
    rest = jnp.where(lane == i1, -jnp.inf, logits)
    m2 = jnp.max(rest, axis=1, keepdims=True)
    i2 = jnp.min(jnp.where(rest == m2, lane, 128.0), axis=1, keepdims=True)
    e2 = jnp.exp(m2 - m1)
    o_ref[...] = (jnp.where(lane == 0.0, i1, 0.0) + jnp.where(lane == 1.0, i2, 0.0)
                  + jnp.where(lane == 2.0, 1.0 / (1.0 + e2), 0.0) + jnp.where(lane == 3.0, e2 / (1.0 + e2), 0.0))


def _route(x, g, router, tm):
    n = x.shape[0]
    return pl.pallas_call(
        functools.partial(_route_kernel, tm=tm),
        grid=(n // tm,),
        in_specs=[pl.BlockSpec((tm, D_MODEL), lambda i: (i, 0)),
                  pl.BlockSpec((1, D_MODEL), lambda i: (0, 0)),
                  pl.BlockSpec((D_MODEL, 128), lambda i: (0, 0))],
        out_specs=pl.BlockSpec((tm, 128), lambda i: (i, 0)),
        out_shape=jax.ShapeDtypeStruct((n, 128), F32),
        compiler_params=_cparams("parallel"),
        name="moe_route",
    )(x, g, router)


def _gather_row_tiles(idx_vmem, idx_smem, src_hbm, dst, count, sems):
    stage = pltpu.make_async_copy(idx_vmem, idx_smem, sems.at[0])
    stage.start()
    stage.wait()

    def row_copy(r):
        s0 = pl.multiple_of(idx_smem[r] * ROW_TILE, ROW_TILE)
        d0 = pl.multiple_of(r * ROW_TILE, ROW_TILE)
        return pltpu.make_async_copy(src_hbm.at[pl.ds(s0, ROW_TILE), :], dst.at[pl.ds(d0, ROW_TILE), :], sems.at[1])

    @pl.loop(0, count)
    def _(r):
        row_copy(r).start()

    @pl.loop(0, count)
    def _(r):
        row_copy(r).wait()


def _rms_row_tiles(v, g8, rows):
    v3 = v.reshape(rows, ROW_TILE, 128)
    ms = jnp.sum(jnp.sum(v3 * v3, axis=2, keepdims=True), axis=1, keepdims=True) * (1.0 / D_MODEL)
    return (v3 * lax.rsqrt(ms + RMS_EPS) * g8[None]).reshape(rows * ROW_TILE, 128)


def _expert_kernel(te_ref, nu_ref, x_hbm, src_ref, w_ref, g8_ref, wg_ref, wu_ref, wd_ref, o_ref,
                   idx_ref, xbuf_ref, x16_ref, sems, *, tm):
    t = pl.program_id(0)
    del te_ref

    @pl.when(t >= nu_ref[0])
    def _():
        o_ref[...] = jnp.zeros_like(o_ref)

    @pl.when(t < nu_ref[0])
    def _():
        _gather_row_tiles(src_ref.at[0, 0], idx_ref, x_hbm, xbuf_ref, tm, sems)
        xbuf_ref[...] = _rms_row_tiles(xbuf_ref[...], g8_ref[...], tm)
        for s in range(ROW_TILE):
            x16_ref[:, s * 128:(s + 1) * 128] = xbuf_ref[pl.ds(s, tm, stride=ROW_TILE), :].astype(BF16)
        y = _swiglu_chunk(x16_ref[...], wg_ref.at[0], wu_ref.at[0], wd_ref.at[0]) * w_ref[...]
        for s in range(ROW_TILE):
            o_ref[pl.ds(s, tm, stride=ROW_TILE), :] = y[:, s * 128:(s + 1) * 128]


def _experts(x_rows, src, w_sorted, tile_expert, n_used, g8, w_gate_up, w_down, tm):
    nt = src.shape[0]
    ff = w_down.shape[1]
    grid_spec = pltpu.PrefetchScalarGridSpec(
        num_scalar_prefetch=2,
        grid=(nt,),
        in_specs=[pl.BlockSpec(memory_space=pl.ANY),
                  pl.BlockSpec((1, 1, tm), lambda t, te, nu: (t, 0, 0)),
                  pl.BlockSpec((tm, 1), lambda t, te, nu: (t, 0)),
                  pl.BlockSpec((ROW_TILE, 128), lambda t, te, nu: (0, 0)),
                  pl.BlockSpec((1, D_MODEL, ff), lambda t, te, nu: (te[t], 0, 0)),
                  pl.BlockSpec((1, D_MODEL, ff), lambda t, te, nu: (te[t], 0, 1)),
                  pl.BlockSpec((1, ff, D_MODEL), lambda t, te, nu: (te[t], 0, 0))],
        out_specs=pl.BlockSpec((tm * ROW_TILE, 128), lambda t, te, nu: (t, 0)),
        scratch_shapes=[pltpu.SMEM((tm,), jnp.int32), pltpu.VMEM((tm * ROW_TILE, 128), F32),
                        pltpu.VMEM((tm, D_MODEL), BF16), pltpu.SemaphoreType.DMA((2,))])
    return pl.pallas_call(
        functools.partial(_expert_kernel, tm=tm),
        grid_spec=grid_spec,
        out_shape=jax.ShapeDtypeStruct((nt * tm * ROW_TILE, 128), F32),
        compiler_params=_cparams("arbitrary"),
        name="moe_experts",
    )(tile_expert, n_used, x_rows, src, w_sorted, g8, w_gate_up, w_gate_up, w_down)


def _combine_kernel(x_ref, y_hbm, dest_ref, fg8_ref, o_ref, idx_ref, ybuf_ref, sems, *, tt):
    _gather_row_tiles(dest_ref.at[0, 0], idx_ref, y_hbm, ybuf_ref, MOE_TOPK * tt, sems)
    n8 = tt * ROW_TILE
    v = x_ref[...] + ybuf_ref[pl.ds(0, n8), :] + ybuf_ref[pl.ds(n8, n8), :]
    o_ref[...] = _rms_row_tiles(v, fg8_ref[...], tt)


def _combine_final(x_rows, y_rows, dest, fg8, tt):
    nt = dest.shape[0]
    return pl.pallas_call(
        functools.partial(_combine_kernel, tt=tt),
        grid=(nt,),
        in_specs=[pl.BlockSpec((tt * ROW_TILE, 128), lambda i: (i, 0)),
                  pl.BlockSpec(memory_space=pl.ANY),
                  pl.BlockSpec((1, 1, MOE_TOPK * tt), lambda i: (i, 0, 0)),
                  pl.BlockSpec((ROW_TILE, 128), lambda i: (0, 0))],
        out_specs=pl.BlockSpec((tt * ROW_TILE, 128), lambda i: (i, 0)),
        out_shape=jax.ShapeDtypeStruct(x_rows.shape, F32),
        scratch_shapes=[pltpu.SMEM((MOE_TOPK * tt,), jnp.int32),
                        pltpu.VMEM((MOE_TOPK * tt * ROW_TILE, 128), F32), pltpu.SemaphoreType.DMA((2,))],
        compiler_params=_cparams("arbitrary"),
        name="moe_combine",
    )(x_rows, y_rows, dest, fg8)


def _moe_sparse_final(xf, g, router, w_gate_up, w_down, final_g, tm, tt):
    n = xf.shape[0]
    nk = MOE_TOPK * n
    route = _route(xf, g, router, tm)
    e_flat = route[:, :MOE_TOPK].astype(jnp.int32).reshape(nk)
    w_flat = route[:, MOE_TOPK:2 * MOE_TOPK].reshape(nk)
    onehot = (e_flat[:, None] == jnp.arange(N_EXPERTS, dtype=jnp.int32)[None, :]).astype(jnp.int32)
    csum = jnp.cumsum(onehot, axis=0)
    cnt = csum[-1]
    rank = jnp.take_along_axis(csum, e_flat[:, None], axis=1)[:, 0] - 1
    padded = ((cnt + tm - 1) // tm) * tm
    ends = jnp.cumsum(padded)
    off = ends - padded
    start = jnp.cumsum(cnt) - cnt
    dest = off[e_flat] + rank
    nt = nk // tm + N_EXPERTS
    slot = jnp.arange(nt * tm, dtype=jnp.int32)
    e_slot = jnp.minimum(jnp.searchsorted(ends, slot, side="right"), N_EXPERTS - 1).astype(jnp.int32)
    loc = slot - off[e_slot]
    valid = loc < cnt[e_slot]
    order = jnp.argsort(e_flat, stable=True).astype(jnp.int32)
    pair = order[jnp.clip(start[e_slot] + loc, 0, nk - 1)]
    src = jnp.where(valid, pair // MOE_TOPK, 0).reshape(nt, 1, tm)
    w_sorted = jnp.where(valid, w_flat[pair], 0.0).reshape(nt * tm, 1)
    tile_expert = e_slot[::tm]
    n_used = (ends[-1:] // tm).astype(jnp.int32)

    x_rows = xf.reshape(n * ROW_TILE, 128)
    y_rows = _experts(x_rows, src, w_sorted, tile_expert, n_used, g.reshape(ROW_TILE, 128), w_gate_up, w_down, tm)
    dest = dest.reshape(n // tt, tt, MOE_TOPK).transpose(0, 2, 1).reshape(n // tt, 1, MOE_TOPK * tt)
    out = _combine_final(x_rows, y_rows, dest, final_g.reshape(ROW_TILE, 128), tt)
    return out.reshape(n, D_MODEL)


def _blockdiag(pool_w):
    g, c, _ = pool_w.shape
    out = jnp.zeros((g * c, g * c), pool_w.dtype)
    for gi in range(g):
        out = out.at[gi * c:(gi + 1) * c, gi * c:(gi + 1) * c].set(pool_w[gi])
    return out


def _lane_row(v, offset):
    return jnp.zeros((1, BA_LANES), F32).at[0, offset:offset + v.shape[0]].set(v.astype(F32))


def kernel(x, mem, positions, mix_norm_g, w_in, pool_w, pool_scale, dn_conv_w, dn_a_log, dn_dt_bias, dn_norm_g,
           w_up_pool, w_up_moba, w_up_dn, w_out, xa_norm_g, mem_norm_g, xa_wq, xa_wkv, xa_wo, ffn_norm_g,
           dense_w_gate_up, dense_w_down, moe_router, moe_w_gate_up, moe_w_down, final_norm_g):
    batch, seq, d = x.shape
    depth = w_in.shape[0]
    mem_len = mem.shape[1]
    n = batch * seq
    nb = seq // MOBA_BLOCK
    assert d == D_MODEL and seq % MOBA_BLOCK == 0 and nb <= MOBA_HD and nb % MOBA_GROUP == 0 and depth == 2
    tm = min(512, seq)

    xf = x.reshape(n, d)
    mem2d = mem.reshape(batch * mem_len, d)
    cos_t, sin_t = _rope_tables(positions.reshape(n, 1).astype(F32), tm)
    off_ba = POOL_W + 3 * MOBA_W + 3 * DN_W

    for layer in range(depth):
        wl = w_in[layer]
        w_main = jnp.concatenate([wl[:, :off_ba], wl[:, off_ba + 2 * DN_HEADS:]], axis=1).astype(BF16)
        w_ba = jnp.pad(wl[:, off_ba:off_ba + 2 * DN_HEADS], ((0, 0), (0, BA_LANES - 2 * DN_HEADS))).astype(BF16)
        proj, ba = _inproj(xf, mix_norm_g[layer][None, :], w_main, w_ba, min(1024, seq))

        y_pool = _pool(proj, _blockdiag(pool_w[layer]).astype(BF16), pool_scale[layer][None, :], batch, seq, tm)
        mq, mk, mv, kbar = _moba_prep(proj, cos_t, sin_t, nb)
        kbar = jnp.pad(kbar.reshape(batch, nb, MOBA_W), ((0, 0), (0, MOBA_HD - nb), (0, 0)))
        y_moba = _moba(mq, mk, mv, kbar, batch, seq)
        w_moba = jnp.pad(w_up_moba[layer].reshape(MOBA_HEADS, MOBA_HD, d), ((0, 0), (0, MOBA_HD), (0, 0)))
        w_moba = w_moba.reshape(MOBA_AUG_W, d).astype(BF16)
        conv_w = dn_conv_w[layer].reshape(DN_CONV, 3, DN_W).transpose(1, 0, 2)
        y_dn = _deltanet(proj, ba, conv_w, _lane_row(dn_a_log[layer], DN_HEADS),
                         _lane_row(dn_dt_bias[layer], DN_HEADS), jnp.tile(dn_norm_g[layer], DN_HEADS)[None, :], batch, seq)
        xf = _merge(xf, y_pool, y_moba, y_dn, proj, w_up_pool[layer].astype(BF16), w_moba,
                    w_up_dn[layer].astype(BF16), w_out[layer].astype(BF16), tm)

        kv = _memkv(mem2d, mem_norm_g[layer][None, :], xa_wkv[layer].astype(BF16))
        xf = _xattn(xf, xa_norm_g[layer][None, :], xa_wq[layer].astype(BF16), kv, xa_wo[layer].astype(BF16),
                    seq, mem_len, tm)

        if layer % 2 == 0:
            xf = _ffn(xf, ffn_norm_g[layer][None, :], dense_w_gate_up[layer // 2].astype(BF16),
                      dense_w_down[layer // 2].astype(BF16), tm, dense_w_down.shape[1] // 2)
        else:
            router = jnp.pad(moe_router[layer // 2], ((0, 0), (0, 128 - N_EXPERTS))).astype(BF16)
            xf = _moe_sparse_final(xf, ffn_norm_g[layer][None, :], router, moe_w_gate_up[layer // 2].astype(BF16),
                                   moe_w_down[layer // 2].astype(BF16), final_norm_g, tm, min(256, seq))
    return xf.reshape(batch, seq, d)
```

```python
import functools

import numpy as np
import jax
import jax.numpy as jnp
from jax import lax
from jax.experimental import pallas as pl
from jax.experimental.pallas import tpu as pltpu

F32 = jnp.float32
BF16 = jnp.bfloat16

RMS_EPS = 1e-6
D_MODEL = 1024
POOL_W = 256
POOL_GROUP_W = 64
POOL_WINDOWS = (2, 4, 8, 16)
SEQ_HALO = 16
MOBA_HEADS = 4
MOBA_HD = 64
MOBA_W = 256
MOBA_AUG_W = 2 * MOBA_W
MOBA_BLOCK = 256
MOBA_TOPK = 3
MOBA_GROUP = 4
ROPE_THETA = 500000.0
ROPE_DIMS = 16
DN_HEADS = 8
DN_HD = 64
DN_W = 512
DN_CONV = 4
DN_CHUNK = 128
DN_STEP_CHUNKS = 2
XA_HEADS = 4
XA_HD = 128
XA_W = 512
N_EXPERTS = 8
MOE_TOPK = 2
ROW_TILE = 8
GATHER_UNROLL = 16
COL_POOL, COL_MQ, COL_MK, COL_MV = 0, 1, 2, 3
COL_DQ, COL_DZ = 2, 5
COL_GATE0 = 3
PROJ_COLS = 6144
PROJ_TN = 1024
PROJ_ID_BLOCKS = 3
BA_LANES = 128
VMEM_LIMIT = 56 * 1024 * 1024
NEG_BIG = -1e30


def _cparams(*sem):
    return pltpu.CompilerParams(dimension_semantics=sem, vmem_limit_bytes=VMEM_LIMIT)


def _rms(x, g):
    ms = jnp.mean(x * x, axis=-1, keepdims=True)
    return x * lax.rsqrt(ms + RMS_EPS) * g


def _silu(x):
    return x * jax.nn.sigmoid(x)


def _dot(a, b):
    return jnp.dot(a, b, preferred_element_type=F32)


def _dot_nt(a, b):
    return lax.dot_general(a, b, (((1,), (1,)), ((), ())), preferred_element_type=F32)


def _dot_tn(a, b):
    return lax.dot_general(a, b, (((0,), (0,)), ((), ())), preferred_element_type=F32)


def _inproj_kernel(x_ref, g_ref, w_ref, wba_ref, proj_ref, ba_ref, h_ref):
    j = pl.program_id(1)

    @pl.when(j == 0)
    def _():
        h = _rms(x_ref[...], g_ref[...]).astype(BF16)
        h_ref[...] = h
        ba_ref[...] = _dot(h, wba_ref[...])

    acc = _dot(h_ref[...], w_ref[...])

    @pl.when(j < PROJ_ID_BLOCKS)
    def _():
        proj_ref[...] = acc.astype(BF16)

    @pl.when(j >= PROJ_ID_BLOCKS)
    def _():
        proj_ref[...] = jax.nn.sigmoid(acc).astype(BF16)


def _inproj(x, g, w_main, w_ba, tm):
    n = x.shape[0]
    return pl.pallas_call(
        _inproj_kernel,
        grid=(n // tm, PROJ_COLS // PROJ_TN),
        in_specs=[
            pl.BlockSpec((tm, D_MODEL), lambda i, j: (i, 0)),
            pl.BlockSpec((1, D_MODEL), lambda i, j: (0, 0)),
            pl.BlockSpec((D_MODEL, PROJ_TN), lambda i, j: (0, j)),
            pl.BlockSpec((D_MODEL, BA_LANES), lambda i, j: (0, 0)),
        ],
        out_specs=[
            pl.BlockSpec((tm, PROJ_TN), lambda i, j: (i, j)),
            pl.BlockSpec((tm, BA_LANES), lambda i, j: (i, 0)),
        ],
        out_shape=[jax.ShapeDtypeStruct((n, PROJ_COLS), BF16),
                   jax.ShapeDtypeStruct((n, BA_LANES), F32)],
        scratch_shapes=[pltpu.VMEM((tm, D_MODEL), BF16)],
        compiler_params=_cparams("parallel", "arbitrary"),
        name="inproj",
    )(x, g, w_main, w_ba)


def _rope_table_kernel(pos_ref, freq_ref, sign_ref, c_ref, s_ref):
    ang = pos_ref[...] * freq_ref[...]
    rot = sign_ref[...]
    c_ref[...] = jnp.where(rot != 0.0, jnp.cos(ang), 1.0)
    s_ref[...] = jnp.sin(ang) * rot


def _rope_tables(pos_f32, tm):
    n = pos_f32.shape[0]
    half = ROPE_DIMS // 2
    inv_freq = np.power(ROPE_THETA, -np.arange(half, dtype=np.float32) * 2.0 / ROPE_DIMS).astype(np.float32)
    d = np.arange(128) % MOBA_HD
    freq = np.where(d < ROPE_DIMS, inv_freq[d % half], 0.0).astype(np.float32)[None, :]
    sign = np.where(d < half, -1.0, np.where(d < ROPE_DIMS, 1.0, 0.0)).astype(np.float32)[None, :]
    return pl.pallas_call(
        _rope_table_kernel,
        grid=(n // tm,),
        in_specs=[pl.BlockSpec((tm, 1), lambda i: (i, 0)),
                  pl.BlockSpec((1, 128), lambda i: (0, 0)),
                  pl.BlockSpec((1, 128), lambda i: (0, 0))],
        out_specs=[pl.BlockSpec((tm, 128), lambda i: (i, 0)),
                   pl.BlockSpec((tm, 128), lambda i: (i, 0))],
        out_shape=[jax.ShapeDtypeStruct((n, 128), F32), jax.ShapeDtypeStruct((n, 128), F32)],
        compiler_params=_cparams("parallel"),
        name="rope_tables",
    )(pos_f32, jnp.asarray(freq), jnp.asarray(sign))


def _moba_prep_kernel(q_ref, k_ref, v_ref, c_ref, s_ref, qo_ref, ko_ref, vo_ref, kbar_ref, *, nb):
    c = jnp.concatenate([c_ref[...], c_ref[...]], axis=1)
    s = jnp.concatenate([s_ref[...], s_ref[...]], axis=1)
    lane = lax.broadcasted_iota(jnp.int32, (MOBA_BLOCK, MOBA_W), 1)
    first_half = (lane % MOBA_HD) < (ROPE_DIMS // 2)

    def rope(x):
        partner = jnp.where(first_half,
                            pltpu.roll(x, MOBA_W - ROPE_DIMS // 2, 1),
                            pltpu.roll(x, ROPE_DIMS // 2, 1))
        return x * c + partner * s

    qo_ref[...] = (rope(q_ref[...].astype(F32)) * (MOBA_HD ** -0.5)).astype(BF16)
    k = rope(k_ref[...].astype(F32))
    kbar_ref[0] = jnp.mean(k, axis=0, keepdims=True)
    k16 = k.astype(BF16)
    v16 = v_ref[...]
    blk_lane = lax.broadcasted_iota(jnp.int32, (MOBA_BLOCK, MOBA_HD), 1)
    onehot = jnp.where(blk_lane == pl.program_id(0) % nb, 1.0, 0.0).astype(BF16)
    zeros = jnp.zeros((MOBA_BLOCK, MOBA_HD), BF16)
    k_parts, v_parts = [], []
    for h in range(MOBA_HEADS):
        sl = slice(h * MOBA_HD, (h + 1) * MOBA_HD)
        k_parts += [k16[:, sl], onehot]
        v_parts += [v16[:, sl], zeros]
    ko_ref[...] = jnp.concatenate(k_parts, axis=1)
    vo_ref[...] = jnp.concatenate(v_parts, axis=1)


def _moba_prep(proj, cos_t, sin_t, nb):
    n = proj.shape[0]
    nblk = n // MOBA_BLOCK
    return pl.pallas_call(
        functools.partial(_moba_prep_kernel, nb=nb),
        grid=(nblk,),
        in_specs=[pl.BlockSpec((MOBA_BLOCK, MOBA_W), lambda r: (r, COL_MQ)),
                  pl.BlockSpec((MOBA_BLOCK, MOBA_W), lambda r: (r, COL_MK)),
                  pl.BlockSpec((MOBA_BLOCK, MOBA_W), lambda r: (r, COL_MV)),
                  pl.BlockSpec((MOBA_BLOCK, 128), lambda r: (r, 0)),
                  pl.BlockSpec((MOBA_BLOCK, 128), lambda r: (r, 0))],
        out_specs=[pl.BlockSpec((MOBA_BLOCK, MOBA_W), lambda r: (r, 0)),
                   pl.BlockSpec((MOBA_BLOCK, MOBA_AUG_W), lambda r: (r, 0)),
                   pl.BlockSpec((MOBA_BLOCK, MOBA_AUG_W), lambda r: (r, 0)),
                   pl.BlockSpec((1, 1, MOBA_W), lambda r: (r, 0, 0))],
        out_shape=[jax.ShapeDtypeStruct((n, MOBA_W), BF16),
                   jax.ShapeDtypeStruct((n, MOBA_AUG_W), BF16),
                   jax.ShapeDtypeStruct((n, MOBA_AUG_W), BF16),
                   jax.ShapeDtypeStruct((nblk, 1, MOBA_W), F32)],
        compiler_params=_cparams("parallel"),
        name="moba_prep",
    )(proj, proj, proj, cos_t, sin_t)


def _moba_kernel(q_ref, k_ref, v_ref, kbar_ref, o_ref, qa_ref, m_ref, ls_ref, acc_ref, s_ref):
    i = pl.program_id(1)
    heads = range(MOBA_HEADS)
    blk = lax.broadcasted_iota(jnp.int32, (MOBA_BLOCK, MOBA_HD), 1).astype(F32)
    row = lax.broadcasted_iota(jnp.int32, (MOBA_BLOCK, MOBA_BLOCK), 0)
    col = lax.broadcasted_iota(jnp.int32, (MOBA_BLOCK, MOBA_BLOCK), 1)
    causal = col <= row
    i_f = i.astype(F32)
    own = pl.multiple_of(i * MOBA_BLOCK, MOBA_BLOCK)
    hs = [slice(h * 2 * MOBA_HD, (h + 1) * 2 * MOBA_HD) for h in heads]

    for h in heads:
        sl = slice(h * MOBA_HD, (h + 1) * MOBA_HD)
        qh = q_ref[:, sl]
        gate = _dot_nt(qh, kbar_ref[0, :, sl].astype(BF16))
        gate = jnp.where(blk < i_f, gate, -jnp.inf)
        keep = jnp.zeros((MOBA_BLOCK, MOBA_HD), jnp.bool_)
        for _ in range(MOBA_TOPK):
            m = jnp.max(gate, axis=1, keepdims=True)
            idx = jnp.min(jnp.where(gate == m, blk, float(MOBA_HD)), axis=1, keepdims=True)
            pick = (blk == idx) & (m > -jnp.inf)
            keep = keep | pick
            gate = jnp.where(pick, -jnp.inf, gate)
        qa_ref[h] = jnp.concatenate([qh, jnp.where(keep, 0.0, NEG_BIG).astype(BF16)], axis=1)
        qa_ref[MOBA_HEADS + h] = jnp.concatenate([qh, jnp.zeros_like(qh)], axis=1)

    def scores(h, start):
        return _dot_nt(qa_ref[h], k_ref[pl.ds(start, MOBA_BLOCK), hs[h]])

    def own_scores(h):
        s = _dot_nt(qa_ref[MOBA_HEADS + h], k_ref[pl.ds(own, MOBA_BLOCK), hs[h]])
        return jnp.where(causal, s, NEG_BIG)

    def fold(x):
        return x[:, :128], x[:, 128:]

    n_groups = (i + (MOBA_GROUP - 1)) // MOBA_GROUP

    def group_blocks(g):
        return [pl.multiple_of((g * MOBA_GROUP + t) * MOBA_BLOCK, MOBA_BLOCK) for t in range(MOBA_GROUP)]

    for h in heads:
        s = own_scores(h)
        m = jnp.max(s, axis=1, keepdims=True)
        a, b = fold(jnp.exp(s - m))
        ls_ref[h] = a + b
        acc_ref[h] = _dot(jnp.concatenate([a, b], axis=1).astype(BF16), v_ref[pl.ds(own, MOBA_BLOCK), hs[h]])
        m_ref[h] = jnp.broadcast_to(m, (MOBA_BLOCK, 128))

    @pl.loop(0, n_groups)
    def _(g):
        starts = group_blocks(g)
        gmax = [None] * MOBA_HEADS
        for t, start in enumerate(starts):
            for h in heads:
                s = scores(h, start)
                s_ref[t, h] = s
                a, b = fold(s)
                ab = jnp.maximum(a, b)
                gmax[h] = ab if gmax[h] is None else jnp.maximum(gmax[h], ab)
        m_new = []
        for h in heads:
            m_old = m_ref[h]
            m_new.append(jnp.maximum(m_old, jnp.max(gmax[h], axis=1, keepdims=True)))
            alpha = jnp.exp(m_old - m_new[h])
            ls_ref[h] = ls_ref[h] * alpha
            acc_ref[h] = acc_ref[h] * alpha
            m_ref[h] = m_new[h]
        for t, start in enumerate(starts):
            for h in heads:
                a = jnp.exp(s_ref[t, h, :, :128] - m_new[h])
                b = jnp.exp(s_ref[t, h, :, 128:] - m_new[h])
                ls_ref[h] += a + b
                acc_ref[h] += _dot(jnp.concatenate([a, b], axis=1).astype(BF16),
                                   v_ref[pl.ds(start, MOBA_BLOCK), hs[h]])

    for h in heads:
        o_ref[:, hs[h]] = (acc_ref[h] / jnp.sum(ls_ref[h], axis=1, keepdims=True)).astype(BF16)


def _moba(q, k_aug, v_aug, kbar, batch, seq):
    n = q.shape[0]
    nb = seq // MOBA_BLOCK
    return pl.pallas_call(
        _moba_kernel,
        grid=(batch, nb),
        in_specs=[pl.BlockSpec((MOBA_BLOCK, MOBA_W), lambda b, i: (b * nb + i, 0)),
                  pl.BlockSpec((seq, MOBA_AUG_W), lambda b, i: (b, 0)),
                  pl.BlockSpec((seq, MOBA_AUG_W), lambda b, i: (b, 0)),
                  pl.BlockSpec((1, MOBA_HD, MOBA_W), lambda b, i: (b, 0, 0))],
        out_specs=pl.BlockSpec((MOBA_BLOCK, MOBA_AUG_W), lambda b, i: (b * nb + i, 0)),
        out_shape=jax.ShapeDtypeStruct((n, MOBA_AUG_W), BF16),
        scratch_shapes=[pltpu.VMEM((2 * MOBA_HEADS, MOBA_BLOCK, 2 * MOBA_HD), BF16)]
        + [pltpu.VMEM((MOBA_HEADS, MOBA_BLOCK, 2 * MOBA_HD), F32)] * 3
        + [pltpu.VMEM((MOBA_GROUP, MOBA_HEADS, MOBA_BLOCK, MOBA_BLOCK), F32)],
        compiler_params=_cparams("parallel", "arbitrary"),
        name="moba",
    )(q, k_aug, v_aug, kbar)


def _pool_kernel(p_ref, halo_ref, w_ref, scale_ref, o_ref, *, ts):
    i = pl.program_id(1)
    halo = jnp.where(i == 0, 0.0, halo_ref[...].astype(F32))
    p = p_ref[...].astype(F32)
    cur = jnp.concatenate([halo, p], axis=0)
    lane = lax.broadcasted_iota(jnp.int32, (ts, POOL_W), 1)
    t1 = (lax.broadcasted_iota(jnp.int32, (ts, POOL_W), 0) + i * ts + 1).astype(F32)
    total = jnp.zeros((ts, POOL_W), F32)
    count = jnp.ones((ts, POOL_W), F32)
    span = 1
    for gi, w in enumerate(POOL_WINDOWS):
        while span < w:
            cur = cur + pltpu.roll(cur, span, 0)
            span *= 2
        in_group = (lane >= gi * POOL_GROUP_W) & (lane < (gi + 1) * POOL_GROUP_W)
        total = jnp.where(in_group, cur[SEQ_HALO:], total)
        count = jnp.where(in_group, jnp.minimum(t1, float(w)), count)
    pooled = total / count - p
    o_ref[...] = (_dot(pooled.astype(BF16), w_ref[...]) * scale_ref[...]).astype(BF16)


def _pool(proj, w_blockdiag, scale, batch, seq, ts):
    n = proj.shape[0]
    nt = seq // ts
    hb = ts // SEQ_HALO
    return pl.pallas_call(
        functools.partial(_pool_kernel, ts=ts),
        grid=(batch, nt),
        in_specs=[pl.BlockSpec((ts, POOL_W), lambda b, i: (b * nt + i, COL_POOL)),
                  pl.BlockSpec((SEQ_HALO, POOL_W), lambda b, i: (jnp.maximum((b * nt + i) * hb - 1, 0), COL_POOL)),
                  pl.BlockSpec((POOL_W, POOL_W), lambda b, i: (0, 0)),
                  pl.BlockSpec((1, POOL_W), lambda b, i: (0, 0))],
        out_specs=pl.BlockSpec((ts, POOL_W), lambda b, i: (b * nt + i, 0)),
        out_shape=jax.ShapeDtypeStruct((n, POOL_W), BF16),
        compiler_params=_cparams("parallel", "parallel"),
        name="pool",
    )(proj, proj, w_blockdiag, scale)


def _deltanet_kernel(q_ref, k_ref, v_ref, qh_ref, kh_ref, vh_ref, z_ref, ba_ref, cw_ref, alog_ref, dtb_ref,
                     ng_ref, expb_ref, expg_ref, o_ref, state_ref, oraw_ref):
    i = pl.program_id(1)
    C = DN_CHUNK
    TS = DN_STEP_CHUNKS * C

    @pl.when(i == 0)
    def _():
        state_ref[...] = jnp.zeros_like(state_ref)

    def conv(x_ref, halo_ref, w):
        halo = jnp.where(i == 0, 0.0, halo_ref[...].astype(F32))
        ext = jnp.concatenate([halo, x_ref[...].astype(F32)], axis=0)
        y = ext * w[DN_CONV - 1:DN_CONV]
        for lag in range(1, DN_CONV):
            y = y + pltpu.roll(ext, lag, 0) * w[DN_CONV - 1 - lag:DN_CONV - lag]
        return _silu(y[SEQ_HALO:])

    cw = cw_ref[...]
    qc = conv(q_ref, qh_ref, cw[0])
    kc = conv(k_ref, kh_ref, cw[1])
    vc = conv(v_ref, vh_ref, cw[2])

    ba = ba_ref[...]
    beta = jax.nn.sigmoid(ba)
    g = -jnp.exp(alog_ref[...]) * jax.nn.softplus(ba + dtb_ref[...])
    rows = lax.broadcasted_iota(jnp.int32, (TS, BA_LANES), 0) & (C - 1)
    G = g
    span = 1
    while span < C:
        G = G + jnp.where(rows >= span, pltpu.roll(G, span, 0), 0.0)
        span *= 2
    GT = G.T
    r_i = lax.broadcasted_iota(jnp.int32, (C, C), 0)
    c_i = lax.broadcasted_iota(jnp.int32, (C, C), 1)
    tril = c_i <= r_i
    eye = (c_i == r_i).astype(F32)
    levels = C.bit_length() - 1
    level_masks = [((r_i >> k) & 1 == 1) & ((c_i >> k) == (r_i >> k) - 1) for k in range(levels)]
    expb = expb_ref[...]
    expg = expg_ref[...]

    def per_head(x, e):
        hi = x.astype(BF16)
        lo = (x - hi.astype(F32)).astype(BF16)
        return _dot(hi, e) + _dot(lo, e)

    def inv_norm(x):
        ss = _dot_nt((x * x).astype(BF16), expb)
        return per_head(lax.rsqrt(ss + RMS_EPS), expb)

    qn_all = qc * (inv_norm(qc) * (DN_HD ** -0.5))
    kn_all = kc * inv_norm(kc)
    beta_all = per_head(beta, expb)
    eG_all = per_head(jnp.exp(G), expg)
    G_last_rows = jnp.concatenate(
        [jnp.broadcast_to(G[(c + 1) * C - 1:(c + 1) * C, :], (C, BA_LANES)) for c in range(DN_STEP_CHUNKS)], axis=0)
    kb_all = kn_all * beta_all
    vb_all = vc * beta_all
    kbe_all = kb_all * eG_all
    qn16 = qn_all.astype(BF16)
    kn16 = kn_all.astype(BF16)
    kb16 = kb_all.astype(BF16)
    qe16 = (qn_all * eG_all).astype(BF16)
    kdec16 = (kn_all * per_head(jnp.exp(G_last_rows - G), expg)).astype(BF16)

    heads = range(DN_HEADS)
    units = [(c, h) for c in range(DN_STEP_CHUNKS) for h in heads]
    rs = [slice(c * C, (c + 1) * C) for c in range(DN_STEP_CHUNKS)]
    sls = [slice(h * DN_HD, (h + 1) * DN_HD) for h in heads]
    G_c, A, aqk, Z = {}, {}, {}, {}
    for u in units:
        c, h = u
        G_c[u] = G[rs[c], DN_HEADS + h:DN_HEADS + h + 1]
        gram = _dot_nt(jnp.concatenate([kb16[rs[c], sls[h]], qn16[rs[c], sls[h]]], axis=0), kn16[rs[c], sls[h]])
        G_r = GT[DN_HEADS + h:DN_HEADS + h + 1, rs[c]]
        decay = jnp.exp(jnp.where(tril, G_c[u] - G_r, -jnp.inf))
        A[u] = (gram[:C] * decay).astype(BF16)
        aqk[u] = (gram[C:] * decay).astype(BF16)
    for u in units:
        c, h = u
        X = jnp.concatenate([vb_all[rs[c], sls[h]], kbe_all[rs[c], sls[h]]], axis=1)
        L1 = jnp.where(level_masks[0], A[u], 0.0)
        Z[u] = jnp.concatenate([eye - L1.astype(F32), X - _dot(L1, X.astype(BF16))], axis=1)
    for lvl in range(1, levels):
        Z16 = {u: Z[u].astype(BF16) for u in units}
        cols = slice(0, C + 2 * DN_HD) if lvl < levels - 1 else slice(C, C + 2 * DN_HD)
        Y = {u: _dot(jnp.where(level_masks[lvl], A[u], 0.0), Z16[u][:, cols]).astype(BF16) for u in units}
        Z = {u: Z[u][:, cols] - _dot(Z16[u][:, :C], Y[u]) for u in units}
    S = [state_ref[h] for h in heads]
    for c in range(DN_STEP_CHUNKS):
        us = [(c, h) for h in heads]
        ws = [_dot(jnp.concatenate([Z[u][:, DN_HD:].astype(BF16), qe16[rs[c], sls[u[1]]]], axis=0),
                   S[u[1]].astype(BF16)) for u in us]
        v16 = [(Z[u][:, :DN_HD] - ws[u[1]][:C]).astype(BF16) for u in us]
        for u in us:
            h = u[1]
            G_last = G_c[u][C - 1:C, :]
            S[h] = S[h] * jnp.exp(G_last) + _dot_tn(kdec16[rs[c], sls[h]], v16[h])
        for u in us:
            h = u[1]
            oraw_ref[rs[c], sls[h]] = ws[h][C:] + _dot(aqk[u], v16[h])
    for h in heads:
        state_ref[h] = S[h]
    o = oraw_ref[...]
    ms = _dot_nt((o * o).astype(BF16), expb) * (1.0 / DN_HD)
    o = o * per_head(lax.rsqrt(ms + RMS_EPS), expb) * ng_ref[...]
    o_ref[...] = (o * _silu(z_ref[...].astype(F32))).astype(BF16)


def _deltanet(proj, ba, conv_w, alog_row, dtb_row, norm_g, batch, seq):
    n = proj.shape[0]
    C = DN_STEP_CHUNKS * DN_CHUNK
    nt = seq // C
    hb = C // SEQ_HALO

    def head_lanes(first_row):
        m = np.zeros((BA_LANES, DN_W), np.float32)
        for h in range(DN_HEADS):
            m[first_row + h, h * DN_HD:(h + 1) * DN_HD] = 1.0
        return m

    def cur(col):
        return pl.BlockSpec((C, DN_W), lambda b, i: (b * nt + i, col))

    def halo(col):
        return pl.BlockSpec((SEQ_HALO, DN_W), lambda b, i: (jnp.maximum((b * nt + i) * hb - 1, 0), col))

    return pl.pallas_call(
        _deltanet_kernel,
        grid=(batch, nt),
        in_specs=[cur(COL_DQ), cur(COL_DQ + 1), cur(COL_DQ + 2),
                  halo(COL_DQ), halo(COL_DQ + 1), halo(COL_DQ + 2),
                  cur(COL_DZ),
                  pl.BlockSpec((C, BA_LANES), lambda b, i: (b * nt + i, 0)),
                  pl.BlockSpec((3, DN_CONV, DN_W), lambda b, i: (0, 0, 0)),
                  pl.BlockSpec((1, BA_LANES), lambda b, i: (0, 0)),
                  pl.BlockSpec((1, BA_LANES), lambda b, i: (0, 0)),
                  pl.BlockSpec((1, DN_W), lambda b, i: (0, 0)),
                  pl.BlockSpec((BA_LANES, DN_W), lambda b, i: (0, 0)),
                  pl.BlockSpec((BA_LANES, DN_W), lambda b, i: (0, 0))],
        out_specs=pl.BlockSpec((C, DN_W), lambda b, i: (b * nt + i, 0)),
        out_shape=jax.ShapeDtypeStruct((n, DN_W), BF16),
        scratch_shapes=[pltpu.VMEM((DN_HEADS, DN_HD, DN_HD), F32), pltpu.VMEM((C, DN_W), F32)],
        compiler_params=_cparams("parallel", "arbitrary"),
        name="deltanet",
    )(proj, proj, proj, proj, proj, proj, proj, ba, conv_w, alog_row, dtb_row, norm_g,
      jnp.asarray(head_lanes(0), BF16), jnp.asarray(head_lanes(DN_HEADS), BF16))


def _merge_kernel(x_ref, yp_ref, ym_ref, yd_ref, g0_ref, g1_ref, g2_ref, wp_ref, wm_ref, wd_ref, wo_ref, o_ref):
    merged = (g0_ref[...].astype(F32) * _dot(yp_ref[...], wp_ref[...])
              + g1_ref[...].astype(F32) * _dot(ym_ref[...], wm_ref[...])
              + g2_ref[...].astype(F32) * _dot(yd_ref[...], wd_ref[...]))
    o_ref[...] = x_ref[...] + _dot(merged.astype(BF16), wo_ref[...])


def _merge(x, y_pool, y_moba, y_dn, proj, w_up_pool, w_up_moba, w_up_dn, w_out, tm):
    n = x.shape[0]

    def rows(width, col=0):
        return pl.BlockSpec((tm, width), lambda i: (i, col))

    def whole(shape):
        return pl.BlockSpec(shape, lambda i: (0, 0))

    return pl.pallas_call(
        _merge_kernel,
        grid=(n // tm,),
        in_specs=[rows(D_MODEL), rows(POOL_W), rows(MOBA_AUG_W), rows(DN_W),
                  rows(D_MODEL, COL_GATE0), rows(D_MODEL, COL_GATE0 + 1), rows(D_MODEL, COL_GATE0 + 2),
                  whole((POOL_W, D_MODEL)), whole((MOBA_AUG_W, D_MODEL)), whole((DN_W, D_MODEL)),
                  whole((D_MODEL, D_MODEL))],
        out_specs=rows(D_MODEL),
        out_shape=jax.ShapeDtypeStruct((n, D_MODEL), F32),
        compiler_params=_cparams("parallel"),
        name="merge",
    )(x, y_pool, y_moba, y_dn, proj, proj, proj, w_up_pool, w_up_moba, w_up_dn, w_out)


def _memkv_kernel(mem_ref, g_ref, w_ref, o_ref):
    o_ref[...] = _dot(_rms(mem_ref[...], g_ref[...]).astype(BF16), w_ref[...]).astype(BF16)


def _memkv(mem2d, g, wkv):
    m = mem2d.shape[0]
    tm = 256
    return pl.pallas_call(
        _memkv_kernel,
        grid=(m // tm,),
        in_specs=[pl.BlockSpec((tm, D_MODEL), lambda i: (i, 0)),
                  pl.BlockSpec((1, D_MODEL), lambda i: (0, 0)),
                  pl.BlockSpec((D_MODEL, 2 * XA_W), lambda i: (0, 0))],
        out_specs=pl.BlockSpec((tm, 2 * XA_W), lambda i: (i, 0)),
        out_shape=jax.ShapeDtypeStruct((m, 2 * XA_W), BF16),
        compiler_params=_cparams("parallel"),
        name="memkv",
    )(mem2d, g, wkv)


def _xattn_kernel(x_ref, g_ref, wq_ref, kv_ref, wo_ref, o_ref):
    x = x_ref[...]
    q = _dot(_rms(x, g_ref[...]).astype(BF16), wq_ref[...]).astype(BF16)
    scale = XA_HD ** -0.5
    outs = []
    for h in range(XA_HEADS):
        sl = slice(h * XA_HD, (h + 1) * XA_HD)
        k = kv_ref[:, h * XA_HD:(h + 1) * XA_HD]
        v = kv_ref[:, XA_W + h * XA_HD:XA_W + (h + 1) * XA_HD]
        s = _dot_nt(q[:, sl], k) * scale
        s = s - jnp.max(s, axis=1, keepdims=True)
        p = jnp.exp(s)
        p = p / jnp.sum(p, axis=1, keepdims=True)
        outs.append(_dot(p.astype(BF16), v).astype(BF16))
    o = jnp.concatenate(outs, axis=1)
    o_ref[...] = x + _dot(o, wo_ref[...])


def _xattn(x, g, wq, kv, wo, seq, mem_len, tm):
    n = x.shape[0]
    tiles_per_seq = seq // tm
    return pl.pallas_call(
        _xattn_kernel,
        grid=(n // tm,),
        in_specs=[pl.BlockSpec((tm, D_MODEL), lambda i: (i, 0)),
                  pl.BlockSpec((1, D_MODEL), lambda i: (0, 0)),
                  pl.BlockSpec((D_MODEL, XA_W), lambda i: (0, 0)),
                  pl.BlockSpec((mem_len, 2 * XA_W), lambda i: (i // tiles_per_seq, 0)),
                  pl.BlockSpec((XA_W, D_MODEL), lambda i: (0, 0))],
        out_specs=pl.BlockSpec((tm, D_MODEL), lambda i: (i, 0)),
        out_shape=jax.ShapeDtypeStruct((n, D_MODEL), F32),
        compiler_params=_cparams("parallel"),
        name="xattn",
    )(x, g, wq, kv, wo)


def _swiglu_chunk(h, wg_ref, wu_ref, wd_ref):
    a = _silu(_dot(h, wg_ref[...])) * _dot(h, wu_ref[...])
    return _dot(a.astype(BF16), wd_ref[...])


def _ffn_kernel(x_ref, g_ref, wg_ref, wu_ref, wd_ref, o_ref, h_ref, acc_ref):
    j = pl.program_id(1)

    @pl.when(j == 0)
    def _():
        h_ref[...] = _rms(x_ref[...], g_ref[...]).astype(BF16)
        acc_ref[...] = x_ref[...]

    acc_ref[...] += _swiglu_chunk(h_ref[...], wg_ref, wu_ref, wd_ref)

    @pl.when(j == pl.num_programs(1) - 1)
    def _():
        o_ref[...] = acc_ref[...]


def _ffn(x, g, w_gate_up, w_down, tm, tf):
    n = x.shape[0]
    ff = w_down.shape[0]
    nf = ff // tf
    return pl.pallas_call(
        _ffn_kernel,
        grid=(n // tm, nf),
        in_specs=[pl.BlockSpec((tm, D_MODEL), lambda i, j: (i, 0)),
                  pl.BlockSpec((1, D_MODEL), lambda i, j: (0, 0)),
                  pl.BlockSpec((D_MODEL, tf), lambda i, j: (0, j)),
                  pl.BlockSpec((D_MODEL, tf), lambda i, j: (0, nf + j)),
                  pl.BlockSpec((tf, D_MODEL), lambda i, j: (j, 0))],
        out_specs=pl.BlockSpec((tm, D_MODEL), lambda i, j: (i, 0)),
        out_shape=jax.ShapeDtypeStruct((n, D_MODEL), F32),
        scratch_shapes=[pltpu.VMEM((tm, D_MODEL), BF16), pltpu.VMEM((tm, D_MODEL), F32)],
        compiler_params=_cparams("parallel", "arbitrary"),
        name="ffn",
    )(x, g, w_gate_up, w_gate_up, w_down)


def _moe_kernel(x_ref, g_ref, r_ref, wg_ref, wu_ref, wd_ref, fg_ref, o_ref, h_ref, comb_ref, acc_ref, *, tm):
    e = pl.program_id(1)
    lane = lax.broadcasted_iota(jnp.int32, (tm, 128), 1).astype(F32)

    @pl.when(e == 0)
    def _():
        h = _rms(x_ref[...], g_ref[...]).astype(BF16)
        h_ref[...] = h
        acc_ref[...] = x_ref[...]
        logits = jnp.where(lane < float(N_EXPERTS), _dot(h, r_ref[...]), -jnp.inf)
        m1 = jnp.max(logits, axis=1, keepdims=True)
        i1 = jnp.min(jnp.where(logits == m1, lane, 128.0), axis=1, keepdims=True)
        rest = jnp.where(lane == i1, -jnp.inf, logits)
        m2 = jnp.max(rest, axis=1, keepdims=True)
        i2 = jnp.min(jnp.where(rest == m2, lane, 128.0), axis=1, keepdims=True)
        e2 = jnp.exp(m2 - m1)
        w1 = 1.0 / (1.0 + e2)
        w2 = e2 / (1.0 + e2)
        comb_ref[...] = jnp.where(lane == i1, w1, 0.0) + jnp.where(lane == i2, w2, 0.0)

    w_e = jnp.sum(jnp.where(lane == e.astype(F32), comb_ref[...], 0.0), axis=1, keepdims=True)
    acc_ref[...] += w_e * _swiglu_chunk(h_ref[...], wg_ref.at[0], wu_ref.at[0], wd_ref.at[0])

    @pl.when(e == pl.num_programs(1) - 1)
    def _():
        o_ref[...] = _rms(acc_ref[...], fg_ref[...])


def _moe_final(x, g, router, w_gate_up, w_down, final_g, tm):
    n = x.shape[0]
    ff = w_down.shape[1]
    return pl.pallas_call(
        functools.partial(_moe_kernel, tm=tm),
        grid=(n // tm, N_EXPERTS),
        in_specs=[pl.BlockSpec((tm, D_MODEL), lambda i, e: (i, 0)),
                  pl.BlockSpec((1, D_MODEL), lambda i, e: (0, 0)),
                  pl.BlockSpec((D_MODEL, 128), lambda i, e: (0, 0)),
                  pl.BlockSpec((1, D_MODEL, ff), lambda i, e: (e, 0, 0)),
                  pl.BlockSpec((1, D_MODEL, ff), lambda i, e: (e, 0, 1)),
                  pl.BlockSpec((1, ff, D_MODEL), lambda i, e: (e, 0, 0)),
                  pl.BlockSpec((1, D_MODEL), lambda i, e: (0, 0))],
        out_specs=pl.BlockSpec((tm, D_MODEL), lambda i, e: (i, 0)),
        out_shape=jax.ShapeDtypeStruct((n, D_MODEL), F32),
        scratch_shapes=[pltpu.VMEM((tm, D_MODEL), BF16), pltpu.VMEM((tm, 128), F32),
                        pltpu.VMEM((tm, D_MODEL), F32)],
        compiler_params=_cparams("parallel", "arbitrary"),
        name="moe",
    )(x, g, router, w_gate_up, w_gate_up, w_down, final_g)


def _route_kernel(x_ref, g_ref, r_ref, o_ref, *, tm):
    lane = lax.broadcasted_iota(jnp.int32, (tm, 128), 1).astype(F32)
    h = _rms(x_ref[...], g_ref[...]).astype(BF16)
    logits = jnp.where(lane < float(N_EXPERTS), _dot(h, r_ref[...]), -jnp.inf)
    m1 = jnp.max(logits, axis=1, keepdims=True)
    i1 = jnp.min(jnp.where(logits == m1, lane, 128.0), axis=1, keepdims=True)
    rest = jnp.where(lane == i1, -jnp.inf, logits)
    m2 = jnp.max(rest, axis=1, keepdims=True)
    i2 = jnp.min(jnp.where(rest == m2, lane, 128.0), axis=1, keepdims=True)
    e2 = jnp.exp(m2 - m1)
    o_ref[...] = (jnp.where(lane == 0.0, i1, 0.0) + jnp.where(lane == 1.0, i2, 0.0)
                  + jnp.where(lane == 2.0, 1.0 / (1.0 + e2), 0.0) + jnp.where(lane == 3.0, e2 / (1.0 + e2), 0.0))


def _route(x, g, router, tm):
    n = x.shape[0]
    return pl.pallas_call(
        functools.partial(_route_kernel, tm=tm),
        grid=(n // tm,),
        in_specs=[pl.BlockSpec((tm, D_MODEL), lambda i: (i, 0)),
                  pl.BlockSpec((1, D_MODEL), lambda i: (0, 0)),
                  pl.BlockSpec((D_MODEL, 128), lambda i: (0, 0))],
        out_specs=pl.BlockSpec((tm, 128), lambda i: (i, 0)),
        out_shape=jax.ShapeDtypeStruct((n, 128), F32),
        compiler_params=_cparams("parallel"),
        name="moe_route",
    )(x, g, router)


def _row_gather(idx_smem, src_hbm, dst, count, sem):
    def row_copy(r):
        s0 = pl.multiple_of(idx_smem[r] * ROW_TILE, ROW_TILE)
        d0 = pl.multiple_of(r * ROW_TILE, ROW_TILE)
        return pltpu.make_async_copy(src_hbm.at[pl.ds(s0, ROW_TILE), :], dst.at[pl.ds(d0, ROW_TILE), :], sem)

    def start():
        @pl.loop(0, count, step=GATHER_UNROLL)
        def _(r0):
            for u in range(GATHER_UNROLL):
                row_copy(r0 + u).start()

    def wait():
        @pl.loop(0, count, step=GATHER_UNROLL)
        def _(r0):
            for u in range(GATHER_UNROLL):
                row_copy(r0 + u).wait()

    return start, wait


def _gather_ahead(step, n_steps, idx_now, idx_next, idx_smem, src_hbm, buf, count, sems):
    slot = step % 2

    def begin(idx_vmem, sl):
        stage = pltpu.make_async_copy(idx_vmem, idx_smem[sl], sems.at[2])
        stage.start()
        stage.wait()
        _row_gather(idx_smem[sl], src_hbm, buf.at[sl], count, sems.at[sl])[0]()

    @pl.when(step == 0)
    def _():
        begin(idx_now, 0)

    for sl in range(2):
        @pl.when(slot == sl)
        def _(sl=sl):
            @pl.when(step + 1 < n_steps)
            def _():
                begin(idx_next, 1 - sl)

            _row_gather(idx_smem[sl], src_hbm, buf.at[sl], count, sems.at[sl])[1]()

    return slot


def _rms_row_tiles(v, g8, rows):
    v3 = v.reshape(rows, ROW_TILE, 128)
    ms = jnp.sum(jnp.sum(v3 * v3, axis=2, keepdims=True), axis=1, keepdims=True) * (1.0 / D_MODEL)
    return (v3 * lax.rsqrt(ms + RMS_EPS) * g8[None]).reshape(rows * ROW_TILE, 128)


def _expert_kernel(te_ref, nu_ref, x_hbm, src_ref, src_next_ref, w_ref, g8_ref, wg_ref, wu_ref, wd_ref, o_ref,
                   idx0_ref, idx1_ref, xbuf_ref, x16_ref, sems, *, tm):
    t = pl.program_id(0)
    del te_ref

    @pl.when(t >= nu_ref[0])
    def _():
        o_ref[...] = jnp.zeros_like(o_ref)

    @pl.when(t < nu_ref[0])
    def _():
        slot = _gather_ahead(t, nu_ref[0], src_ref.at[0, 0], src_next_ref.at[0, 0], (idx0_ref, idx1_ref), x_hbm, xbuf_ref,
                             tm, sems)
        xb = xbuf_ref.at[slot]
        xb[...] = _rms_row_tiles(xb[...], g8_ref[...], tm)
        for s in range(ROW_TILE):
            x16_ref[:, s * 128:(s + 1) * 128] = xb[pl.ds(s, tm, stride=ROW_TILE), :].astype(BF16)
        y = _swiglu_chunk(x16_ref[...], wg_ref.at[0], wu_ref.at[0], wd_ref.at[0]) * w_ref[...]
        for s in range(ROW_TILE):
            o_ref[pl.ds(s, tm, stride=ROW_TILE), :] = y[:, s * 128:(s + 1) * 128]


def _experts(x_rows, src, w_sorted, tile_expert, n_used, g8, w_gate_up, w_down, tm):
    nt = src.shape[0]
    ff = w_down.shape[1]
    grid_spec = pltpu.PrefetchScalarGridSpec(
        num_scalar_prefetch=2,
        grid=(nt,),
        in_specs=[pl.BlockSpec(memory_space=pl.ANY),
                  pl.BlockSpec((1, 1, tm), lambda t, te, nu: (t, 0, 0)),
                  pl.BlockSpec((1, 1, tm), lambda t, te, nu: (jnp.minimum(t + 1, nt - 1), 0, 0)),
                  pl.BlockSpec((tm, 1), lambda t, te, nu: (t, 0)),
                  pl.BlockSpec((ROW_TILE, 128), lambda t, te, nu: (0, 0)),
                  pl.BlockSpec((1, D_MODEL, ff), lambda t, te, nu: (te[t], 0, 0)),
                  pl.BlockSpec((1, D_MODEL, ff), lambda t, te, nu: (te[t], 0, 1)),
                  pl.BlockSpec((1, ff, D_MODEL), lambda t, te, nu: (te[t], 0, 0))],
        out_specs=pl.BlockSpec((tm * ROW_TILE, 128), lambda t, te, nu: (t, 0)),
        scratch_shapes=[pltpu.SMEM((tm,), jnp.int32), pltpu.SMEM((tm,), jnp.int32),
                        pltpu.VMEM((2, tm * ROW_TILE, 128), F32),
                        pltpu.VMEM((tm, D_MODEL), BF16), pltpu.SemaphoreType.DMA((3,))])
    return pl.pallas_call(
        functools.partial(_expert_kernel, tm=tm),
        grid_spec=grid_spec,
        out_shape=jax.ShapeDtypeStruct((nt * tm * ROW_TILE, 128), F32),
        compiler_params=_cparams("arbitrary"),
        name="moe_experts",
    )(tile_expert, n_used, x_rows, src, src, w_sorted, g8, w_gate_up, w_gate_up, w_down)


def _combine_kernel(x_ref, y_hbm, dest_ref, dest_next_ref, fg_ref, o_ref, idx0_ref, idx1_ref, ybuf_ref, sems, *, tt):
    slot = _gather_ahead(pl.program_id(0), pl.num_programs(0), dest_ref.at[0, 0], dest_next_ref.at[0, 0],
                         (idx0_ref, idx1_ref),
                         y_hbm, ybuf_ref, MOE_TOPK * tt, sems)
    yb = ybuf_ref.at[slot]
    n8 = tt * ROW_TILE
    for s in range(ROW_TILE):
        sl = slice(s * 128, (s + 1) * 128)
        o_ref[:, sl] = (x_ref[:, sl] + yb[pl.ds(s, tt, stride=ROW_TILE), :]
                        + yb[pl.ds(n8 + s, tt, stride=ROW_TILE), :])
    o_ref[...] = _rms(o_ref[...], fg_ref[...])


def _combine_final(x, y_rows, dest, fg, tt):
    nt = dest.shape[0]
    return pl.pallas_call(
        functools.partial(_combine_kernel, tt=tt),
        grid=(nt,),
        in_specs=[pl.BlockSpec((tt, D_MODEL), lambda i: (i, 0)),
                  pl.BlockSpec(memory_space=pl.ANY),
                  pl.BlockSpec((1, 1, MOE_TOPK * tt), lambda i: (i, 0, 0)),
                  pl.BlockSpec((1, 1, MOE_TOPK * tt), lambda i: (jnp.minimum(i + 1, nt - 1), 0, 0)),
                  pl.BlockSpec((1, D_MODEL), lambda i: (0, 0))],
        out_specs=pl.BlockSpec((tt, D_MODEL), lambda i: (i, 0)),
        out_shape=jax.ShapeDtypeStruct(x.shape, F32),
        scratch_shapes=[pltpu.SMEM((MOE_TOPK * tt,), jnp.int32), pltpu.SMEM((MOE_TOPK * tt,), jnp.int32),
                        pltpu.VMEM((2, MOE_TOPK * tt * ROW_TILE, 128), F32), pltpu.SemaphoreType.DMA((3,))],
        compiler_params=_cparams("arbitrary"),
        name="moe_combine",
    )(x, y_rows, dest, dest, fg)


def _moe_sparse_final(xf, g, router, w_gate_up, w_down, final_g, tm, tt):
    n = xf.shape[0]
    nk = MOE_TOPK * n
    route = _route(xf, g, router, tm)
    e_flat = route[:, :MOE_TOPK].astype(jnp.int32).reshape(nk)
    w_flat = route[:, MOE_TOPK:2 * MOE_TOPK].reshape(nk)
    onehot = (e_flat[:, None] == jnp.arange(N_EXPERTS, dtype=jnp.int32)[None, :]).astype(jnp.int32)
    csum = jnp.cumsum(onehot, axis=0)
    cnt = csum[-1]
    rank = jnp.take_along_axis(csum, e_flat[:, None], axis=1)[:, 0] - 1
    padded = ((cnt + tm - 1) // tm) * tm
    ends = jnp.cumsum(padded)
    off = ends - padded
    start = jnp.cumsum(cnt) - cnt
    dest = off[e_flat] + rank
    nt = nk // tm + N_EXPERTS
    slot = jnp.arange(nt * tm, dtype=jnp.int32)
    e_slot = jnp.minimum(jnp.searchsorted(ends, slot, side="right"), N_EXPERTS - 1).astype(jnp.int32)
    loc = slot - off[e_slot]
    valid = loc < cnt[e_slot]
    order = jnp.argsort(e_flat, stable=True).astype(jnp.int32)
    pair = order[jnp.clip(start[e_slot] + loc, 0, nk - 1)]
    src = jnp.where(valid, pair // MOE_TOPK, 0).reshape(nt, 1, tm)
    w_sorted = jnp.where(valid, w_flat[pair], 0.0).reshape(nt * tm, 1)
    tile_expert = e_slot[::tm]
    n_used = (ends[-1:] // tm).astype(jnp.int32)

    x_rows = xf.reshape(n * ROW_TILE, 128)
    y_rows = _experts(x_rows, src, w_sorted, tile_expert, n_used, g.reshape(ROW_TILE, 128), w_gate_up, w_down, tm)
    dest = dest.reshape(n // tt, tt, MOE_TOPK).transpose(0, 2, 1).reshape(n // tt, 1, MOE_TOPK * tt)
    return _combine_final(xf, y_rows, dest, final_g[None, :], tt)


def _blockdiag(pool_w):
    g, c, _ = pool_w.shape
    out = jnp.zeros((g * c, g * c), pool_w.dtype)
    for gi in range(g):
        out = out.at[gi * c:(gi + 1) * c, gi * c:(gi + 1) * c].set(pool_w[gi])
    return out


def _lane_row(v, offset):
    return jnp.zeros((1, BA_LANES), F32).at[0, offset:offset + v.shape[0]].set(v.astype(F32))


def kernel(x, mem, positions, mix_norm_g, w_in, pool_w, pool_scale, dn_conv_w, dn_a_log, dn_dt_bias, dn_norm_g,
           w_up_pool, w_up_moba, w_up_dn, w_out, xa_norm_g, mem_norm_g, xa_wq, xa_wkv, xa_wo, ffn_norm_g,
           dense_w_gate_up, dense_w_down, moe_router, moe_w_gate_up, moe_w_down, final_norm_g):
    batch, seq, d = x.shape
    depth = w_in.shape[0]
    mem_len = mem.shape[1]
    n = batch * seq
    nb = seq // MOBA_BLOCK
    assert d == D_MODEL and seq % MOBA_BLOCK == 0 and nb <= MOBA_HD and nb % MOBA_GROUP == 0 and depth == 2
    tm = min(512, seq)

    xf = x.reshape(n, d)
    mem2d = mem.reshape(batch * mem_len, d)
    cos_t, sin_t = _rope_tables(positions.reshape(n, 1).astype(F32), tm)
    off_ba = POOL_W + 3 * MOBA_W + 3 * DN_W

    for layer in range(depth):
        wl = w_in[layer]
        w_main = jnp.concatenate([wl[:, :off_ba], wl[:, off_ba + 2 * DN_HEADS:]], axis=1).astype(BF16)
        w_ba = jnp.pad(wl[:, off_ba:off_ba + 2 * DN_HEADS], ((0, 0), (0, BA_LANES - 2 * DN_HEADS))).astype(BF16)
        proj, ba = _inproj(xf, mix_norm_g[layer][None, :], w_main, w_ba, min(1024, seq))

        y_pool = _pool(proj, _blockdiag(pool_w[layer]).astype(BF16), pool_scale[layer][None, :], batch, seq, tm)
        mq, mk, mv, kbar = _moba_prep(proj, cos_t, sin_t, nb)
        kbar = jnp.pad(kbar.reshape(batch, nb, MOBA_W), ((0, 0), (0, MOBA_HD - nb), (0, 0)))
        y_moba = _moba(mq, mk, mv, kbar, batch, seq)
        w_moba = jnp.pad(w_up_moba[layer].reshape(MOBA_HEADS, MOBA_HD, d), ((0, 0), (0, MOBA_HD), (0, 0)))
        w_moba = w_moba.reshape(MOBA_AUG_W, d).astype(BF16)
        conv_w = dn_conv_w[layer].reshape(DN_CONV, 3, DN_W).transpose(1, 0, 2)
        y_dn = _deltanet(proj, ba, conv_w, _lane_row(dn_a_log[layer], DN_HEADS),
                         _lane_row(dn_dt_bias[layer], DN_HEADS), jnp.tile(dn_norm_g[layer], DN_HEADS)[None, :], batch, seq)
        xf = _merge(xf, y_pool, y_moba, y_dn, proj, w_up_pool[layer].astype(BF16), w_moba,
                    w_up_dn[layer].astype(BF16), w_out[layer].astype(BF16), tm)

        kv = _memkv(mem2d, mem_norm_g[layer][None, :], xa_wkv[layer].astype(BF16))
        xf = _xattn(xf, xa_norm_g[layer][None, :], xa_wq[layer].astype(BF16), kv, xa_wo[layer].astype(BF16),
                    seq, mem_len, tm)

        if layer % 2 == 0:
            xf = _ffn(xf, ffn_norm_g[layer][None, :], dense_w_gate_up[layer // 2].astype(BF16),
                      dense_w_down[layer // 2].astype(BF16), tm, dense_w_down.shape[1] // 2)
        else:
            router = jnp.pad(moe_router[layer // 2], ((0, 0), (0, 128 - N_EXPERTS))).astype(BF16)
            xf = _moe_sparse_final(xf, ffn_norm_g[layer][None, :], router, moe_w_gate_up[layer // 2].astype(BF16),
                                   moe_w_down[layer // 2].astype(BF16), final_norm_g, tm, min(256, seq))
    return xf.reshape(batch, seq, d)
```

```python
import functools

import numpy as np
import jax
import jax.numpy as jnp
from jax import lax
from jax.experimental import pallas as pl
from jax.experimental.pallas import tpu as pltpu

F32 = jnp.float32
BF16 = jnp.bfloat16

RMS_EPS = 1e-6
D_MODEL = 1024
POOL_W = 256
POOL_GROUP_W = 64
POOL_WINDOWS = (2, 4, 8, 16)
SEQ_HALO = 16
MOBA_HEADS = 4
MOBA_HD = 64
MOBA_W = 256
MOBA_AUG_W = 2 * MOBA_W
MOBA_BLOCK = 256
MOBA_TOPK = 3
MOBA_KBAR_ROWS = 128
MOBA_GROUP = 4
ROPE_THETA = 500000.0
ROPE_DIMS = 16
DN_HEADS = 8
DN_HD = 64
DN_W = 512
DN_CONV = 4
DN_CHUNK = 128
DN_STEP_CHUNKS = 2
XA_HEADS = 4
XA_HD = 128
XA_W = 512
N_EXPERTS = 8
MOE_TOPK = 2
ROW_TILE = 8
GATHER_UNROLL = 16
COL_POOL, COL_MQ, COL_MK, COL_MV = 0, 1, 2, 3
COL_DQ, COL_DZ = 2, 5
COL_GATE0 = 3
PROJ_COLS = 6144
PROJ_TN = 1024
PROJ_ID_BLOCKS = 3
BA_LANES = 128
VMEM_LIMIT = 56 * 1024 * 1024
NEG_BIG = -1e30


def _cparams(*sem):
    return pltpu.CompilerParams(dimension_semantics=sem, vmem_limit_bytes=VMEM_LIMIT)


def _rms(x, g):
    ms = jnp.mean(x * x, axis=-1, keepdims=True)
    return x * lax.rsqrt(ms + RMS_EPS) * g


def _silu(x):
    return x * jax.nn.sigmoid(x)


def _dot(a, b):
    return jnp.dot(a, b, preferred_element_type=F32)


def _dot_nt(a, b):
    return lax.dot_general(a, b, (((1,), (1,)), ((), ())), preferred_element_type=F32)


def _dot_tn(a, b):
    return lax.dot_general(a, b, (((0,), (0,)), ((), ())), preferred_element_type=F32)


def _inproj_kernel(x_ref, g_ref, w_ref, wba_ref, proj_ref, ba_ref, h_ref):
    j = pl.program_id(1)

    @pl.when(j == 0)
    def _():
        h = _rms(x_ref[...], g_ref[...]).astype(BF16)
        h_ref[...] = h
        ba_ref[...] = _dot(h, wba_ref[...])

    acc = _dot(h_ref[...], w_ref[...])

    @pl.when(j < PROJ_ID_BLOCKS)
    def _():
        proj_ref[...] = acc.astype(BF16)

    @pl.when(j >= PROJ_ID_BLOCKS)
    def _():
        proj_ref[...] = jax.nn.sigmoid(acc).astype(BF16)


def _inproj(x, g, w_main, w_ba, tm):
    n = x.shape[0]
    return pl.pallas_call(
        _inproj_kernel,
        grid=(n // tm, PROJ_COLS // PROJ_TN),
        in_specs=[
            pl.BlockSpec((tm, D_MODEL), lambda i, j: (i, 0)),
            pl.BlockSpec((1, D_MODEL), lambda i, j: (0, 0)),
            pl.BlockSpec((D_MODEL, PROJ_TN), lambda i, j: (0, j)),
            pl.BlockSpec((D_MODEL, BA_LANES), lambda i, j: (0, 0)),
        ],
        out_specs=[
            pl.BlockSpec((tm, PROJ_TN), lambda i, j: (i, j)),
            pl.BlockSpec((tm, BA_LANES), lambda i, j: (i, 0)),
        ],
        out_shape=[jax.ShapeDtypeStruct((n, PROJ_COLS), BF16),
                   jax.ShapeDtypeStruct((n, BA_LANES), F32)],
        scratch_shapes=[pltpu.VMEM((tm, D_MODEL), BF16)],
        compiler_params=_cparams("parallel", "arbitrary"),
        name="inproj",
    )(x, g, w_main, w_ba)


def _rope_table_kernel(pos_ref, freq_ref, sign_ref, c_ref, s_ref):
    ang = pos_ref[...] * freq_ref[...]
    rot = sign_ref[...]
    c_ref[...] = jnp.where(rot != 0.0, jnp.cos(ang), 1.0)
    s_ref[...] = jnp.sin(ang) * rot


def _rope_tables(pos_f32, tm):
    n = pos_f32.shape[0]
    half = ROPE_DIMS // 2
    inv_freq = np.power(ROPE_THETA, -np.arange(half, dtype=np.float32) * 2.0 / ROPE_DIMS).astype(np.float32)
    d = np.arange(128) % MOBA_HD
    freq = np.where(d < ROPE_DIMS, inv_freq[d % half], 0.0).astype(np.float32)[None, :]
    sign = np.where(d < half, -1.0, np.where(d < ROPE_DIMS, 1.0, 0.0)).astype(np.float32)[None, :]
    return pl.pallas_call(
        _rope_table_kernel,
        grid=(n // tm,),
        in_specs=[pl.BlockSpec((tm, 1), lambda i: (i, 0)),
                  pl.BlockSpec((1, 128), lambda i: (0, 0)),
                  pl.BlockSpec((1, 128), lambda i: (0, 0))],
        out_specs=[pl.BlockSpec((tm, 128), lambda i: (i, 0)),
                   pl.BlockSpec((tm, 128), lambda i: (i, 0))],
        out_shape=[jax.ShapeDtypeStruct((n, 128), F32), jax.ShapeDtypeStruct((n, 128), F32)],
        compiler_params=_cparams("parallel"),
        name="rope_tables",
    )(pos_f32, jnp.asarray(freq), jnp.asarray(sign))


def _moba_prep_kernel(q_ref, k_ref, v_ref, c_ref, s_ref, qo_ref, ko_ref, vo_ref, kbar_ref, *, nb):
    c = jnp.concatenate([c_ref[...], c_ref[...]], axis=1)
    s = jnp.concatenate([s_ref[...], s_ref[...]], axis=1)
    lane = lax.broadcasted_iota(jnp.int32, (MOBA_BLOCK, MOBA_W), 1)
    first_half = (lane % MOBA_HD) < (ROPE_DIMS // 2)

    def rope(x):
        partner = jnp.where(first_half,
                            pltpu.roll(x, MOBA_W - ROPE_DIMS // 2, 1),
                            pltpu.roll(x, ROPE_DIMS // 2, 1))
        return x * c + partner * s

    qo_ref[...] = (rope(q_ref[...].astype(F32)) * (MOBA_HD ** -0.5)).astype(BF16)
    k = rope(k_ref[...].astype(F32))
    kbar_ref[0] = jnp.mean(k, axis=0, keepdims=True)
    k16 = k.astype(BF16)
    blk_lane = lax.broadcasted_iota(jnp.int32, (MOBA_BLOCK, MOBA_HD), 1)
    onehot = jnp.where(blk_lane == pl.program_id(0) % nb, 1.0, 0.0).astype(BF16)
    k_parts = []
    for h in range(MOBA_HEADS):
        k_parts += [k16[:, h * MOBA_HD:(h + 1) * MOBA_HD], onehot]
    ko_ref[...] = jnp.concatenate(k_parts, axis=1)
    vo_ref[...] = v_ref[...].astype(F32).T.astype(BF16)


def _moba_prep(proj, cos_t, sin_t, nb):
    n = proj.shape[0]
    nblk = n // MOBA_BLOCK
    return pl.pallas_call(
        functools.partial(_moba_prep_kernel, nb=nb),
        grid=(nblk,),
        in_specs=[pl.BlockSpec((MOBA_BLOCK, MOBA_W), lambda r: (r, COL_MQ)),
                  pl.BlockSpec((MOBA_BLOCK, MOBA_W), lambda r: (r, COL_MK)),
                  pl.BlockSpec((MOBA_BLOCK, MOBA_W), lambda r: (r, COL_MV)),
                  pl.BlockSpec((MOBA_BLOCK, 128), lambda r: (r, 0)),
                  pl.BlockSpec((MOBA_BLOCK, 128), lambda r: (r, 0))],
        out_specs=[pl.BlockSpec((MOBA_BLOCK, MOBA_W), lambda r: (r, 0)),
                   pl.BlockSpec((MOBA_BLOCK, MOBA_AUG_W), lambda r: (r, 0)),
                   pl.BlockSpec((MOBA_W, MOBA_BLOCK), lambda r: (r, 0)),
                   pl.BlockSpec((1, 1, MOBA_W), lambda r: (r, 0, 0))],
        out_shape=[jax.ShapeDtypeStruct((n, MOBA_W), BF16),
                   jax.ShapeDtypeStruct((n, MOBA_AUG_W), BF16),
                   jax.ShapeDtypeStruct((nblk * MOBA_W, MOBA_BLOCK), BF16),
                   jax.ShapeDtypeStruct((nblk, 1, MOBA_W), F32)],
        compiler_params=_cparams("parallel"),
        name="moba_prep",
    )(proj, proj, proj, cos_t, sin_t)


def _moba_kernel(q_ref, k_ref, vt_ref, kbar_ref, o_ref, qa_ref, m_ref, ls_ref, acc_ref, s_ref):
    i = pl.program_id(1)
    heads = range(MOBA_HEADS)
    blk = lax.broadcasted_iota(jnp.int32, (MOBA_KBAR_ROWS, MOBA_BLOCK), 0).astype(F32)
    row = lax.broadcasted_iota(jnp.int32, (MOBA_BLOCK, MOBA_BLOCK), 0)
    col = lax.broadcasted_iota(jnp.int32, (MOBA_BLOCK, MOBA_BLOCK), 1)
    causal = row <= col
    i_f = i.astype(F32)
    own = pl.multiple_of(i * MOBA_BLOCK, MOBA_BLOCK)
    hs = [slice(h * 2 * MOBA_HD, (h + 1) * 2 * MOBA_HD) for h in heads]

    for h in heads:
        sl = slice(h * MOBA_HD, (h + 1) * MOBA_HD)
        qh = q_ref[:, sl]
        gate = _dot_nt(kbar_ref[0, :, sl].astype(BF16), qh)
        gate = jnp.where(blk < i_f, gate, -jnp.inf)
        keep = jnp.zeros(gate.shape, jnp.bool_)
        for _ in range(MOBA_TOPK):
            m = jnp.max(gate, axis=0, keepdims=True)
            idx = jnp.min(jnp.where(gate == m, blk, float(MOBA_KBAR_ROWS)), axis=0, keepdims=True)
            pick = (blk == idx) & (m > -jnp.inf)
            keep = keep | pick
            gate = jnp.where(pick, -jnp.inf, gate)
        bias = jnp.where(keep, 0.0, NEG_BIG).T[:, :MOBA_HD]
        qa_ref[h] = jnp.concatenate([qh, bias.astype(BF16)], axis=1)
        qa_ref[MOBA_HEADS + h] = jnp.concatenate([qh, jnp.zeros_like(qh)], axis=1)

    groups8 = MOBA_BLOCK // ROW_TILE

    def scores(h, start):
        return _dot_nt(k_ref[pl.ds(start, MOBA_BLOCK), hs[h]], qa_ref[h])

    def own_scores(h):
        s = _dot_nt(k_ref[pl.ds(own, MOBA_BLOCK), hs[h]], qa_ref[MOBA_HEADS + h])
        return jnp.where(causal, s, NEG_BIG)

    def rows8(x):
        return x.reshape(groups8, ROW_TILE, MOBA_BLOCK)

    def all8(x, op):
        return jnp.broadcast_to(op(x, axis=0, keepdims=True), x.shape)

    def values_t(h, start):
        return vt_ref[pl.ds(start + h * MOBA_HD, MOBA_HD), :]

    n_groups = (i + (MOBA_GROUP - 1)) // MOBA_GROUP

    def group_blocks(g):
        return [pl.multiple_of((g * MOBA_GROUP + t) * MOBA_BLOCK, MOBA_BLOCK) for t in range(MOBA_GROUP)]

    for h in heads:
        s3 = rows8(own_scores(h))
        m = all8(jnp.max(s3, axis=0), jnp.max)
        p3 = jnp.exp(s3 - m[None])
        ls_ref[h] = jnp.sum(p3, axis=0)
        acc_ref[h] = _dot(values_t(h, own), p3.reshape(MOBA_BLOCK, MOBA_BLOCK).astype(BF16))
        m_ref[h] = m

    def score_group(g):
        gmax = [None] * MOBA_HEADS
        for t, start in enumerate(group_blocks(g)):
            for h in heads:
                s = scores(h, start)
                s_ref[g % 2, t, h] = s
                smax = jnp.max(rows8(s), axis=0)
                gmax[h] = smax if gmax[h] is None else jnp.maximum(gmax[h], smax)
        return gmax

    def rescale(gmax):
        for h in heads:
            m_old = m_ref[h]
            m_new = jnp.maximum(m_old, all8(gmax[h], jnp.max))
            alpha = jnp.exp(m_old - m_new)
            ls_ref[h] = ls_ref[h] * alpha
            acc = acc_ref[h].reshape(MOBA_HD // ROW_TILE, ROW_TILE, MOBA_BLOCK) * alpha[None]
            acc_ref[h] = acc.reshape(MOBA_HD, MOBA_BLOCK)
            m_ref[h] = m_new

    def accumulate_group(g):
        for t, start in enumerate(group_blocks(g)):
            for h in heads:
                p3 = jnp.exp(rows8(s_ref[g % 2, t, h]) - m_ref[h][None])
                ls_ref[h] += jnp.sum(p3, axis=0)
                acc_ref[h] += _dot(values_t(h, start), p3.reshape(MOBA_BLOCK, MOBA_BLOCK).astype(BF16))

    @pl.when(n_groups > 0)
    def _():
        rescale(score_group(0))

    @pl.loop(1, n_groups)
    def _(g):
        accumulate_group(g - 1)
        rescale(score_group(g))

    @pl.when(n_groups > 0)
    def _():
        accumulate_group(n_groups - 1)

    for h in heads:
        o_t = acc_ref[h] / jnp.sum(ls_ref[h], axis=0, keepdims=True)
        o_ref[:, hs[h]] = jnp.concatenate([o_t, jnp.zeros_like(o_t)], axis=0).T.astype(BF16)


def _moba(q, k_aug, v_aug, kbar, batch, seq):
    n = q.shape[0]
    nb = seq // MOBA_BLOCK
    return pl.pallas_call(
        _moba_kernel,
        grid=(batch, nb),
        in_specs=[pl.BlockSpec((MOBA_BLOCK, MOBA_W), lambda b, i: (b * nb + i, 0)),
                  pl.BlockSpec((seq, MOBA_AUG_W), lambda b, i: (b, 0)),
                  pl.BlockSpec((seq, MOBA_W), lambda b, i: (b, 0)),
                  pl.BlockSpec((1, MOBA_KBAR_ROWS, MOBA_W), lambda b, i: (b, 0, 0))],
        out_specs=pl.BlockSpec((MOBA_BLOCK, MOBA_AUG_W), lambda b, i: (b * nb + i, 0)),
        out_shape=jax.ShapeDtypeStruct((n, MOBA_AUG_W), BF16),
        scratch_shapes=[pltpu.VMEM((2 * MOBA_HEADS, MOBA_BLOCK, 2 * MOBA_HD), BF16)]
        + [pltpu.VMEM((MOBA_HEADS, ROW_TILE, MOBA_BLOCK), F32)] * 2
        + [pltpu.VMEM((MOBA_HEADS, MOBA_HD, MOBA_BLOCK), F32)]
        + [pltpu.VMEM((2, MOBA_GROUP, MOBA_HEADS, MOBA_BLOCK, MOBA_BLOCK), F32)],
        compiler_params=_cparams("parallel", "arbitrary"),
        name="moba",
    )(q, k_aug, v_aug, kbar)


def _pool_kernel(p_ref, halo_ref, w_ref, scale_ref, o_ref, *, ts):
    i = pl.program_id(1)
    halo = jnp.where(i == 0, 0.0, halo_ref[...].astype(F32))
    p = p_ref[...].astype(F32)
    cur = jnp.concatenate([halo, p], axis=0)
    lane = lax.broadcasted_iota(jnp.int32, (ts, POOL_W), 1)
    t1 = (lax.broadcasted_iota(jnp.int32, (ts, POOL_W), 0) + i * ts + 1).astype(F32)
    total = jnp.zeros((ts, POOL_W), F32)
    count = jnp.ones((ts, POOL_W), F32)
    span = 1
    for gi, w in enumerate(POOL_WINDOWS):
        while span < w:
            cur = cur + pltpu.roll(cur, span, 0)
            span *= 2
        in_group = (lane >= gi * POOL_GROUP_W) & (lane < (gi + 1) * POOL_GROUP_W)
        total = jnp.where(in_group, cur[SEQ_HALO:], total)
        count = jnp.where(in_group, jnp.minimum(t1, float(w)), count)
    pooled = total / count - p
    o_ref[...] = (_dot(pooled.astype(BF16), w_ref[...]) * scale_ref[...]).astype(BF16)


def _pool(proj, w_blockdiag, scale, batch, seq, ts):
    n = proj.shape[0]
    nt = seq // ts
    hb = ts // SEQ_HALO
    return pl.pallas_call(
        functools.partial(_pool_kernel, ts=ts),
        grid=(batch, nt),
        in_specs=[pl.BlockSpec((ts, POOL_W), lambda b, i: (b * nt + i, COL_POOL)),
                  pl.BlockSpec((SEQ_HALO, POOL_W), lambda b, i: (jnp.maximum((b * nt + i) * hb - 1, 0), COL_POOL)),
                  pl.BlockSpec((POOL_W, POOL_W), lambda b, i: (0, 0)),
                  pl.BlockSpec((1, POOL_W), lambda b, i: (0, 0))],
        out_specs=pl.BlockSpec((ts, POOL_W), lambda b, i: (b * nt + i, 0)),
        out_shape=jax.ShapeDtypeStruct((n, POOL_W), BF16),
        compiler_params=_cparams("parallel", "parallel"),
        name="pool",
    )(proj, proj, w_blockdiag, scale)


def _deltanet_kernel(q_ref, k_ref, v_ref, qh_ref, kh_ref, vh_ref, z_ref, ba_ref, cw_ref, alog_ref, dtb_ref,
                     ng_ref, expb_ref, expg_ref, o_ref, state_ref, oraw_ref):
    i = pl.program_id(1)
    C = DN_CHUNK
    TS = DN_STEP_CHUNKS * C

    @pl.when(i == 0)
    def _():
        state_ref[...] = jnp.zeros_like(state_ref)

    def conv(x_ref, halo_ref, w):
        halo = jnp.where(i == 0, 0.0, halo_ref[...].astype(F32))
        ext = jnp.concatenate([halo, x_ref[...].astype(F32)], axis=0)
        y = ext * w[DN_CONV - 1:DN_CONV]
        for lag in range(1, DN_CONV):
            y = y + pltpu.roll(ext, lag, 0) * w[DN_CONV - 1 - lag:DN_CONV - lag]
        return _silu(y[SEQ_HALO:])

    cw = cw_ref[...]
    qc = conv(q_ref, qh_ref, cw[0])
    kc = conv(k_ref, kh_ref, cw[1])
    vc = conv(v_ref, vh_ref, cw[2])

    ba = ba_ref[...]
    beta = jax.nn.sigmoid(ba)
    g = -jnp.exp(alog_ref[...]) * jax.nn.softplus(ba + dtb_ref[...])
    rows = lax.broadcasted_iota(jnp.int32, (TS, BA_LANES), 0) & (C - 1)
    G = g
    span = 1
    while span < C:
        G = G + jnp.where(rows >= span, pltpu.roll(G, span, 0), 0.0)
        span *= 2
    GT = G.T
    r_i = lax.broadcasted_iota(jnp.int32, (C, C), 0)
    c_i = lax.broadcasted_iota(jnp.int32, (C, C), 1)
    tril = c_i <= r_i
    eye = (c_i == r_i).astype(F32)
    levels = C.bit_length() - 1
    level_masks = [((r_i >> k) & 1 == 1) & ((c_i >> k) == (r_i >> k) - 1) for k in range(levels)]
    expb = expb_ref[...]
    expg = expg_ref[...]

    def per_head(x, e):
        hi = x.astype(BF16)
        lo = (x - hi.astype(F32)).astype(BF16)
        return _dot(hi, e) + _dot(lo, e)

    def inv_norm(x):
        ss = _dot_nt((x * x).astype(BF16), expb)
        return per_head(lax.rsqrt(ss + RMS_EPS), expb)

    qn_all = qc * (inv_norm(qc) * (DN_HD ** -0.5))
    kn_all = kc * inv_norm(kc)
    beta_all = per_head(beta, expb)
    eG_all = per_head(jnp.exp(G), expg)
    G_last_rows = jnp.concatenate(
        [jnp.broadcast_to(G[(c + 1) * C - 1:(c + 1) * C, :], (C, BA_LANES)) for c in range(DN_STEP_CHUNKS)], axis=0)
    kb_all = kn_all * beta_all
    vb_all = vc * beta_all
    kbe_all = kb_all * eG_all
    qn16 = qn_all.astype(BF16)
    kn16 = kn_all.astype(BF16)
    kb16 = kb_all.astype(BF16)
    qe16 = (qn_all * eG_all).astype(BF16)
    kdec16 = (kn_all * per_head(jnp.exp(G_last_rows - G), expg)).astype(BF16)

    heads = range(DN_HEADS)
    units = [(c, h) for c in range(DN_STEP_CHUNKS) for h in heads]
    rs = [slice(c * C, (c + 1) * C) for c in range(DN_STEP_CHUNKS)]
    sls = [slice(h * DN_HD, (h + 1) * DN_HD) for h in heads]
    G_c, A, aqk, Z = {}, {}, {}, {}
    for u in units:
        c, h = u
        G_c[u] = G[rs[c], DN_HEADS + h:DN_HEADS + h + 1]
        gram = _dot_nt(jnp.concatenate([kb16[rs[c], sls[h]], qn16[rs[c], sls[h]]], axis=0), kn16[rs[c], sls[h]])
        G_r = GT[DN_HEADS + h:DN_HEADS + h + 1, rs[c]]
        decay = jnp.exp(jnp.where(tril, G_c[u] - G_r, -jnp.inf))
        A[u] = (gram[:C] * decay).astype(BF16)
        aqk[u] = (gram[C:] * decay).astype(BF16)
    for u in units:
        c, h = u
        X = jnp.concatenate([vb_all[rs[c], sls[h]], kbe_all[rs[c], sls[h]]], axis=1)
        L1 = jnp.where(level_masks[0], A[u], 0.0)
        Z[u] = jnp.concatenate([eye - L1.astype(F32), X - _dot(L1, X.astype(BF16))], axis=1)
    for lvl in range(1, levels):
        Z16 = {u: Z[u].astype(BF16) for u in units}
        cols = slice(0, C + 2 * DN_HD) if lvl < levels - 1 else slice(C, C + 2 * DN_HD)
        Y = {u: _dot(jnp.where(level_masks[lvl], A[u], 0.0), Z16[u][:, cols]).astype(BF16) for u in units}
        Z = {u: Z[u][:, cols] - _dot(Z16[u][:, :C], Y[u]) for u in units}
    S = [state_ref[h] for h in heads]
    for c in range(DN_STEP_CHUNKS):
        us = [(c, h) for h in heads]
        ws = [_dot(jnp.concatenate([Z[u][:, DN_HD:].astype(BF16), qe16[rs[c], sls[u[1]]]], axis=0),
                   S[u[1]].astype(BF16)) for u in us]
        v16 = [(Z[u][:, :DN_HD] - ws[u[1]][:C]).astype(BF16) for u in us]
        for u in us:
            h = u[1]
            G_last = G_c[u][C - 1:C, :]
            S[h] = S[h] * jnp.exp(G_last) + _dot_tn(kdec16[rs[c], sls[h]], v16[h])
        for u in us:
            h = u[1]
            oraw_ref[rs[c], sls[h]] = ws[h][C:] + _dot(aqk[u], v16[h])
    for h in heads:
        state_ref[h] = S[h]
    o = oraw_ref[...]
    ms = _dot_nt((o * o).astype(BF16), expb) * (1.0 / DN_HD)
    o = o * per_head(lax.rsqrt(ms + RMS_EPS), expb) * ng_ref[...]
    o_ref[...] = (o * _silu(z_ref[...].astype(F32))).astype(BF16)


def _deltanet(proj, ba, conv_w, alog_row, dtb_row, norm_g, batch, seq):
    n = proj.shape[0]
    C = DN_STEP_CHUNKS * DN_CHUNK
    nt = seq // C
    hb = C // SEQ_HALO

    def head_lanes(first_row):
        m = np.zeros((BA_LANES, DN_W), np.float32)
        for h in range(DN_HEADS):
            m[first_row + h, h * DN_HD:(h + 1) * DN_HD] = 1.0
        return m

    def cur(col):
        return pl.BlockSpec((C, DN_W), lambda b, i: (b * nt + i, col))

    def halo(col):
        return pl.BlockSpec((SEQ_HALO, DN_W), lambda b, i: (jnp.maximum((b * nt + i) * hb - 1, 0), col))

    return pl.pallas_call(
        _deltanet_kernel,
        grid=(batch, nt),
        in_specs=[cur(COL_DQ), cur(COL_DQ + 1), cur(COL_DQ + 2),
                  halo(COL_DQ), halo(COL_DQ + 1), halo(COL_DQ + 2),
                  cur(COL_DZ),
                  pl.BlockSpec((C, BA_LANES), lambda b, i: (b * nt + i, 0)),
                  pl.BlockSpec((3, DN_CONV, DN_W), lambda b, i: (0, 0, 0)),
                  pl.BlockSpec((1, BA_LANES), lambda b, i: (0, 0)),
                  pl.BlockSpec((1, BA_LANES), lambda b, i: (0, 0)),
                  pl.BlockSpec((1, DN_W), lambda b, i: (0, 0)),
                  pl.BlockSpec((BA_LANES, DN_W), lambda b, i: (0, 0)),
                  pl.BlockSpec((BA_LANES, DN_W), lambda b, i: (0, 0))],
        out_specs=pl.BlockSpec((C, DN_W), lambda b, i: (b * nt + i, 0)),
        out_shape=jax.ShapeDtypeStruct((n, DN_W), BF16),
        scratch_shapes=[pltpu.VMEM((DN_HEADS, DN_HD, DN_HD), F32), pltpu.VMEM((C, DN_W), F32)],
        compiler_params=_cparams("parallel", "arbitrary"),
        name="deltanet",
    )(proj, proj, proj, proj, proj, proj, proj, ba, conv_w, alog_row, dtb_row, norm_g,
      jnp.asarray(head_lanes(0), BF16), jnp.asarray(head_lanes(DN_HEADS), BF16))


def _merge_kernel(x_ref, yp_ref, ym_ref, yd_ref, g0_ref, g1_ref, g2_ref, wp_ref, wm_ref, wd_ref, wo_ref, o_ref):
    merged = (g0_ref[...].astype(F32) * _dot(yp_ref[...], wp_ref[...])
              + g1_ref[...].astype(F32) * _dot(ym_ref[...], wm_ref[...])
              + g2_ref[...].astype(F32) * _dot(yd_ref[...], wd_ref[...]))
    o_ref[...] = x_ref[...] + _dot(merged.astype(BF16), wo_ref[...])


def _merge(x, y_pool, y_moba, y_dn, proj, w_up_pool, w_up_moba, w_up_dn, w_out, tm):
    n = x.shape[0]

    def rows(width, col=0):
        return pl.BlockSpec((tm, width), lambda i: (i, col))

    def whole(shape):
        return pl.BlockSpec(shape, lambda i: (0, 0))

    return pl.pallas_call(
        _merge_kernel,
        grid=(n // tm,),
        in_specs=[rows(D_MODEL), rows(POOL_W), rows(MOBA_AUG_W), rows(DN_W),
                  rows(D_MODEL, COL_GATE0), rows(D_MODEL, COL_GATE0 + 1), rows(D_MODEL, COL_GATE0 + 2),
                  whole((POOL_W, D_MODEL)), whole((MOBA_AUG_W, D_MODEL)), whole((DN_W, D_MODEL)),
                  whole((D_MODEL, D_MODEL))],
        out_specs=rows(D_MODEL),
        out_shape=jax.ShapeDtypeStruct((n, D_MODEL), F32),
        compiler_params=_cparams("parallel"),
        name="merge",
    )(x, y_pool, y_moba, y_dn, proj, proj, proj, w_up_pool, w_up_moba, w_up_dn, w_out)


def _memkv_kernel(mem_ref, g_ref, w_ref, o_ref):
    o_ref[...] = _dot(_rms(mem_ref[...], g_ref[...]).astype(BF16), w_ref[...]).astype(BF16)


def _memkv(mem2d, g, wkv):
    m = mem2d.shape[0]
    tm = 256
    return pl.pallas_call(
        _memkv_kernel,
        grid=(m // tm,),
        in_specs=[pl.BlockSpec((tm, D_MODEL), lambda i: (i, 0)),
                  pl.BlockSpec((1, D_MODEL), lambda i: (0, 0)),
                  pl.BlockSpec((D_MODEL, 2 * XA_W), lambda i: (0, 0))],
        out_specs=pl.BlockSpec((tm, 2 * XA_W), lambda i: (i, 0)),
        out_shape=jax.ShapeDtypeStruct((m, 2 * XA_W), BF16),
        compiler_params=_cparams("parallel"),
        name="memkv",
    )(mem2d, g, wkv)


def _xattn_kernel(x_ref, g_ref, wq_ref, kv_ref, wo_ref, o_ref):
    x = x_ref[...]
    q = _dot(_rms(x, g_ref[...]).astype(BF16), wq_ref[...]).astype(BF16)
    scale = XA_HD ** -0.5
    outs = []
    for h in range(XA_HEADS):
        sl = slice(h * XA_HD, (h + 1) * XA_HD)
        k = kv_ref[:, h * XA_HD:(h + 1) * XA_HD]
        v = kv_ref[:, XA_W + h * XA_HD:XA_W + (h + 1) * XA_HD]
        s = _dot_nt(q[:, sl], k) * scale
        s = s - jnp.max(s, axis=1, keepdims=True)
        p = jnp.exp(s)
        p = p / jnp.sum(p, axis=1, keepdims=True)
        outs.append(_dot(p.astype(BF16), v).astype(BF16))
    o = jnp.concatenate(outs, axis=1)
    o_ref[...] = x + _dot(o, wo_ref[...])


def _xattn(x, g, wq, kv, wo, seq, mem_len, tm):
    n = x.shape[0]
    tiles_per_seq = seq // tm
    return pl.pallas_call(
        _xattn_kernel,
        grid=(n // tm,),
        in_specs=[pl.BlockSpec((tm, D_MODEL), lambda i: (i, 0)),
                  pl.BlockSpec((1, D_MODEL), lambda i: (0, 0)),
                  pl.BlockSpec((D_MODEL, XA_W), lambda i: (0, 0)),
                  pl.BlockSpec((mem_len, 2 * XA_W), lambda i: (i // tiles_per_seq, 0)),
                  pl.BlockSpec((XA_W, D_MODEL), lambda i: (0, 0))],
        out_specs=pl.BlockSpec((tm, D_MODEL), lambda i: (i, 0)),
        out_shape=jax.ShapeDtypeStruct((n, D_MODEL), F32),
        compiler_params=_cparams("parallel"),
        name="xattn",
    )(x, g, wq, kv, wo)


def _swiglu_chunk(h, wg_ref, wu_ref, wd_ref):
    a = _silu(_dot(h, wg_ref[...])) * _dot(h, wu_ref[...])
    return _dot(a.astype(BF16), wd_ref[...])


def _ffn_kernel(x_ref, g_ref, wg_ref, wu_ref, wd_ref, o_ref, h_ref, acc_ref):
    j = pl.program_id(1)

    @pl.when(j == 0)
    def _():
        h_ref[...] = _rms(x_ref[...], g_ref[...]).astype(BF16)
        acc_ref[...] = x_ref[...]

    acc_ref[...] += _swiglu_chunk(h_ref[...], wg_ref, wu_ref, wd_ref)

    @pl.when(j == pl.num_programs(1) - 1)
    def _():
        o_ref[...] = acc_ref[...]


def _ffn(x, g, w_gate_up, w_down, tm, tf):
    n = x.shape[0]
    ff = w_down.shape[0]
    nf = ff // tf
    return pl.pallas_call(
        _ffn_kernel,
        grid=(n // tm, nf),
        in_specs=[pl.BlockSpec((tm, D_MODEL), lambda i, j: (i, 0)),
                  pl.BlockSpec((1, D_MODEL), lambda i, j: (0, 0)),
                  pl.BlockSpec((D_MODEL, tf), lambda i, j: (0, j)),
                  pl.BlockSpec((D_MODEL, tf), lambda i, j: (0, nf + j)),
                  pl.BlockSpec((tf, D_MODEL), lambda i, j: (j, 0))],
        out_specs=pl.BlockSpec((tm, D_MODEL), lambda i, j: (i, 0)),
        out_shape=jax.ShapeDtypeStruct((n, D_MODEL), F32),
        scratch_shapes=[pltpu.VMEM((tm, D_MODEL), BF16), pltpu.VMEM((tm, D_MODEL), F32)],
        compiler_params=_cparams("parallel", "arbitrary"),
        name="ffn",
    )(x, g, w_gate_up, w_gate_up, w_down)


def _route_kernel(x_ref, g_ref, r_ref, o_ref, *, tm):
    lane = lax.broadcasted_iota(jnp.int32, (tm, 128), 1).astype(F32)
    h = _rms(x_ref[...], g_ref[...]).astype(BF16)
    logits = jnp.where(lane < float(N_EXPERTS), _dot(h, r_ref[...]), -jnp.inf)
    m1 = jnp.max(logits, axis=1, keepdims=True)
    i1 = jnp.min(jnp.where(logits == m1, lane, 128.0), axis=1, keepdims=True)
    rest = jnp.where(lane == i1, -jnp.inf, logits)
    m2 = jnp.max(rest, axis=1, keepdims=True)
    i2 = jnp.min(jnp.where(rest == m2, lane, 128.0), axis=1, keepdims=True)
    e2 = jnp.exp(m2 - m1)
    o_ref[...] = (jnp.where(lane == 0.0, i1, 0.0) + jnp.where(lane == 1.0, i2, 0.0)
                  + jnp.where(lane == 2.0, 1.0 / (1.0 + e2), 0.0) + jnp.where(lane == 3.0, e2 / (1.0 + e2), 0.0))


def _route(x, g, router, tm):
    n = x.shape[0]
    return pl.pallas_call(
        functools.partial(_route_kernel, tm=tm),
        grid=(n // tm,),
        in_specs=[pl.BlockSpec((tm, D_MODEL), lambda i: (i, 0)),
                  pl.BlockSpec((1, D_MODEL), lambda i: (0, 0)),
                  pl.BlockSpec((D_MODEL, 128), lambda i: (0, 0))],
        out_specs=pl.BlockSpec((tm, 128), lambda i: (i, 0)),
        out_shape=jax.ShapeDtypeStruct((n, 128), F32),
        compiler_params=_cparams("parallel"),
        name="moe_route",
    )(x, g, router)


def _row_gather(idx_smem, src_hbm, dst, count, sem):
    def row_copy(r):
        s0 = pl.multiple_of(idx_smem[r] * ROW_TILE, ROW_TILE)
        d0 = pl.multiple_of(r * ROW_TILE, ROW_TILE)
        return pltpu.make_async_copy(src_hbm.at[pl.ds(s0, ROW_TILE), :], dst.at[pl.ds(d0, ROW_TILE), :], sem)

    def start():
        @pl.loop(0, count, step=GATHER_UNROLL)
        def _(r0):
            for u in range(GATHER_UNROLL):
                row_copy(r0 + u).start()

    def wait():
        @pl.loop(0, count, step=GATHER_UNROLL)
        def _(r0):
            for u in range(GATHER_UNROLL):
                row_copy(r0 + u).wait()

    return start, wait


def _gather_ahead(step, n_steps, idx_now, idx_next, idx_smem, src_hbm, buf, count, sems):
    slot = step % 2

    def begin(idx_vmem, sl):
        stage = pltpu.make_async_copy(idx_vmem, idx_smem[sl], sems.at[2])
        stage.start()
        stage.wait()
        _row_gather(idx_smem[sl], src_hbm, buf.at[sl], count, sems.at[sl])[0]()

    @pl.when(step == 0)
    def _():
        begin(idx_now, 0)

    for sl in range(2):
        @pl.when(slot == sl)
        def _(sl=sl):
            @pl.when(step + 1 < n_steps)
            def _():
                begin(idx_next, 1 - sl)

            _row_gather(idx_smem[sl], src_hbm, buf.at[sl], count, sems.at[sl])[1]()

    return slot


def _rms_row_tiles(v, g8, rows):
    v3 = v.reshape(rows, ROW_TILE, 128)
    ms = jnp.sum(jnp.sum(v3 * v3, axis=2, keepdims=True), axis=1, keepdims=True) * (1.0 / D_MODEL)
    return (v3 * lax.rsqrt(ms + RMS_EPS) * g8[None]).reshape(rows * ROW_TILE, 128)


def _expert_kernel(te_ref, nu_ref, x_hbm, src_ref, src_next_ref, w_ref, g8_ref, wg_ref, wu_ref, wd_ref, o_ref,
                   idx0_ref, idx1_ref, xbuf_ref, x16_ref, sems, *, tm):
    t = pl.program_id(0)
    del te_ref

    @pl.when(t >= nu_ref[0])
    def _():
        o_ref[...] = jnp.zeros_like(o_ref)

    @pl.when(t < nu_ref[0])
    def _():
        slot = _gather_ahead(t, nu_ref[0], src_ref.at[0, 0], src_next_ref.at[0, 0], (idx0_ref, idx1_ref), x_hbm, xbuf_ref,
                             tm, sems)
        xb = xbuf_ref.at[slot]
        xb[...] = _rms_row_tiles(xb[...], g8_ref[...], tm)
        for s in range(ROW_TILE):
            x16_ref[:, s * 128:(s + 1) * 128] = xb[pl.ds(s, tm, stride=ROW_TILE), :].astype(BF16)
        y = _swiglu_chunk(x16_ref[...], wg_ref.at[0], wu_ref.at[0], wd_ref.at[0]) * w_ref[...]
        for s in range(ROW_TILE):
            o_ref[pl.ds(s, tm, stride=ROW_TILE), :] = y[:, s * 128:(s + 1) * 128]


def _experts(x_rows, src, w_sorted, tile_expert, n_used, g8, w_gate_up, w_down, tm):
    nt = src.shape[0]
    ff = w_down.shape[1]
    grid_spec = pltpu.PrefetchScalarGridSpec(
        num_scalar_prefetch=2,
        grid=(nt,),
        in_specs=[pl.BlockSpec(memory_space=pl.ANY),
                  pl.BlockSpec((1, 1, tm), lambda t, te, nu: (t, 0, 0)),
                  pl.BlockSpec((1, 1, tm), lambda t, te, nu: (jnp.minimum(t + 1, nt - 1), 0, 0)),
                  pl.BlockSpec((tm, 1), lambda t, te, nu: (t, 0)),
                  pl.BlockSpec((ROW_TILE, 128), lambda t, te, nu: (0, 0)),
                  pl.BlockSpec((1, D_MODEL, ff), lambda t, te, nu: (te[t], 0, 0)),
                  pl.BlockSpec((1, D_MODEL, ff), lambda t, te, nu: (te[t], 0, 1)),
                  pl.BlockSpec((1, ff, D_MODEL), lambda t, te, nu: (te[t], 0, 0))],
        out_specs=pl.BlockSpec((tm * ROW_TILE, 128), lambda t, te, nu: (t, 0)),
        scratch_shapes=[pltpu.SMEM((tm,), jnp.int32), pltpu.SMEM((tm,), jnp.int32),
                        pltpu.VMEM((2, tm * ROW_TILE, 128), F32),
                        pltpu.VMEM((tm, D_MODEL), BF16), pltpu.SemaphoreType.DMA((3,))])
    return pl.pallas_call(
        functools.partial(_expert_kernel, tm=tm),
        grid_spec=grid_spec,
        out_shape=jax.ShapeDtypeStruct((nt * tm * ROW_TILE, 128), F32),
        compiler_params=_cparams("arbitrary"),
        name="moe_experts",
    )(tile_expert, n_used, x_rows, src, src, w_sorted, g8, w_gate_up, w_gate_up, w_down)


def _combine_kernel(x_ref, y_hbm, dest_ref, dest_next_ref, fg_ref, o_ref, idx0_ref, idx1_ref, ybuf_ref, sems, *, tt):
    slot = _gather_ahead(pl.program_id(0), pl.num_programs(0), dest_ref.at[0, 0], dest_next_ref.at[0, 0],
                         (idx0_ref, idx1_ref),
                         y_hbm, ybuf_ref, MOE_TOPK * tt, sems)
    yb = ybuf_ref.at[slot]
    n8 = tt * ROW_TILE
    for s in range(ROW_TILE):
        sl = slice(s * 128, (s + 1) * 128)
        o_ref[:, sl] = (x_ref[:, sl] + yb[pl.ds(s, tt, stride=ROW_TILE), :]
                        + yb[pl.ds(n8 + s, tt, stride=ROW_TILE), :])
    o_ref[...] = _rms(o_ref[...], fg_ref[...])


def _combine_final(x, y_rows, dest, fg, tt):
    nt = dest.shape[0]
    return pl.pallas_call(
        functools.partial(_combine_kernel, tt=tt),
        grid=(nt,),
        in_specs=[pl.BlockSpec((tt, D_MODEL), lambda i: (i, 0)),
                  pl.BlockSpec(memory_space=pl.ANY),
                  pl.BlockSpec((1, 1, MOE_TOPK * tt), lambda i: (i, 0, 0)),
                  pl.BlockSpec((1, 1, MOE_TOPK * tt), lambda i: (jnp.minimum(i + 1, nt - 1), 0, 0)),
                  pl.BlockSpec((1, D_MODEL), lambda i: (0, 0))],
        out_specs=pl.BlockSpec((tt, D_MODEL), lambda i: (i, 0)),
        out_shape=jax.ShapeDtypeStruct(x.shape, F32),
        scratch_shapes=[pltpu.SMEM((MOE_TOPK * tt,), jnp.int32), pltpu.SMEM((MOE_TOPK * tt,), jnp.int32),
                        pltpu.VMEM((2, MOE_TOPK * tt * ROW_TILE, 128), F32), pltpu.SemaphoreType.DMA((3,))],
        compiler_params=_cparams("arbitrary"),
        name="moe_combine",
    )(x, y_rows, dest, dest, fg)


def _moe_sparse_final(xf, g, router, w_gate_up, w_down, final_g, tm, tt):
    n = xf.shape[0]
    nk = MOE_TOPK * n
    route = _route(xf, g, router, tm)
    e_flat = route[:, :MOE_TOPK].astype(jnp.int32).reshape(nk)
    w_flat = route[:, MOE_TOPK:2 * MOE_TOPK].reshape(nk)
    onehot = (e_flat[:, None] == jnp.arange(N_EXPERTS, dtype=jnp.int32)[None, :]).astype(jnp.int32)
    csum = jnp.cumsum(onehot, axis=0)
    cnt = csum[-1]
    rank = jnp.take_along_axis(csum, e_flat[:, None], axis=1)[:, 0] - 1
    padded = ((cnt + tm - 1) // tm) * tm
    ends = jnp.cumsum(padded)
    off = ends - padded
    start = jnp.cumsum(cnt) - cnt
    dest = off[e_flat] + rank
    nt = nk // tm + N_EXPERTS
    slot = jnp.arange(nt * tm, dtype=jnp.int32)
    e_slot = jnp.minimum(jnp.searchsorted(ends, slot, side="right"), N_EXPERTS - 1).astype(jnp.int32)
    loc = slot - off[e_slot]
    valid = loc < cnt[e_slot]
    order = jnp.argsort(e_flat, stable=True).astype(jnp.int32)
    pair = order[jnp.clip(start[e_slot] + loc, 0, nk - 1)]
    src = jnp.where(valid, pair // MOE_TOPK, 0).reshape(nt, 1, tm)
    w_sorted = jnp.where(valid, w_flat[pair], 0.0).reshape(nt * tm, 1)
    tile_expert = e_slot[::tm]
    n_used = (ends[-1:] // tm).astype(jnp.int32)

    x_rows = xf.reshape(n * ROW_TILE, 128)
    y_rows = _experts(x_rows, src, w_sorted, tile_expert, n_used, g.reshape(ROW_TILE, 128), w_gate_up, w_down, tm)
    dest = dest.reshape(n // tt, tt, MOE_TOPK).transpose(0, 2, 1).reshape(n // tt, 1, MOE_TOPK * tt)
    return _combine_final(xf, y_rows, dest, final_g[None, :], tt)


def _blockdiag(pool_w):
    g, c, _ = pool_w.shape
    out = jnp.zeros((g * c, g * c), pool_w.dtype)
    for gi in range(g):
        out = out.at[gi * c:(gi + 1) * c, gi * c:(gi + 1) * c].set(pool_w[gi])
    return out


def _lane_row(v, offset):
    return jnp.zeros((1, BA_LANES), F32).at[0, offset:offset + v.shape[0]].set(v.astype(F32))


def kernel(x, mem, positions, mix_norm_g, w_in, pool_w, pool_scale, dn_conv_w, dn_a_log, dn_dt_bias, dn_norm_g,
           w_up_pool, w_up_moba, w_up_dn, w_out, xa_norm_g, mem_norm_g, xa_wq, xa_wkv, xa_wo, ffn_norm_g,
           dense_w_gate_up, dense_w_down, moe_router, moe_w_gate_up, moe_w_down, final_norm_g):
    batch, seq, d = x.shape
    depth = w_in.shape[0]
    mem_len = mem.shape[1]
    n = batch * seq
    nb = seq // MOBA_BLOCK
    assert d == D_MODEL and seq % MOBA_BLOCK == 0 and nb <= MOBA_HD and nb % MOBA_GROUP == 0 and depth == 2
    tm = min(512, seq)

    xf = x.reshape(n, d)
    mem2d = mem.reshape(batch * mem_len, d)
    cos_t, sin_t = _rope_tables(positions.reshape(n, 1).astype(F32), tm)
    off_ba = POOL_W + 3 * MOBA_W + 3 * DN_W

    for layer in range(depth):
        wl = w_in[layer]
        w_main = jnp.concatenate([wl[:, :off_ba], wl[:, off_ba + 2 * DN_HEADS:]], axis=1).astype(BF16)
        w_ba = jnp.pad(wl[:, off_ba:off_ba + 2 * DN_HEADS], ((0, 0), (0, BA_LANES - 2 * DN_HEADS))).astype(BF16)
        proj, ba = _inproj(xf, mix_norm_g[layer][None, :], w_main, w_ba, min(2048, seq))

        y_pool = _pool(proj, _blockdiag(pool_w[layer]).astype(BF16), pool_scale[layer][None, :], batch, seq, tm)
        mq, mk, mv, kbar = _moba_prep(proj, cos_t, sin_t, nb)
        kbar = jnp.pad(kbar.reshape(batch, nb, MOBA_W), ((0, 0), (0, MOBA_KBAR_ROWS - nb), (0, 0)))
        y_moba = _moba(mq, mk, mv, kbar, batch, seq)
        w_moba = jnp.pad(w_up_moba[layer].reshape(MOBA_HEADS, MOBA_HD, d), ((0, 0), (0, MOBA_HD), (0, 0)))
        w_moba = w_moba.reshape(MOBA_AUG_W, d).astype(BF16)
        conv_w = dn_conv_w[layer].reshape(DN_CONV, 3, DN_W).transpose(1, 0, 2)
        y_dn = _deltanet(proj, ba, conv_w, _lane_row(dn_a_log[layer], DN_HEADS),
                         _lane_row(dn_dt_bias[layer], DN_HEADS), jnp.tile(dn_norm_g[layer], DN_HEADS)[None, :], batch, seq)
        xf = _merge(xf, y_pool, y_moba, y_dn, proj, w_up_pool[layer].astype(BF16), w_moba,
                    w_up_dn[layer].astype(BF16), w_out[layer].astype(BF16), tm)

        kv = _memkv(mem2d, mem_norm_g[layer][None, :], xa_wkv[layer].astype(BF16))
        xf = _xattn(xf, xa_norm_g[layer][None, :], xa_wq[layer].astype(BF16), kv, xa_wo[layer].astype(BF16),
                    seq, mem_len, tm)

        if layer % 2 == 0:
            xf = _ffn(xf, ffn_norm_g[layer][None, :], dense_w_gate_up[layer // 2].astype(BF16),
                      dense_w_down[layer // 2].astype(BF16), tm, dense_w_down.shape[1] // 2)
        else:
            router = jnp.pad(moe_router[layer // 2], ((0, 0), (0, 128 - N_EXPERTS))).astype(BF16)
            xf = _moe_sparse_final(xf, ffn_norm_g[layer][None, :], router, moe_w_gate_up[layer // 2].astype(BF16),
                                   moe_w_down[layer // 2].astype(BF16), final_norm_g, tm, min(256, seq))
    return xf.reshape(batch, seq, d)
```

```python
import functools

import numpy as np
import jax
import jax.numpy as jnp
from jax import lax
from jax.experimental import pallas as pl
from jax.experimental.pallas import tpu as pltpu

F32 = jnp.float32
BF16 = jnp.bfloat16

RMS_EPS = 1e-6
D_MODEL = 1024
POOL_W = 256
POOL_GROUP_W = 64
POOL_WINDOWS = (2, 4, 8, 16)
SEQ_HALO = 16
MOBA_HEADS = 4
MOBA_HD = 64
MOBA_W = 256
MOBA_AUG_W = 2 * MOBA_W
MOBA_BLOCK = 256
MOBA_TOPK = 3
MOBA_KBAR_ROWS = 128
MOBA_GROUP = 2
ROPE_THETA = 500000.0
ROPE_DIMS = 16
DN_HEADS = 8
DN_HD = 64
DN_W = 512
DN_CONV = 4
DN_CHUNK = 128
DN_STEP_CHUNKS = 2
XA_HEADS = 4
XA_HD = 128
XA_W = 512
N_EXPERTS = 8
MOE_TOPK = 2
ROW_TILE = 8
GATHER_UNROLL = 16
COL_POOL, COL_MQ, COL_MK, COL_MV = 0, 1, 2, 3
COL_DQ, COL_DZ = 2, 5
COL_GATE0 = 3
PROJ_COLS = 6144
PROJ_TN = 1024
PROJ_ID_BLOCKS = 3
BA_LANES = 128
VMEM_LIMIT = 56 * 1024 * 1024
NEG_BIG = -1e30
LOG2_E = 1.4426950408889634


def _cparams(*sem):
    return pltpu.CompilerParams(dimension_semantics=sem, vmem_limit_bytes=VMEM_LIMIT)


def _rms(x, g):
    ms = jnp.mean(x * x, axis=-1, keepdims=True)
    return x * lax.rsqrt(ms + RMS_EPS) * g


def _silu(x):
    return x * jax.nn.sigmoid(x)


def _dot(a, b):
    return jnp.dot(a, b, preferred_element_type=F32)


def _dot_nt(a, b):
    return lax.dot_general(a, b, (((1,), (1,)), ((), ())), preferred_element_type=F32)


def _dot_tn(a, b):
    return lax.dot_general(a, b, (((0,), (0,)), ((), ())), preferred_element_type=F32)


def _inproj_kernel(x_ref, g_ref, w_ref, wba_ref, proj_ref, ba_ref, h_ref):
    j = pl.program_id(1)

    @pl.when(j == 0)
    def _():
        h = _rms(x_ref[...], g_ref[...]).astype(BF16)
        h_ref[...] = h
        ba_ref[...] = _dot(h, wba_ref[...])

    acc = _dot(h_ref[...], w_ref[...])

    @pl.when(j < PROJ_ID_BLOCKS)
    def _():
        proj_ref[...] = acc.astype(BF16)

    @pl.when(j >= PROJ_ID_BLOCKS)
    def _():
        proj_ref[...] = jax.nn.sigmoid(acc).astype(BF16)


def _inproj(x, g, w_main, w_ba, tm):
    n = x.shape[0]
    return pl.pallas_call(
        _inproj_kernel,
        grid=(n // tm, PROJ_COLS // PROJ_TN),
        in_specs=[
            pl.BlockSpec((tm, D_MODEL), lambda i, j: (i, 0)),
            pl.BlockSpec((1, D_MODEL), lambda i, j: (0, 0)),
            pl.BlockSpec((D_MODEL, PROJ_TN), lambda i, j: (0, j)),
            pl.BlockSpec((D_MODEL, BA_LANES), lambda i, j: (0, 0)),
        ],
        out_specs=[
            pl.BlockSpec((tm, PROJ_TN), lambda i, j: (i, j)),
            pl.BlockSpec((tm, BA_LANES), lambda i, j: (i, 0)),
        ],
        out_shape=[jax.ShapeDtypeStruct((n, PROJ_COLS), BF16),
                   jax.ShapeDtypeStruct((n, BA_LANES), F32)],
        scratch_shapes=[pltpu.VMEM((tm, D_MODEL), BF16)],
        compiler_params=_cparams("parallel", "arbitrary"),
        name="inproj",
    )(x, g, w_main, w_ba)


def _rope_table_kernel(pos_ref, freq_ref, sign_ref, c_ref, s_ref):
    ang = pos_ref[...] * freq_ref[...]
    rot = sign_ref[...]
    c_ref[...] = jnp.where(rot != 0.0, jnp.cos(ang), 1.0)
    s_ref[...] = jnp.sin(ang) * rot


def _rope_tables(pos_f32, tm):
    n = pos_f32.shape[0]
    half = ROPE_DIMS // 2
    inv_freq = np.power(ROPE_THETA, -np.arange(half, dtype=np.float32) * 2.0 / ROPE_DIMS).astype(np.float32)
    d = np.arange(128) % MOBA_HD
    freq = np.where(d < ROPE_DIMS, inv_freq[d % half], 0.0).astype(np.float32)[None, :]
    sign = np.where(d < half, -1.0, np.where(d < ROPE_DIMS, 1.0, 0.0)).astype(np.float32)[None, :]
    return pl.pallas_call(
        _rope_table_kernel,
        grid=(n // tm,),
        in_specs=[pl.BlockSpec((tm, 1), lambda i: (i, 0)),
                  pl.BlockSpec((1, 128), lambda i: (0, 0)),
                  pl.BlockSpec((1, 128), lambda i: (0, 0))],
        out_specs=[pl.BlockSpec((tm, 128), lambda i: (i, 0)),
                   pl.BlockSpec((tm, 128), lambda i: (i, 0))],
        out_shape=[jax.ShapeDtypeStruct((n, 128), F32), jax.ShapeDtypeStruct((n, 128), F32)],
        compiler_params=_cparams("parallel"),
        name="rope_tables",
    )(pos_f32, jnp.asarray(freq), jnp.asarray(sign))


def _moba_prep_kernel(q_ref, k_ref, v_ref, c_ref, s_ref, qo_ref, ko_ref, vo_ref, kbar_ref, *, nb):
    c = jnp.concatenate([c_ref[...], c_ref[...]], axis=1)
    s = jnp.concatenate([s_ref[...], s_ref[...]], axis=1)
    lane = lax.broadcasted_iota(jnp.int32, (MOBA_BLOCK, MOBA_W), 1)
    first_half = (lane % MOBA_HD) < (ROPE_DIMS // 2)

    def rope(x):
        partner = jnp.where(first_half,
                            pltpu.roll(x, MOBA_W - ROPE_DIMS // 2, 1),
                            pltpu.roll(x, ROPE_DIMS // 2, 1))
        return x * c + partner * s

    qo_ref[...] = (rope(q_ref[...].astype(F32)) * (MOBA_HD ** -0.5 * LOG2_E)).astype(BF16)
    k = rope(k_ref[...].astype(F32))
    kbar_ref[0] = jnp.mean(k, axis=0, keepdims=True)
    k16 = k.astype(BF16)
    blk_lane = lax.broadcasted_iota(jnp.int32, (MOBA_BLOCK, MOBA_HD), 1)
    onehot = jnp.where(blk_lane == pl.program_id(0) % nb, 1.0, 0.0).astype(BF16)
    k_parts = []
    for h in range(MOBA_HEADS):
        k_parts += [k16[:, h * MOBA_HD:(h + 1) * MOBA_HD], onehot]
    ko_ref[...] = jnp.concatenate(k_parts, axis=1)
    vo_ref[...] = v_ref[...].astype(F32).T.astype(BF16)


def _moba_prep(proj, cos_t, sin_t, nb):
    n = proj.shape[0]
    nblk = n // MOBA_BLOCK
    return pl.pallas_call(
        functools.partial(_moba_prep_kernel, nb=nb),
        grid=(nblk,),
        in_specs=[pl.BlockSpec((MOBA_BLOCK, MOBA_W), lambda r: (r, COL_MQ)),
                  pl.BlockSpec((MOBA_BLOCK, MOBA_W), lambda r: (r, COL_MK)),
                  pl.BlockSpec((MOBA_BLOCK, MOBA_W), lambda r: (r, COL_MV)),
                  pl.BlockSpec((MOBA_BLOCK, 128), lambda r: (r, 0)),
                  pl.BlockSpec((MOBA_BLOCK, 128), lambda r: (r, 0))],
        out_specs=[pl.BlockSpec((MOBA_BLOCK, MOBA_W), lambda r: (r, 0)),
                   pl.BlockSpec((MOBA_BLOCK, MOBA_AUG_W), lambda r: (r, 0)),
                   pl.BlockSpec((MOBA_W, MOBA_BLOCK), lambda r: (r, 0)),
                   pl.BlockSpec((1, 1, MOBA_W), lambda r: (r, 0, 0))],
        out_shape=[jax.ShapeDtypeStruct((n, MOBA_W), BF16),
                   jax.ShapeDtypeStruct((n, MOBA_AUG_W), BF16),
                   jax.ShapeDtypeStruct((nblk * MOBA_W, MOBA_BLOCK), BF16),
                   jax.ShapeDtypeStruct((nblk, 1, MOBA_W), F32)],
        compiler_params=_cparams("parallel"),
        name="moba_prep",
    )(proj, proj, proj, cos_t, sin_t)


def _moba_kernel(q_ref, k_ref, vt_ref, kbar_ref, o_ref, qa_ref, m_ref, ls_ref, acc_ref, s_ref):
    i = pl.program_id(1)
    heads = range(MOBA_HEADS)
    blk = lax.broadcasted_iota(jnp.int32, (MOBA_KBAR_ROWS, MOBA_BLOCK), 0).astype(F32)
    row = lax.broadcasted_iota(jnp.int32, (MOBA_BLOCK, MOBA_BLOCK), 0)
    col = lax.broadcasted_iota(jnp.int32, (MOBA_BLOCK, MOBA_BLOCK), 1)
    causal = row <= col
    i_f = i.astype(F32)
    own = pl.multiple_of(i * MOBA_BLOCK, MOBA_BLOCK)
    hs = [slice(h * 2 * MOBA_HD, (h + 1) * 2 * MOBA_HD) for h in heads]

    for h in heads:
        sl = slice(h * MOBA_HD, (h + 1) * MOBA_HD)
        qh = q_ref[:, sl]
        gate = _dot_nt(kbar_ref[0, :, sl].astype(BF16), qh)
        gate = jnp.where(blk < i_f, gate, -jnp.inf)
        keep = jnp.zeros(gate.shape, jnp.bool_)
        for _ in range(MOBA_TOPK):
            m = jnp.max(gate, axis=0, keepdims=True)
            idx = jnp.min(jnp.where(gate == m, blk, float(MOBA_KBAR_ROWS)), axis=0, keepdims=True)
            pick = (blk == idx) & (m > -jnp.inf)
            keep = keep | pick
            gate = jnp.where(pick, -jnp.inf, gate)
        bias = jnp.where(keep, 0.0, NEG_BIG).T[:, :MOBA_HD]
        qa_ref[h] = jnp.concatenate([qh, bias.astype(BF16)], axis=1)
        qa_ref[MOBA_HEADS + h] = jnp.concatenate([qh, jnp.zeros_like(qh)], axis=1)

    groups8 = MOBA_BLOCK // ROW_TILE

    def scores(h, start):
        return _dot_nt(k_ref[pl.ds(start, MOBA_BLOCK), hs[h]], qa_ref[h])

    def own_scores(h):
        s = _dot_nt(k_ref[pl.ds(own, MOBA_BLOCK), hs[h]], qa_ref[MOBA_HEADS + h])
        return jnp.where(causal, s, NEG_BIG)

    def rows8(x):
        return x.reshape(groups8, ROW_TILE, MOBA_BLOCK)

    def all8(x, op):
        return jnp.broadcast_to(op(x, axis=0, keepdims=True), x.shape)

    def values_t(h, start):
        return vt_ref[pl.ds(start + h * MOBA_HD, MOBA_HD), :]

    n_pairs = (i + (2 * MOBA_GROUP - 1)) // (2 * MOBA_GROUP)

    def group_blocks(g):
        return [pl.multiple_of((g * MOBA_GROUP + t) * MOBA_BLOCK, MOBA_BLOCK) for t in range(MOBA_GROUP)]

    for h in heads:
        s3 = rows8(own_scores(h))
        m = all8(jnp.max(s3, axis=0), jnp.max)
        p3 = jnp.exp2(s3 - m[None])
        ls_ref[h] = jnp.sum(p3, axis=0)
        acc_ref[h] = _dot(values_t(h, own), p3.reshape(MOBA_BLOCK, MOBA_BLOCK).astype(BF16))
        m_ref[h] = m

    def score_group(g, buf):
        gmax = [None] * MOBA_HEADS
        for t, start in enumerate(group_blocks(g)):
            for h in heads:
                s = scores(h, start)
                s_ref[buf, t, h] = s
                smax = jnp.max(rows8(s), axis=0)
                gmax[h] = smax if gmax[h] is None else jnp.maximum(gmax[h], smax)
        return gmax

    def rescale(gmax):
        for h in heads:
            m_old = m_ref[h]
            m_new = jnp.maximum(m_old, all8(gmax[h], jnp.max))
            alpha = jnp.exp2(m_old - m_new)
            ls_ref[h] = ls_ref[h] * alpha
            acc = acc_ref[h].reshape(MOBA_HD // ROW_TILE, ROW_TILE, MOBA_BLOCK) * alpha[None]
            acc_ref[h] = acc.reshape(MOBA_HD, MOBA_BLOCK)
            m_ref[h] = m_new

    def accumulate_group(g, buf):
        for t, start in enumerate(group_blocks(g)):
            for h in heads:
                p3 = jnp.exp2(rows8(s_ref[buf, t, h]) - m_ref[h][None])
                ls_ref[h] += jnp.sum(p3, axis=0)
                acc_ref[h] += _dot(values_t(h, start), p3.reshape(MOBA_BLOCK, MOBA_BLOCK).astype(BF16))

    @pl.when(n_pairs > 0)
    def _():
        rescale(score_group(0, 0))

    @pl.loop(0, n_pairs)
    def _(p):
        gmax = score_group(2 * p + 1, 1)
        accumulate_group(2 * p, 0)
        rescale(gmax)

        @pl.when(p + 1 < n_pairs)
        def _():
            gmax = score_group(2 * p + 2, 0)
            accumulate_group(2 * p + 1, 1)
            rescale(gmax)

        @pl.when(p + 1 == n_pairs)
        def _():
            accumulate_group(2 * p + 1, 1)

    for h in heads:
        o_t = acc_ref[h] / jnp.sum(ls_ref[h], axis=0, keepdims=True)
        o_ref[:, hs[h]] = jnp.concatenate([o_t, jnp.zeros_like(o_t)], axis=0).T.astype(BF16)


def _moba(q, k_aug, v_aug, kbar, batch, seq):
    n = q.shape[0]
    nb = seq // MOBA_BLOCK
    return pl.pallas_call(
        _moba_kernel,
        grid=(batch, nb),
        in_specs=[pl.BlockSpec((MOBA_BLOCK, MOBA_W), lambda b, i: (b * nb + i, 0)),
                  pl.BlockSpec((seq, MOBA_AUG_W), lambda b, i: (b, 0)),
                  pl.BlockSpec((seq, MOBA_W), lambda b, i: (b, 0)),
                  pl.BlockSpec((1, MOBA_KBAR_ROWS, MOBA_W), lambda b, i: (b, 0, 0))],
        out_specs=pl.BlockSpec((MOBA_BLOCK, MOBA_AUG_W), lambda b, i: (b * nb + i, 0)),
        out_shape=jax.ShapeDtypeStruct((n, MOBA_AUG_W), BF16),
        scratch_shapes=[pltpu.VMEM((2 * MOBA_HEADS, MOBA_BLOCK, 2 * MOBA_HD), BF16)]
        + [pltpu.VMEM((MOBA_HEADS, ROW_TILE, MOBA_BLOCK), F32)] * 2
        + [pltpu.VMEM((MOBA_HEADS, MOBA_HD, MOBA_BLOCK), F32)]
        + [pltpu.VMEM((2, MOBA_GROUP, MOBA_HEADS, MOBA_BLOCK, MOBA_BLOCK), F32)],
        compiler_params=_cparams("parallel", "arbitrary"),
        name="moba",
    )(q, k_aug, v_aug, kbar)


def _pool_kernel(p_ref, halo_ref, w_ref, scale_ref, o_ref, *, ts):
    i = pl.program_id(1)
    halo = jnp.where(i == 0, 0.0, halo_ref[...].astype(F32))
    p = p_ref[...].astype(F32)
    cur = jnp.concatenate([halo, p], axis=0)
    lane = lax.broadcasted_iota(jnp.int32, (ts, POOL_W), 1)
    t1 = (lax.broadcasted_iota(jnp.int32, (ts, POOL_W), 0) + i * ts + 1).astype(F32)
    total = jnp.zeros((ts, POOL_W), F32)
    count = jnp.ones((ts, POOL_W), F32)
    span = 1
    for gi, w in enumerate(POOL_WINDOWS):
        while span < w:
            cur = cur + pltpu.roll(cur, span, 0)
            span *= 2
        in_group = (lane >= gi * POOL_GROUP_W) & (lane < (gi + 1) * POOL_GROUP_W)
        total = jnp.where(in_group, cur[SEQ_HALO:], total)
        count = jnp.where(in_group, jnp.minimum(t1, float(w)), count)
    pooled = total / count - p
    o_ref[...] = (_dot(pooled.astype(BF16), w_ref[...]) * scale_ref[...]).astype(BF16)


def _pool(proj, w_blockdiag, scale, batch, seq, ts):
    n = proj.shape[0]
    nt = seq // ts
    hb = ts // SEQ_HALO
    return pl.pallas_call(
        functools.partial(_pool_kernel, ts=ts),
        grid=(batch, nt),
        in_specs=[pl.BlockSpec((ts, POOL_W), lambda b, i: (b * nt + i, COL_POOL)),
                  pl.BlockSpec((SEQ_HALO, POOL_W), lambda b, i: (jnp.maximum((b * nt + i) * hb - 1, 0), COL_POOL)),
                  pl.BlockSpec((POOL_W, POOL_W), lambda b, i: (0, 0)),
                  pl.BlockSpec((1, POOL_W), lambda b, i: (0, 0))],
        out_specs=pl.BlockSpec((ts, POOL_W), lambda b, i: (b * nt + i, 0)),
        out_shape=jax.ShapeDtypeStruct((n, POOL_W), BF16),
        compiler_params=_cparams("parallel", "parallel"),
        name="pool",
    )(proj, proj, w_blockdiag, scale)


def _deltanet_kernel(q_ref, k_ref, v_ref, qh_ref, kh_ref, vh_ref, z_ref, ba_ref, cw_ref, alog_ref, dtb_ref,
                     ng_ref, expb_ref, expg_ref, o_ref, state_ref, oraw_ref):
    i = pl.program_id(1)
    C = DN_CHUNK
    TS = DN_STEP_CHUNKS * C

    @pl.when(i == 0)
    def _():
        state_ref[...] = jnp.zeros_like(state_ref)

    def conv(x_ref, halo_ref, w):
        halo = jnp.where(i == 0, 0.0, halo_ref[...].astype(F32))
        ext = jnp.concatenate([halo, x_ref[...].astype(F32)], axis=0)
        y = ext * w[DN_CONV - 1:DN_CONV]
        for lag in range(1, DN_CONV):
            y = y + pltpu.roll(ext, lag, 0) * w[DN_CONV - 1 - lag:DN_CONV - lag]
        return _silu(y[SEQ_HALO:])

    cw = cw_ref[...]
    qc = conv(q_ref, qh_ref, cw[0])
    kc = conv(k_ref, kh_ref, cw[1])
    vc = conv(v_ref, vh_ref, cw[2])

    ba = ba_ref[...]
    beta = jax.nn.sigmoid(ba)
    g = -jnp.exp(alog_ref[...]) * jax.nn.softplus(ba + dtb_ref[...])
    rows = lax.broadcasted_iota(jnp.int32, (TS, BA_LANES), 0) & (C - 1)
    G = g
    span = 1
    while span < C:
        G = G + jnp.where(rows >= span, pltpu.roll(G, span, 0), 0.0)
        span *= 2
    GT = G.T
    r_i = lax.broadcasted_iota(jnp.int32, (C, C), 0)
    c_i = lax.broadcasted_iota(jnp.int32, (C, C), 1)
    tril = c_i <= r_i
    eye = (c_i == r_i).astype(F32)
    levels = C.bit_length() - 1
    level_masks = [((r_i >> k) & 1 == 1) & ((c_i >> k) == (r_i >> k) - 1) for k in range(levels)]
    expb = expb_ref[...]
    expg = expg_ref[...]

    def per_head(x, e):
        hi = x.astype(BF16)
        lo = (x - hi.astype(F32)).astype(BF16)
        return _dot(hi, e) + _dot(lo, e)

    def inv_norm(x):
        ss = _dot_nt((x * x).astype(BF16), expb)
        return per_head(lax.rsqrt(ss + RMS_EPS), expb)

    qn_all = qc * (inv_norm(qc) * (DN_HD ** -0.5))
    kn_all = kc * inv_norm(kc)
    beta_all = per_head(beta, expb)
    eG_all = per_head(jnp.exp(G), expg)
    G_last_rows = jnp.concatenate(
        [jnp.broadcast_to(G[(c + 1) * C - 1:(c + 1) * C, :], (C, BA_LANES)) for c in range(DN_STEP_CHUNKS)], axis=0)
    kb_all = kn_all * beta_all
    vb_all = vc * beta_all
    kbe_all = kb_all * eG_all
    qn16 = qn_all.astype(BF16)
    kn16 = kn_all.astype(BF16)
    kb16 = kb_all.astype(BF16)
    qe16 = (qn_all * eG_all).astype(BF16)
    kdec16 = (kn_all * per_head(jnp.exp(G_last_rows - G), expg)).astype(BF16)

    heads = range(DN_HEADS)
    units = [(c, h) for c in range(DN_STEP_CHUNKS) for h in heads]
    rs = [slice(c * C, (c + 1) * C) for c in range(DN_STEP_CHUNKS)]
    sls = [slice(h * DN_HD, (h + 1) * DN_HD) for h in heads]
    G_c, A, aqk, Z = {}, {}, {}, {}
    for u in units:
        c, h = u
        G_c[u] = G[rs[c], DN_HEADS + h:DN_HEADS + h + 1]
        gram = _dot_nt(jnp.concatenate([kb16[rs[c], sls[h]], qn16[rs[c], sls[h]]], axis=0), kn16[rs[c], sls[h]])
        G_r = GT[DN_HEADS + h:DN_HEADS + h + 1, rs[c]]
        decay = jnp.exp(jnp.where(tril, G_c[u] - G_r, -jnp.inf))
        A[u] = (gram[:C] * decay).astype(BF16)
        aqk[u] = (gram[C:] * decay).astype(BF16)
    for u in units:
        c, h = u
        X = jnp.concatenate([vb_all[rs[c], sls[h]], kbe_all[rs[c], sls[h]]], axis=1)
        L1 = jnp.where(level_masks[0], A[u], 0.0)
        Z[u] = jnp.concatenate([eye - L1.astype(F32), X - _dot(L1, X.astype(BF16))], axis=1)
    for lvl in range(1, levels):
        Z16 = {u: Z[u].astype(BF16) for u in units}
        cols = slice(0, C + 2 * DN_HD) if lvl < levels - 1 else slice(C, C + 2 * DN_HD)
        Y = {u: _dot(jnp.where(level_masks[lvl], A[u], 0.0), Z16[u][:, cols]).astype(BF16) for u in units}
        Z = {u: Z[u][:, cols] - _dot(Z16[u][:, :C], Y[u]) for u in units}
    S = [state_ref[h] for h in heads]
    for c in range(DN_STEP_CHUNKS):
        us = [(c, h) for h in heads]
        ws = [_dot(jnp.concatenate([Z[u][:, DN_HD:].astype(BF16), qe16[rs[c], sls[u[1]]]], axis=0),
                   S[u[1]].astype(BF16)) for u in us]
        v16 = [(Z[u][:, :DN_HD] - ws[u[1]][:C]).astype(BF16) for u in us]
        for u in us:
            h = u[1]
            G_last = G_c[u][C - 1:C, :]
            S[h] = S[h] * jnp.exp(G_last) + _dot_tn(kdec16[rs[c], sls[h]], v16[h])
        for u in us:
            h = u[1]
            oraw_ref[rs[c], sls[h]] = ws[h][C:] + _dot(aqk[u], v16[h])
    for h in heads:
        state_ref[h] = S[h]
    o = oraw_ref[...]
    ms = _dot_nt((o * o).astype(BF16), expb) * (1.0 / DN_HD)
    o = o * per_head(lax.rsqrt(ms + RMS_EPS), expb) * ng_ref[...]
    o_ref[...] = (o * _silu(z_ref[...].astype(F32))).astype(BF16)


def _deltanet(proj, ba, conv_w, alog_row, dtb_row, norm_g, batch, seq):
    n = proj.shape[0]
    C = DN_STEP_CHUNKS * DN_CHUNK
    nt = seq // C
    hb = C // SEQ_HALO

    def head_lanes(first_row):
        m = np.zeros((BA_LANES, DN_W), np.float32)
        for h in range(DN_HEADS):
            m[first_row + h, h * DN_HD:(h + 1) * DN_HD] = 1.0
        return m

    def cur(col):
        return pl.BlockSpec((C, DN_W), lambda b, i: (b * nt + i, col))

    def halo(col):
        return pl.BlockSpec((SEQ_HALO, DN_W), lambda b, i: (jnp.maximum((b * nt + i) * hb - 1, 0), col))

    return pl.pallas_call(
        _deltanet_kernel,
        grid=(batch, nt),
        in_specs=[cur(COL_DQ), cur(COL_DQ + 1), cur(COL_DQ + 2),
                  halo(COL_DQ), halo(COL_DQ + 1), halo(COL_DQ + 2),
                  cur(COL_DZ),
                  pl.BlockSpec((C, BA_LANES), lambda b, i: (b * nt + i, 0)),
                  pl.BlockSpec((3, DN_CONV, DN_W), lambda b, i: (0, 0, 0)),
                  pl.BlockSpec((1, BA_LANES), lambda b, i: (0, 0)),
                  pl.BlockSpec((1, BA_LANES), lambda b, i: (0, 0)),
                  pl.BlockSpec((1, DN_W), lambda b, i: (0, 0)),
                  pl.BlockSpec((BA_LANES, DN_W), lambda b, i: (0, 0)),
                  pl.BlockSpec((BA_LANES, DN_W), lambda b, i: (0, 0))],
        out_specs=pl.BlockSpec((C, DN_W), lambda b, i: (b * nt + i, 0)),
        out_shape=jax.ShapeDtypeStruct((n, DN_W), BF16),
        scratch_shapes=[pltpu.VMEM((DN_HEADS, DN_HD, DN_HD), F32), pltpu.VMEM((C, DN_W), F32)],
        compiler_params=_cparams("parallel", "arbitrary"),
        name="deltanet",
    )(proj, proj, proj, proj, proj, proj, proj, ba, conv_w, alog_row, dtb_row, norm_g,
      jnp.asarray(head_lanes(0), BF16), jnp.asarray(head_lanes(DN_HEADS), BF16))


def _merge_kernel(x_ref, yp_ref, ym_ref, yd_ref, g0_ref, g1_ref, g2_ref, wp_ref, wm_ref, wd_ref, wo_ref, o_ref):
    merged = (g0_ref[...].astype(F32) * _dot(yp_ref[...], wp_ref[...])
              + g1_ref[...].astype(F32) * _dot(ym_ref[...], wm_ref[...])
              + g2_ref[...].astype(F32) * _dot(yd_ref[...], wd_ref[...]))
    o_ref[...] = x_ref[...] + _dot(merged.astype(BF16), wo_ref[...])


def _merge(x, y_pool, y_moba, y_dn, proj, w_up_pool, w_up_moba, w_up_dn, w_out, tm):
    n = x.shape[0]

    def rows(width, col=0):
        return pl.BlockSpec((tm, width), lambda i: (i, col))

    def whole(shape):
        return pl.BlockSpec(shape, lambda i: (0, 0))

    return pl.pallas_call(
        _merge_kernel,
        grid=(n // tm,),
        in_specs=[rows(D_MODEL), rows(POOL_W), rows(MOBA_AUG_W), rows(DN_W),
                  rows(D_MODEL, COL_GATE0), rows(D_MODEL, COL_GATE0 + 1), rows(D_MODEL, COL_GATE0 + 2),
                  whole((POOL_W, D_MODEL)), whole((MOBA_AUG_W, D_MODEL)), whole((DN_W, D_MODEL)),
                  whole((D_MODEL, D_MODEL))],
        out_specs=rows(D_MODEL),
        out_shape=jax.ShapeDtypeStruct((n, D_MODEL), F32),
        compiler_params=_cparams("parallel"),
        name="merge",
    )(x, y_pool, y_moba, y_dn, proj, proj, proj, w_up_pool, w_up_moba, w_up_dn, w_out)


def _memkv_kernel(mem_ref, g_ref, w_ref, o_ref):
    o_ref[...] = _dot(_rms(mem_ref[...], g_ref[...]).astype(BF16), w_ref[...]).astype(BF16)


def _memkv(mem2d, g, wkv):
    m = mem2d.shape[0]
    tm = 256
    return pl.pallas_call(
        _memkv_kernel,
        grid=(m // tm,),
        in_specs=[pl.BlockSpec((tm, D_MODEL), lambda i: (i, 0)),
                  pl.BlockSpec((1, D_MODEL), lambda i: (0, 0)),
                  pl.BlockSpec((D_MODEL, 2 * XA_W), lambda i: (0, 0))],
        out_specs=pl.BlockSpec((tm, 2 * XA_W), lambda i: (i, 0)),
        out_shape=jax.ShapeDtypeStruct((m, 2 * XA_W), BF16),
        compiler_params=_cparams("parallel"),
        name="memkv",
    )(mem2d, g, wkv)


def _xattn_kernel(x_ref, g_ref, wq_ref, kv_ref, wo_ref, o_ref, *maybe_rows_ref):
    x = x_ref[...]
    q = _dot(_rms(x, g_ref[...]).astype(BF16), wq_ref[...]).astype(BF16)
    scale = XA_HD ** -0.5
    outs = []
    for h in range(XA_HEADS):
        sl = slice(h * XA_HD, (h + 1) * XA_HD)
        k = kv_ref[:, h * XA_HD:(h + 1) * XA_HD]
        v = kv_ref[:, XA_W + h * XA_HD:XA_W + (h + 1) * XA_HD]
        s = _dot_nt(q[:, sl], k) * scale
        s = s - jnp.max(s, axis=1, keepdims=True)
        p = jnp.exp(s)
        p = p / jnp.sum(p, axis=1, keepdims=True)
        outs.append(_dot(p.astype(BF16), v).astype(BF16))
    o = jnp.concatenate(outs, axis=1)
    out = x + _dot(o, wo_ref[...])
    o_ref[...] = out
    for rows_ref in maybe_rows_ref:
        for s in range(ROW_TILE):
            rows_ref[pl.ds(s, out.shape[0], stride=ROW_TILE), :] = out[:, s * 128:(s + 1) * 128]


def _xattn(x, g, wq, kv, wo, seq, mem_len, tm, with_row_tiles):
    n = x.shape[0]
    tiles_per_seq = seq // tm
    out_specs = [pl.BlockSpec((tm, D_MODEL), lambda i: (i, 0))]
    out_shape = [jax.ShapeDtypeStruct((n, D_MODEL), F32)]
    if with_row_tiles:
        out_specs.append(pl.BlockSpec((tm * ROW_TILE, 128), lambda i: (i, 0)))
        out_shape.append(jax.ShapeDtypeStruct((n * ROW_TILE, 128), F32))
    return pl.pallas_call(
        _xattn_kernel,
        grid=(n // tm,),
        in_specs=[pl.BlockSpec((tm, D_MODEL), lambda i: (i, 0)),
                  pl.BlockSpec((1, D_MODEL), lambda i: (0, 0)),
                  pl.BlockSpec((D_MODEL, XA_W), lambda i: (0, 0)),
                  pl.BlockSpec((mem_len, 2 * XA_W), lambda i: (i // tiles_per_seq, 0)),
                  pl.BlockSpec((XA_W, D_MODEL), lambda i: (0, 0))],
        out_specs=out_specs,
        out_shape=out_shape,
        compiler_params=_cparams("parallel"),
        name="xattn",
    )(x, g, wq, kv, wo)


def _swiglu_chunk(h, wg_ref, wu_ref, wd_ref):
    a = _silu(_dot(h, wg_ref[...])) * _dot(h, wu_ref[...])
    return _dot(a.astype(BF16), wd_ref[...])


def _ffn_kernel(x_ref, g_ref, wg_ref, wu_ref, wd_ref, o_ref, h_ref, acc_ref):
    j = pl.program_id(1)

    @pl.when(j == 0)
    def _():
        h_ref[...] = _rms(x_ref[...], g_ref[...]).astype(BF16)
        acc_ref[...] = x_ref[...]

    acc_ref[...] += _swiglu_chunk(h_ref[...], wg_ref, wu_ref, wd_ref)

    @pl.when(j == pl.num_programs(1) - 1)
    def _():
        o_ref[...] = acc_ref[...]


def _ffn(x, g, w_gate_up, w_down, tm, tf):
    n = x.shape[0]
    ff = w_down.shape[0]
    nf = ff // tf
    return pl.pallas_call(
        _ffn_kernel,
        grid=(n // tm, nf),
        in_specs=[pl.BlockSpec((tm, D_MODEL), lambda i, j: (i, 0)),
                  pl.BlockSpec((1, D_MODEL), lambda i, j: (0, 0)),
                  pl.BlockSpec((D_MODEL, tf), lambda i, j: (0, j)),
                  pl.BlockSpec((D_MODEL, tf), lambda i, j: (0, nf + j)),
                  pl.BlockSpec((tf, D_MODEL), lambda i, j: (j, 0))],
        out_specs=pl.BlockSpec((tm, D_MODEL), lambda i, j: (i, 0)),
        out_shape=jax.ShapeDtypeStruct((n, D_MODEL), F32),
        scratch_shapes=[pltpu.VMEM((tm, D_MODEL), BF16), pltpu.VMEM((tm, D_MODEL), F32)],
        compiler_params=_cparams("parallel", "arbitrary"),
        name="ffn",
    )(x, g, w_gate_up, w_gate_up, w_down)


def _route_kernel(x_ref, g_ref, r_ref, o_ref, *, tm):
    lane = lax.broadcasted_iota(jnp.int32, (tm, 128), 1).astype(F32)
    h = _rms(x_ref[...], g_ref[...]).astype(BF16)
    logits = jnp.where(lane < float(N_EXPERTS), _dot(h, r_ref[...]), -jnp.inf)
    m1 = jnp.max(logits, axis=1, keepdims=True)
    i1 = jnp.min(jnp.where(logits == m1, lane, 128.0), axis=1, keepdims=True)
    rest = jnp.where(lane == i1, -jnp.inf, logits)
    m2 = jnp.max(rest, axis=1, keepdims=True)
    i2 = jnp.min(jnp.where(rest == m2, lane, 128.0), axis=1, keepdims=True)
    e2 = jnp.exp(m2 - m1)
    o_ref[...] = (jnp.where(lane == 0.0, i1, 0.0) + jnp.where(lane == 1.0, i2, 0.0)
                  + jnp.where(lane == 2.0, 1.0 / (1.0 + e2), 0.0) + jnp.where(lane == 3.0, e2 / (1.0 + e2), 0.0))


def _route(x, g, router, tm):
    n = x.shape[0]
    return pl.pallas_call(
        functools.partial(_route_kernel, tm=tm),
        grid=(n // tm,),
        in_specs=[pl.BlockSpec((tm, D_MODEL), lambda i: (i, 0)),
                  pl.BlockSpec((1, D_MODEL), lambda i: (0, 0)),
                  pl.BlockSpec((D_MODEL, 128), lambda i: (0, 0))],
        out_specs=pl.BlockSpec((tm, 128), lambda i: (i, 0)),
        out_shape=jax.ShapeDtypeStruct((n, 128), F32),
        compiler_params=_cparams("parallel"),
        name="moe_route",
    )(x, g, router)


def _row_gather(idx_smem, src_hbm, dst, count, sem):
    def row_copy(r):
        s0 = pl.multiple_of(idx_smem[r] * ROW_TILE, ROW_TILE)
        d0 = pl.multiple_of(r * ROW_TILE, ROW_TILE)
        return pltpu.make_async_copy(src_hbm.at[pl.ds(s0, ROW_TILE), :], dst.at[pl.ds(d0, ROW_TILE), :], sem)

    def start():
        @pl.loop(0, count, step=GATHER_UNROLL)
        def _(r0):
            for u in range(GATHER_UNROLL):
                row_copy(r0 + u).start()

    def wait():
        @pl.loop(0, count, step=GATHER_UNROLL)
        def _(r0):
            for u in range(GATHER_UNROLL):
                row_copy(r0 + u).wait()

    return start, wait


def _gather_ahead(step, n_steps, idx_now, idx_next, idx_smem, src_hbm, buf, count, sems):
    slot = step % 2

    def begin(idx_vmem, sl):
        stage = pltpu.make_async_copy(idx_vmem, idx_smem[sl], sems.at[2])
        stage.start()
        stage.wait()
        _row_gather(idx_smem[sl], src_hbm, buf.at[sl], count, sems.at[sl])[0]()

    @pl.when(step == 0)
    def _():
        begin(idx_now, 0)

    for sl in range(2):
        @pl.when(slot == sl)
        def _(sl=sl):
            @pl.when(step + 1 < n_steps)
            def _():
                begin(idx_next, 1 - sl)

            _row_gather(idx_smem[sl], src_hbm, buf.at[sl], count, sems.at[sl])[1]()

    return slot


def _rms_row_tiles(v, g8, rows):
    v3 = v.reshape(rows, ROW_TILE, 128)
    ms = jnp.sum(jnp.sum(v3 * v3, axis=2, keepdims=True), axis=1, keepdims=True) * (1.0 / D_MODEL)
    return (v3 * lax.rsqrt(ms + RMS_EPS) * g8[None]).reshape(rows * ROW_TILE, 128)


def _expert_kernel(te_ref, nu_ref, x_hbm, src_ref, src_next_ref, w_ref, g8_ref, wg_ref, wu_ref, wd_ref, o_ref,
                   idx0_ref, idx1_ref, xbuf_ref, x16_ref, sems, *, tm):
    t = pl.program_id(0)
    del te_ref

    @pl.when(t >= nu_ref[0])
    def _():
        o_ref[...] = jnp.zeros_like(o_ref)

    @pl.when(t < nu_ref[0])
    def _():
        slot = _gather_ahead(t, nu_ref[0], src_ref.at[0, 0], src_next_ref.at[0, 0], (idx0_ref, idx1_ref), x_hbm, xbuf_ref,
                             tm, sems)
        xb = xbuf_ref.at[slot]
        xb[...] = _rms_row_tiles(xb[...], g8_ref[...], tm)
        for s in range(ROW_TILE):
            x16_ref[:, s * 128:(s + 1) * 128] = xb[pl.ds(s, tm, stride=ROW_TILE), :].astype(BF16)
        y = _swiglu_chunk(x16_ref[...], wg_ref.at[0], wu_ref.at[0], wd_ref.at[0]) * w_ref[...]
        for s in range(ROW_TILE):
            o_ref[pl.ds(s, tm, stride=ROW_TILE), :] = y[:, s * 128:(s + 1) * 128]


def _experts(x_rows, src, w_sorted, tile_expert, n_used, g8, w_gate_up, w_down, tm):
    nt = src.shape[0]
    ff = w_down.shape[1]
    grid_spec = pltpu.PrefetchScalarGridSpec(
        num_scalar_prefetch=2,
        grid=(nt,),
        in_specs=[pl.BlockSpec(memory_space=pl.ANY),
                  pl.BlockSpec((1, 1, tm), lambda t, te, nu: (t, 0, 0)),
                  pl.BlockSpec((1, 1, tm), lambda t, te, nu: (jnp.minimum(t + 1, nt - 1), 0, 0)),
                  pl.BlockSpec((tm, 1), lambda t, te, nu: (t, 0)),
                  pl.BlockSpec((ROW_TILE, 128), lambda t, te, nu: (0, 0)),
                  pl.BlockSpec((1, D_MODEL, ff), lambda t, te, nu: (te[t], 0, 0)),
                  pl.BlockSpec((1, D_MODEL, ff), lambda t, te, nu: (te[t], 0, 1)),
                  pl.BlockSpec((1, ff, D_MODEL), lambda t, te, nu: (te[t], 0, 0))],
        out_specs=pl.BlockSpec((tm * ROW_TILE, 128), lambda t, te, nu: (t, 0)),
        scratch_shapes=[pltpu.SMEM((tm,), jnp.int32), pltpu.SMEM((tm,), jnp.int32),
                        pltpu.VMEM((2, tm * ROW_TILE, 128), F32),
                        pltpu.VMEM((tm, D_MODEL), BF16), pltpu.SemaphoreType.DMA((3,))])
    return pl.pallas_call(
        functools.partial(_expert_kernel, tm=tm),
        grid_spec=grid_spec,
        out_shape=jax.ShapeDtypeStruct((nt * tm * ROW_TILE, 128), F32),
        compiler_params=_cparams("arbitrary"),
        name="moe_experts",
    )(tile_expert, n_used, x_rows, src, src, w_sorted, g8, w_gate_up, w_gate_up, w_down)


def _combine_kernel(x_ref, y_hbm, dest_ref, dest_next_ref, fg_ref, o_ref, idx0_ref, idx1_ref, ybuf_ref, sems, *, tt):
    slot = _gather_ahead(pl.program_id(0), pl.num_programs(0), dest_ref.at[0, 0], dest_next_ref.at[0, 0],
                         (idx0_ref, idx1_ref),
                         y_hbm, ybuf_ref, MOE_TOPK * tt, sems)
    yb = ybuf_ref.at[slot]
    n8 = tt * ROW_TILE
    for s in range(ROW_TILE):
        sl = slice(s * 128, (s + 1) * 128)
        o_ref[:, sl] = (x_ref[:, sl] + yb[pl.ds(s, tt, stride=ROW_TILE), :]
                        + yb[pl.ds(n8 + s, tt, stride=ROW_TILE), :])
    o_ref[...] = _rms(o_ref[...], fg_ref[...])


def _combine_final(x, y_rows, dest, fg, tt):
    nt = dest.shape[0]
    return pl.pallas_call(
        functools.partial(_combine_kernel, tt=tt),
        grid=(nt,),
        in_specs=[pl.BlockSpec((tt, D_MODEL), lambda i: (i, 0)),
                  pl.BlockSpec(memory_space=pl.ANY),
                  pl.BlockSpec((1, 1, MOE_TOPK * tt), lambda i: (i, 0, 0)),
                  pl.BlockSpec((1, 1, MOE_TOPK * tt), lambda i: (jnp.minimum(i + 1, nt - 1), 0, 0)),
                  pl.BlockSpec((1, D_MODEL), lambda i: (0, 0))],
        out_specs=pl.BlockSpec((tt, D_MODEL), lambda i: (i, 0)),
        out_shape=jax.ShapeDtypeStruct(x.shape, F32),
        scratch_shapes=[pltpu.SMEM((MOE_TOPK * tt,), jnp.int32), pltpu.SMEM((MOE_TOPK * tt,), jnp.int32),
                        pltpu.VMEM((2, MOE_TOPK * tt * ROW_TILE, 128), F32), pltpu.SemaphoreType.DMA((3,))],
        compiler_params=_cparams("arbitrary"),
        name="moe_combine",
    )(x, y_rows, dest, dest, fg)


def _moe_sparse_final(xf, x_rows, g, router, w_gate_up, w_down, final_g, tm, tt):
    n = xf.shape[0]
    nk = MOE_TOPK * n
    route = _route(xf, g, router, tm)
    e_flat = route[:, :MOE_TOPK].astype(jnp.int32).reshape(nk)
    w_flat = route[:, MOE_TOPK:2 * MOE_TOPK].reshape(nk)
    onehot = (e_flat[:, None] == jnp.arange(N_EXPERTS, dtype=jnp.int32)[None, :]).astype(jnp.int32)
    csum = jnp.cumsum(onehot, axis=0)
    cnt = csum[-1]
    rank = jnp.take_along_axis(csum, e_flat[:, None], axis=1)[:, 0] - 1
    padded = ((cnt + tm - 1) // tm) * tm
    ends = jnp.cumsum(padded)
    off = ends - padded
    start = jnp.cumsum(cnt) - cnt
    dest = off[e_flat] + rank
    nt = nk // tm + N_EXPERTS
    slot = jnp.arange(nt * tm, dtype=jnp.int32)
    e_slot = jnp.minimum(jnp.searchsorted(ends, slot, side="right"), N_EXPERTS - 1).astype(jnp.int32)
    loc = slot - off[e_slot]
    valid = loc < cnt[e_slot]
    order = jnp.argsort(e_flat, stable=True).astype(jnp.int32)
    pair = order[jnp.clip(start[e_slot] + loc, 0, nk - 1)]
    src = jnp.where(valid, pair // MOE_TOPK, 0).reshape(nt, 1, tm)
    w_sorted = jnp.where(valid, w_flat[pair], 0.0).reshape(nt * tm, 1)
    tile_expert = e_slot[::tm]
    n_used = (ends[-1:] // tm).astype(jnp.int32)

    y_rows = _experts(x_rows, src, w_sorted, tile_expert, n_used, g.reshape(ROW_TILE, 128), w_gate_up, w_down, tm)
    dest = dest.reshape(n // tt, tt, MOE_TOPK).transpose(0, 2, 1).reshape(n // tt, 1, MOE_TOPK * tt)
    return _combine_final(xf, y_rows, dest, final_g[None, :], tt)


def _blockdiag(pool_w):
    g, c, _ = pool_w.shape
    out = jnp.zeros((g * c, g * c), pool_w.dtype)
    for gi in range(g):
        out = out.at[gi * c:(gi + 1) * c, gi * c:(gi + 1) * c].set(pool_w[gi])
    return out


def _lane_row(v, offset):
    return jnp.zeros((1, BA_LANES), F32).at[0, offset:offset + v.shape[0]].set(v.astype(F32))


def kernel(x, mem, positions, mix_norm_g, w_in, pool_w, pool_scale, dn_conv_w, dn_a_log, dn_dt_bias, dn_norm_g,
           w_up_pool, w_up_moba, w_up_dn, w_out, xa_norm_g, mem_norm_g, xa_wq, xa_wkv, xa_wo, ffn_norm_g,
           dense_w_gate_up, dense_w_down, moe_router, moe_w_gate_up, moe_w_down, final_norm_g):
    batch, seq, d = x.shape
    depth = w_in.shape[0]
    mem_len = mem.shape[1]
    n = batch * seq
    nb = seq // MOBA_BLOCK
    assert d == D_MODEL and seq % MOBA_BLOCK == 0 and nb <= MOBA_HD and nb % (2 * MOBA_GROUP) == 0 and depth == 2
    tm = min(512, seq)

    xf = x.reshape(n, d)
    mem2d = mem.reshape(batch * mem_len, d)
    cos_t, sin_t = _rope_tables(positions.reshape(n, 1).astype(F32), tm)
    off_ba = POOL_W + 3 * MOBA_W + 3 * DN_W

    for layer in range(depth):
        wl = w_in[layer]
        w_main = jnp.concatenate([wl[:, :off_ba], wl[:, off_ba + 2 * DN_HEADS:]], axis=1).astype(BF16)
        w_ba = jnp.pad(wl[:, off_ba:off_ba + 2 * DN_HEADS], ((0, 0), (0, BA_LANES - 2 * DN_HEADS))).astype(BF16)
        proj, ba = _inproj(xf, mix_norm_g[layer][None, :], w_main, w_ba, min(2048, seq))

        y_pool = _pool(proj, _blockdiag(pool_w[layer]).astype(BF16), pool_scale[layer][None, :], batch, seq, tm)
        mq, mk, mv, kbar = _moba_prep(proj, cos_t, sin_t, nb)
        kbar = jnp.pad(kbar.reshape(batch, nb, MOBA_W), ((0, 0), (0, MOBA_KBAR_ROWS - nb), (0, 0)))
        y_moba = _moba(mq, mk, mv, kbar, batch, seq)
        w_moba = jnp.pad(w_up_moba[layer].reshape(MOBA_HEADS, MOBA_HD, d), ((0, 0), (0, MOBA_HD), (0, 0)))
        w_moba = w_moba.reshape(MOBA_AUG_W, d).astype(BF16)
        conv_w = dn_conv_w[layer].reshape(DN_CONV, 3, DN_W).transpose(1, 0, 2)
        y_dn = _deltanet(proj, ba, conv_w, _lane_row(dn_a_log[layer], DN_HEADS),
                         _lane_row(dn_dt_bias[layer], DN_HEADS), jnp.tile(dn_norm_g[layer], DN_HEADS)[None, :], batch, seq)
        xf = _merge(xf, y_pool, y_moba, y_dn, proj, w_up_pool[layer].astype(BF16), w_moba,
                    w_up_dn[layer].astype(BF16), w_out[layer].astype(BF16), tm)

        kv = _memkv(mem2d, mem_norm_g[layer][None, :], xa_wkv[layer].astype(BF16))
        routed = layer % 2 == 1
        xf, *x_rows = _xattn(xf, xa_norm_g[layer][None, :], xa_wq[layer].astype(BF16), kv,
                             xa_wo[layer].astype(BF16), seq, mem_len, tm, routed)

        if layer % 2 == 0:
            xf = _ffn(xf, ffn_norm_g[layer][None, :], dense_w_gate_up[layer // 2].astype(BF16),
                      dense_w_down[layer // 2].astype(BF16), tm, dense_w_down.shape[1] // 2)
        else:
            router = jnp.pad(moe_router[layer // 2], ((0, 0), (0, 128 - N_EXPERTS))).astype(BF16)
            xf = _moe_sparse_final(xf, x_rows[0], ffn_norm_g[layer][None, :], router, moe_w_gate_up[layer // 2].astype(BF16),
                                   moe_w_down[layer // 2].astype(BF16), final_norm_g, tm, min(256, seq))
    return xf.reshape(batch, seq, d)
```

```python
import functools

import numpy as np
import jax
import jax.numpy as jnp
from jax import lax
from jax.experimental import pallas as pl
from jax.experimental.pallas import tpu as pltpu

F32 = jnp.float32
BF16 = jnp.bfloat16

RMS_EPS = 1e-6
D_MODEL = 1024
POOL_W = 256
POOL_GROUP_W = 64
POOL_WINDOWS = (2, 4, 8, 16)
SEQ_HALO = 16
MOBA_HEADS = 4
MOBA_HD = 64
MOBA_W = 256
MOBA_AUG_W = 2 * MOBA_W
MOBA_BLOCK = 256
MOBA_TOPK = 3
MOBA_KBAR_ROWS = 128
MOBA_GROUP = 2
ROPE_THETA = 500000.0
ROPE_DIMS = 16
DN_HEADS = 8
DN_HD = 64
DN_W = 512
DN_CONV = 4
DN_CHUNK = 128
DN_STEP_CHUNKS = 2
XA_HEADS = 4
XA_HD = 128
XA_W = 512
N_EXPERTS = 8
MOE_TOPK = 2
ROW_TILE = 8
GATHER_UNROLL = 16
COL_POOL, COL_MQ, COL_MK, COL_MV = 0, 1, 2, 3
COL_DQ, COL_DZ = 2, 5
COL_GATE0 = 3
PROJ_COLS = 6144
PROJ_TN = 1024
PROJ_ID_BLOCKS = 3
BA_LANES = 128
VMEM_LIMIT = 56 * 1024 * 1024
NEG_BIG = -1e30
LOG2_E = 1.4426950408889634


def _cparams(*sem):
    return pltpu.CompilerParams(dimension_semantics=sem, vmem_limit_bytes=VMEM_LIMIT)


def _rms(x, g):
    ms = jnp.mean(x * x, axis=-1, keepdims=True)
    return x * lax.rsqrt(ms + RMS_EPS) * g


def _silu(x):
    return x * jax.nn.sigmoid(x)


def _dot(a, b):
    return jnp.dot(a, b, preferred_element_type=F32)


def _dot_nt(a, b):
    return lax.dot_general(a, b, (((1,), (1,)), ((), ())), preferred_element_type=F32)


def _dot_tn(a, b):
    return lax.dot_general(a, b, (((0,), (0,)), ((), ())), preferred_element_type=F32)


def _inproj_kernel(x_ref, g_ref, w_ref, wba_ref, proj_ref, ba_ref, h_ref):
    j = pl.program_id(1)

    @pl.when(j == 0)
    def _():
        h = _rms(x_ref[...], g_ref[...]).astype(BF16)
        h_ref[...] = h
        ba_ref[...] = _dot(h, wba_ref[...])

    acc = _dot(h_ref[...], w_ref[...])

    @pl.when(j < PROJ_ID_BLOCKS)
    def _():
        proj_ref[...] = acc.astype(BF16)

    @pl.when(j >= PROJ_ID_BLOCKS)
    def _():
        proj_ref[...] = jax.nn.sigmoid(acc).astype(BF16)


def _inproj(x, g, w_main, w_ba, tm):
    n = x.shape[0]
    return pl.pallas_call(
        _inproj_kernel,
        grid=(n // tm, PROJ_COLS // PROJ_TN),
        in_specs=[
            pl.BlockSpec((tm, D_MODEL), lambda i, j: (i, 0)),
            pl.BlockSpec((1, D_MODEL), lambda i, j: (0, 0)),
            pl.BlockSpec((D_MODEL, PROJ_TN), lambda i, j: (0, j)),
            pl.BlockSpec((D_MODEL, BA_LANES), lambda i, j: (0, 0)),
        ],
        out_specs=[
            pl.BlockSpec((tm, PROJ_TN), lambda i, j: (i, j)),
            pl.BlockSpec((tm, BA_LANES), lambda i, j: (i, 0)),
        ],
        out_shape=[jax.ShapeDtypeStruct((n, PROJ_COLS), BF16),
                   jax.ShapeDtypeStruct((n, BA_LANES), F32)],
        scratch_shapes=[pltpu.VMEM((tm, D_MODEL), BF16)],
        compiler_params=_cparams("parallel", "arbitrary"),
        name="inproj",
    )(x, g, w_main, w_ba)


def _rope_table_kernel(pos_ref, freq_ref, sign_ref, c_ref, s_ref):
    ang = pos_ref[...] * freq_ref[...]
    rot = sign_ref[...]
    c_ref[...] = jnp.where(rot != 0.0, jnp.cos(ang), 1.0)
    s_ref[...] = jnp.sin(ang) * rot


def _rope_tables(pos_f32, tm):
    n = pos_f32.shape[0]
    half = ROPE_DIMS // 2
    inv_freq = np.power(ROPE_THETA, -np.arange(half, dtype=np.float32) * 2.0 / ROPE_DIMS).astype(np.float32)
    d = np.arange(128) % MOBA_HD
    freq = np.where(d < ROPE_DIMS, inv_freq[d % half], 0.0).astype(np.float32)[None, :]
    sign = np.where(d < half, -1.0, np.where(d < ROPE_DIMS, 1.0, 0.0)).astype(np.float32)[None, :]
    return pl.pallas_call(
        _rope_table_kernel,
        grid=(n // tm,),
        in_specs=[pl.BlockSpec((tm, 1), lambda i: (i, 0)),
                  pl.BlockSpec((1, 128), lambda i: (0, 0)),
                  pl.BlockSpec((1, 128), lambda i: (0, 0))],
        out_specs=[pl.BlockSpec((tm, 128), lambda i: (i, 0)),
                   pl.BlockSpec((tm, 128), lambda i: (i, 0))],
        out_shape=[jax.ShapeDtypeStruct((n, 128), F32), jax.ShapeDtypeStruct((n, 128), F32)],
        compiler_params=_cparams("parallel"),
        name="rope_tables",
    )(pos_f32, jnp.asarray(freq), jnp.asarray(sign))


def _moba_prep_kernel(q_ref, k_ref, v_ref, c_ref, s_ref, qo_ref, ko_ref, vo_ref, kbar_ref, *, nb):
    c = jnp.concatenate([c_ref[...], c_ref[...]], axis=1)
    s = jnp.concatenate([s_ref[...], s_ref[...]], axis=1)
    lane = lax.broadcasted_iota(jnp.int32, (MOBA_BLOCK, MOBA_W), 1)
    first_half = (lane % MOBA_HD) < (ROPE_DIMS // 2)

    def rope(x):
        partner = jnp.where(first_half,
                            pltpu.roll(x, MOBA_W - ROPE_DIMS // 2, 1),
                            pltpu.roll(x, ROPE_DIMS // 2, 1))
        return x * c + partner * s

    qo_ref[...] = (rope(q_ref[...].astype(F32)) * (MOBA_HD ** -0.5 * LOG2_E)).astype(BF16)
    k = rope(k_ref[...].astype(F32))
    kbar_ref[0] = jnp.mean(k, axis=0, keepdims=True)
    k16 = k.astype(BF16)
    blk_lane = lax.broadcasted_iota(jnp.int32, (MOBA_BLOCK, MOBA_HD), 1)
    onehot = jnp.where(blk_lane == pl.program_id(0) % nb, 1.0, 0.0).astype(BF16)
    k_parts = []
    for h in range(MOBA_HEADS):
        k_parts += [k16[:, h * MOBA_HD:(h + 1) * MOBA_HD], onehot]
    ko_ref[...] = jnp.concatenate(k_parts, axis=1)
    vo_ref[...] = v_ref[...].astype(F32).T.astype(BF16)


def _moba_prep(proj, cos_t, sin_t, nb):
    n = proj.shape[0]
    nblk = n // MOBA_BLOCK
    return pl.pallas_call(
        functools.partial(_moba_prep_kernel, nb=nb),
        grid=(nblk,),
        in_specs=[pl.BlockSpec((MOBA_BLOCK, MOBA_W), lambda r: (r, COL_MQ)),
                  pl.BlockSpec((MOBA_BLOCK, MOBA_W), lambda r: (r, COL_MK)),
                  pl.BlockSpec((MOBA_BLOCK, MOBA_W), lambda r: (r, COL_MV)),
                  pl.BlockSpec((MOBA_BLOCK, 128), lambda r: (r, 0)),
                  pl.BlockSpec((MOBA_BLOCK, 128), lambda r: (r, 0))],
        out_specs=[pl.BlockSpec((MOBA_BLOCK, MOBA_W), lambda r: (r, 0)),
                   pl.BlockSpec((MOBA_BLOCK, MOBA_AUG_W), lambda r: (r, 0)),
                   pl.BlockSpec((MOBA_W, MOBA_BLOCK), lambda r: (r, 0)),
                   pl.BlockSpec((1, 1, MOBA_W), lambda r: (r, 0, 0))],
        out_shape=[jax.ShapeDtypeStruct((n, MOBA_W), BF16),
                   jax.ShapeDtypeStruct((n, MOBA_AUG_W), BF16),
                   jax.ShapeDtypeStruct((nblk * MOBA_W, MOBA_BLOCK), BF16),
                   jax.ShapeDtypeStruct((nblk, 1, MOBA_W), F32)],
        compiler_params=_cparams("parallel"),
        name="moba_prep",
    )(proj, proj, proj, cos_t, sin_t)


def _moba_kernel(q_ref, k_ref, vt_ref, kbar_ref, o_ref, qa_ref, m_ref, ls_ref, acc_ref, s_ref):
    i = pl.program_id(1)
    heads = range(MOBA_HEADS)
    blk = lax.broadcasted_iota(jnp.int32, (MOBA_KBAR_ROWS, MOBA_BLOCK), 0).astype(F32)
    row = lax.broadcasted_iota(jnp.int32, (MOBA_BLOCK, MOBA_BLOCK), 0)
    col = lax.broadcasted_iota(jnp.int32, (MOBA_BLOCK, MOBA_BLOCK), 1)
    causal = row <= col
    i_f = i.astype(F32)
    own = pl.multiple_of(i * MOBA_BLOCK, MOBA_BLOCK)
    hs = [slice(h * 2 * MOBA_HD, (h + 1) * 2 * MOBA_HD) for h in heads]

    for h in heads:
        sl = slice(h * MOBA_HD, (h + 1) * MOBA_HD)
        qh = q_ref[:, sl]
        gate = _dot_nt(kbar_ref[0, :, sl].astype(BF16), qh)
        gate = jnp.where(blk < i_f, gate, -jnp.inf)
        keep = jnp.zeros(gate.shape, jnp.bool_)
        for _ in range(MOBA_TOPK):
            m = jnp.max(gate, axis=0, keepdims=True)
            idx = jnp.min(jnp.where(gate == m, blk, float(MOBA_KBAR_ROWS)), axis=0, keepdims=True)
            pick = (blk == idx) & (m > -jnp.inf)
            keep = keep | pick
            gate = jnp.where(pick, -jnp.inf, gate)
        bias = jnp.where(keep, 0.0, NEG_BIG).T[:, :MOBA_HD]
        qa_ref[h] = jnp.concatenate([qh, bias.astype(BF16)], axis=1)
        qa_ref[MOBA_HEADS + h] = jnp.concatenate([qh, jnp.zeros_like(qh)], axis=1)

    groups8 = MOBA_BLOCK // ROW_TILE

    def scores(h, start):
        return _dot_nt(k_ref[pl.ds(start, MOBA_BLOCK), hs[h]], qa_ref[h])

    def own_scores(h):
        s = _dot_nt(k_ref[pl.ds(own, MOBA_BLOCK), hs[h]], qa_ref[MOBA_HEADS + h])
        return jnp.where(causal, s, NEG_BIG)

    def rows8(x):
        return x.reshape(groups8, ROW_TILE, MOBA_BLOCK)

    def all8(x, op):
        return jnp.broadcast_to(op(x, axis=0, keepdims=True), x.shape)

    def values_t(h, start):
        return vt_ref[pl.ds(start + h * MOBA_HD, MOBA_HD), :]

    n_pairs = (i + (2 * MOBA_GROUP - 1)) // (2 * MOBA_GROUP)

    def group_blocks(g):
        return [pl.multiple_of((g * MOBA_GROUP + t) * MOBA_BLOCK, MOBA_BLOCK) for t in range(MOBA_GROUP)]

    for h in heads:
        s3 = rows8(own_scores(h))
        m = all8(jnp.max(s3, axis=0), jnp.max)
        p3 = jnp.exp2(s3 - m[None])
        ls_ref[h] = jnp.sum(p3, axis=0)
        acc_ref[h] = _dot(values_t(h, own), p3.reshape(MOBA_BLOCK, MOBA_BLOCK).astype(BF16))
        m_ref[h] = m

    def score_group(g, buf):
        gmax = [None] * MOBA_HEADS
        for t, start in enumerate(group_blocks(g)):
            for h in heads:
                s = scores(h, start)
                s_ref[buf, t, h] = s
                smax = jnp.max(rows8(s), axis=0)
                gmax[h] = smax if gmax[h] is None else jnp.maximum(gmax[h], smax)
        return gmax

    def rescale(gmax):
        for h in heads:
            m_old = m_ref[h]
            m_new = jnp.maximum(m_old, all8(gmax[h], jnp.max))
            alpha = jnp.exp2(m_old - m_new)
            ls_ref[h] = ls_ref[h] * alpha
            acc = acc_ref[h].reshape(MOBA_HD // ROW_TILE, ROW_TILE, MOBA_BLOCK) * alpha[None]
            acc_ref[h] = acc.reshape(MOBA_HD, MOBA_BLOCK)
            m_ref[h] = m_new

    def accumulate_group(g, buf):
        for t, start in enumerate(group_blocks(g)):
            for h in heads:
                p3 = jnp.exp2(rows8(s_ref[buf, t, h]) - m_ref[h][None])
                ls_ref[h] += jnp.sum(p3, axis=0)
                acc_ref[h] += _dot(values_t(h, start), p3.reshape(MOBA_BLOCK, MOBA_BLOCK).astype(BF16))

    @pl.when(n_pairs > 0)
    def _():
        rescale(score_group(0, 0))

    @pl.loop(0, n_pairs)
    def _(p):
        gmax = score_group(2 * p + 1, 1)
        accumulate_group(2 * p, 0)
        rescale(gmax)

        @pl.when(p + 1 < n_pairs)
        def _():
            gmax = score_group(2 * p + 2, 0)
            accumulate_group(2 * p + 1, 1)
            rescale(gmax)

        @pl.when(p + 1 == n_pairs)
        def _():
            accumulate_group(2 * p + 1, 1)

    for h in heads:
        o_t = acc_ref[h] / jnp.sum(ls_ref[h], axis=0, keepdims=True)
        o_ref[:, hs[h]] = jnp.concatenate([o_t, jnp.zeros_like(o_t)], axis=0).T.astype(BF16)


def _moba(q, k_aug, v_aug, kbar, batch, seq):
    n = q.shape[0]
    nb = seq // MOBA_BLOCK
    return pl.pallas_call(
        _moba_kernel,
        grid=(batch, nb),
        in_specs=[pl.BlockSpec((MOBA_BLOCK, MOBA_W), lambda b, i: (b * nb + i, 0)),
                  pl.BlockSpec((seq, MOBA_AUG_W), lambda b, i: (b, 0)),
                  pl.BlockSpec((seq, MOBA_W), lambda b, i: (b, 0)),
                  pl.BlockSpec((1, MOBA_KBAR_ROWS, MOBA_W), lambda b, i: (b, 0, 0))],
        out_specs=pl.BlockSpec((MOBA_BLOCK, MOBA_AUG_W), lambda b, i: (b * nb + i, 0)),
        out_shape=jax.ShapeDtypeStruct((n, MOBA_AUG_W), BF16),
        scratch_shapes=[pltpu.VMEM((2 * MOBA_HEADS, MOBA_BLOCK, 2 * MOBA_HD), BF16)]
        + [pltpu.VMEM((MOBA_HEADS, ROW_TILE, MOBA_BLOCK), F32)] * 2
        + [pltpu.VMEM((MOBA_HEADS, MOBA_HD, MOBA_BLOCK), F32)]
        + [pltpu.VMEM((2, MOBA_GROUP, MOBA_HEADS, MOBA_BLOCK, MOBA_BLOCK), F32)],
        compiler_params=_cparams("parallel", "arbitrary"),
        name="moba",
    )(q, k_aug, v_aug, kbar)


def _pool_kernel(p_ref, halo_ref, w_ref, scale_ref, o_ref, *, ts):
    i = pl.program_id(1)
    halo = jnp.where(i == 0, 0.0, halo_ref[...].astype(F32))
    p = p_ref[...].astype(F32)
    cur = jnp.concatenate([halo, p], axis=0)
    lane = lax.broadcasted_iota(jnp.int32, (ts, POOL_W), 1)
    t1 = (lax.broadcasted_iota(jnp.int32, (ts, POOL_W), 0) + i * ts + 1).astype(F32)
    total = jnp.zeros((ts, POOL_W), F32)
    count = jnp.ones((ts, POOL_W), F32)
    span = 1
    for gi, w in enumerate(POOL_WINDOWS):
        while span < w:
            cur = cur + pltpu.roll(cur, span, 0)
            span *= 2
        in_group = (lane >= gi * POOL_GROUP_W) & (lane < (gi + 1) * POOL_GROUP_W)
        total = jnp.where(in_group, cur[SEQ_HALO:], total)
        count = jnp.where(in_group, jnp.minimum(t1, float(w)), count)
    pooled = total / count - p
    o_ref[...] = (_dot(pooled.astype(BF16), w_ref[...]) * scale_ref[...]).astype(BF16)


def _pool(proj, w_blockdiag, scale, batch, seq, ts):
    n = proj.shape[0]
    nt = seq // ts
    hb = ts // SEQ_HALO
    return pl.pallas_call(
        functools.partial(_pool_kernel, ts=ts),
        grid=(batch, nt),
        in_specs=[pl.BlockSpec((ts, POOL_W), lambda b, i: (b * nt + i, COL_POOL)),
                  pl.BlockSpec((SEQ_HALO, POOL_W), lambda b, i: (jnp.maximum((b * nt + i) * hb - 1, 0), COL_POOL)),
                  pl.BlockSpec((POOL_W, POOL_W), lambda b, i: (0, 0)),
                  pl.BlockSpec((1, POOL_W), lambda b, i: (0, 0))],
        out_specs=pl.BlockSpec((ts, POOL_W), lambda b, i: (b * nt + i, 0)),
        out_shape=jax.ShapeDtypeStruct((n, POOL_W), BF16),
        compiler_params=_cparams("parallel", "parallel"),
        name="pool",
    )(proj, proj, w_blockdiag, scale)


def _deltanet_kernel(q_ref, k_ref, v_ref, qh_ref, kh_ref, vh_ref, z_ref, ba_ref, cw_ref, alog_ref, dtb_ref,
                     ng_ref, expb_ref, expg_ref, o_ref, state_ref, oraw_ref):
    i = pl.program_id(1)
    C = DN_CHUNK
    TS = DN_STEP_CHUNKS * C

    @pl.when(i == 0)
    def _():
        state_ref[...] = jnp.zeros_like(state_ref)

    def conv(x_ref, halo_ref, w):
        halo = jnp.where(i == 0, 0.0, halo_ref[...].astype(F32))
        ext = jnp.concatenate([halo, x_ref[...].astype(F32)], axis=0)
        y = ext * w[DN_CONV - 1:DN_CONV]
        for lag in range(1, DN_CONV):
            y = y + pltpu.roll(ext, lag, 0) * w[DN_CONV - 1 - lag:DN_CONV - lag]
        return _silu(y[SEQ_HALO:])

    cw = cw_ref[...]
    qc = conv(q_ref, qh_ref, cw[0])
    kc = conv(k_ref, kh_ref, cw[1])
    vc = conv(v_ref, vh_ref, cw[2])

    ba = ba_ref[...]
    beta = jax.nn.sigmoid(ba)
    g = -jnp.exp(alog_ref[...]) * jax.nn.softplus(ba + dtb_ref[...])
    rows = lax.broadcasted_iota(jnp.int32, (TS, BA_LANES), 0) & (C - 1)
    G = g
    span = 1
    while span < C:
        G = G + jnp.where(rows >= span, pltpu.roll(G, span, 0), 0.0)
        span *= 2
    GT = G.T
    r_i = lax.broadcasted_iota(jnp.int32, (C, C), 0)
    c_i = lax.broadcasted_iota(jnp.int32, (C, C), 1)
    tril = c_i <= r_i
    eye = (c_i == r_i).astype(F32)
    levels = C.bit_length() - 1
    level_masks = [((r_i >> k) & 1 == 1) & ((c_i >> k) == (r_i >> k) - 1) for k in range(levels)]
    expb = expb_ref[...]
    expg = expg_ref[...]

    def per_head(x, e):
        hi = x.astype(BF16)
        lo = (x - hi.astype(F32)).astype(BF16)
        return _dot(hi, e) + _dot(lo, e)

    def inv_norm(x):
        ss = _dot_nt((x * x).astype(BF16), expb)
        return per_head(lax.rsqrt(ss + RMS_EPS), expb)

    qn_all = qc * (inv_norm(qc) * (DN_HD ** -0.5))
    kn_all = kc * inv_norm(kc)
    beta_all = per_head(beta, expb)
    eG_all = per_head(jnp.exp(G), expg)
    G_last_rows = jnp.concatenate(
        [jnp.broadcast_to(G[(c + 1) * C - 1:(c + 1) * C, :], (C, BA_LANES)) for c in range(DN_STEP_CHUNKS)], axis=0)
    kb_all = kn_all * beta_all
    vb_all = vc * beta_all
    kbe_all = kb_all * eG_all
    qn16 = qn_all.astype(BF16)
    kn16 = kn_all.astype(BF16)
    kb16 = kb_all.astype(BF16)
    qe16 = (qn_all * eG_all).astype(BF16)
    kdec16 = (kn_all * per_head(jnp.exp(G_last_rows - G), expg)).astype(BF16)

    heads = range(DN_HEADS)
    units = [(c, h) for c in range(DN_STEP_CHUNKS) for h in heads]
    rs = [slice(c * C, (c + 1) * C) for c in range(DN_STEP_CHUNKS)]
    sls = [slice(h * DN_HD, (h + 1) * DN_HD) for h in heads]
    G_c, A, aqk, Z = {}, {}, {}, {}
    for u in units:
        c, h = u
        G_c[u] = G[rs[c], DN_HEADS + h:DN_HEADS + h + 1]
        gram = _dot_nt(jnp.concatenate([kb16[rs[c], sls[h]], qn16[rs[c], sls[h]]], axis=0), kn16[rs[c], sls[h]])
        G_r = GT[DN_HEADS + h:DN_HEADS + h + 1, rs[c]]
        decay = jnp.exp(jnp.where(tril, G_c[u] - G_r, -jnp.inf))
        A[u] = (gram[:C] * decay).astype(BF16)
        aqk[u] = (gram[C:] * decay).astype(BF16)
    for u in units:
        c, h = u
        X = jnp.concatenate([vb_all[rs[c], sls[h]], kbe_all[rs[c], sls[h]]], axis=1)
        L1 = jnp.where(level_masks[0], A[u], 0.0)
        Z[u] = jnp.concatenate([eye - L1.astype(F32), X - _dot(L1, X.astype(BF16))], axis=1)
    for lvl in range(1, levels):
        Z16 = {u: Z[u].astype(BF16) for u in units}
        cols = slice(0, C + 2 * DN_HD) if lvl < levels - 1 else slice(C, C + 2 * DN_HD)
        Y = {u: _dot(jnp.where(level_masks[lvl], A[u], 0.0), Z16[u][:, cols]).astype(BF16) for u in units}
        Z = {u: Z[u][:, cols] - _dot(Z16[u][:, :C], Y[u]) for u in units}
    S = [state_ref[h] for h in heads]
    for c in range(DN_STEP_CHUNKS):
        us = [(c, h) for h in heads]
        ws = [_dot(jnp.concatenate([Z[u][:, DN_HD:].astype(BF16), qe16[rs[c], sls[u[1]]]], axis=0),
                   S[u[1]].astype(BF16)) for u in us]
        v16 = [(Z[u][:, :DN_HD] - ws[u[1]][:C]).astype(BF16) for u in us]
        for u in us:
            h = u[1]
            G_last = G_c[u][C - 1:C, :]
            S[h] = S[h] * jnp.exp(G_last) + _dot_tn(kdec16[rs[c], sls[h]], v16[h])
        for u in us:
            h = u[1]
            oraw_ref[rs[c], sls[h]] = ws[h][C:] + _dot(aqk[u], v16[h])
    for h in heads:
        state_ref[h] = S[h]
    o = oraw_ref[...]
    ms = _dot_nt((o * o).astype(BF16), expb) * (1.0 / DN_HD)
    o = o * per_head(lax.rsqrt(ms + RMS_EPS), expb) * ng_ref[...]
    o_ref[...] = (o * _silu(z_ref[...].astype(F32))).astype(BF16)


def _deltanet(proj, ba, conv_w, alog_row, dtb_row, norm_g, batch, seq):
    n = proj.shape[0]
    C = DN_STEP_CHUNKS * DN_CHUNK
    nt = seq // C
    hb = C // SEQ_HALO

    def head_lanes(first_row):
        m = np.zeros((BA_LANES, DN_W), np.float32)
        for h in range(DN_HEADS):
            m[first_row + h, h * DN_HD:(h + 1) * DN_HD] = 1.0
        return m

    def cur(col):
        return pl.BlockSpec((C, DN_W), lambda b, i: (b * nt + i, col))

    def halo(col):
        return pl.BlockSpec((SEQ_HALO, DN_W), lambda b, i: (jnp.maximum((b * nt + i) * hb - 1, 0), col))

    return pl.pallas_call(
        _deltanet_kernel,
        grid=(batch, nt),
        in_specs=[cur(COL_DQ), cur(COL_DQ + 1), cur(COL_DQ + 2),
                  halo(COL_DQ), halo(COL_DQ + 1), halo(COL_DQ + 2),
                  cur(COL_DZ),
                  pl.BlockSpec((C, BA_LANES), lambda b, i: (b * nt + i, 0)),
                  pl.BlockSpec((3, DN_CONV, DN_W), lambda b, i: (0, 0, 0)),
                  pl.BlockSpec((1, BA_LANES), lambda b, i: (0, 0)),
                  pl.BlockSpec((1, BA_LANES), lambda b, i: (0, 0)),
                  pl.BlockSpec((1, DN_W), lambda b, i: (0, 0)),
                  pl.BlockSpec((BA_LANES, DN_W), lambda b, i: (0, 0)),
                  pl.BlockSpec((BA_LANES, DN_W), lambda b, i: (0, 0))],
        out_specs=pl.BlockSpec((C, DN_W), lambda b, i: (b * nt + i, 0)),
        out_shape=jax.ShapeDtypeStruct((n, DN_W), BF16),
        scratch_shapes=[pltpu.VMEM((DN_HEADS, DN_HD, DN_HD), F32), pltpu.VMEM((C, DN_W), F32)],
        compiler_params=_cparams("parallel", "arbitrary"),
        name="deltanet",
    )(proj, proj, proj, proj, proj, proj, proj, ba, conv_w, alog_row, dtb_row, norm_g,
      jnp.asarray(head_lanes(0), BF16), jnp.asarray(head_lanes(DN_HEADS), BF16))


def _merge_kernel(x_ref, yp_ref, ym_ref, yd_ref, g0_ref, g1_ref, g2_ref, wp_ref, wm_ref, wd_ref, wo_ref, o_ref):
    merged = (g0_ref[...].astype(F32) * _dot(yp_ref[...], wp_ref[...])
              + g1_ref[...].astype(F32) * _dot(ym_ref[...], wm_ref[...])
              + g2_ref[...].astype(F32) * _dot(yd_ref[...], wd_ref[...]))
    o_ref[...] = x_ref[...] + _dot(merged.astype(BF16), wo_ref[...])


def _merge(x, y_pool, y_moba, y_dn, proj, w_up_pool, w_up_moba, w_up_dn, w_out, tm):
    n = x.shape[0]

    def rows(width, col=0):
        return pl.BlockSpec((tm, width), lambda i: (i, col))

    def whole(shape):
        return pl.BlockSpec(shape, lambda i: (0, 0))

    return pl.pallas_call(
        _merge_kernel,
        grid=(n // tm,),
        in_specs=[rows(D_MODEL), rows(POOL_W), rows(MOBA_AUG_W), rows(DN_W),
                  rows(D_MODEL, COL_GATE0), rows(D_MODEL, COL_GATE0 + 1), rows(D_MODEL, COL_GATE0 + 2),
                  whole((POOL_W, D_MODEL)), whole((MOBA_AUG_W, D_MODEL)), whole((DN_W, D_MODEL)),
                  whole((D_MODEL, D_MODEL))],
        out_specs=rows(D_MODEL),
        out_shape=jax.ShapeDtypeStruct((n, D_MODEL), F32),
        compiler_params=_cparams("parallel"),
        name="merge",
    )(x, y_pool, y_moba, y_dn, proj, proj, proj, w_up_pool, w_up_moba, w_up_dn, w_out)


def _memkv_kernel(mem_ref, g_ref, w_ref, o_ref):
    o_ref[...] = _dot(_rms(mem_ref[...], g_ref[...]).astype(BF16), w_ref[...]).astype(BF16)


def _memkv(mem2d, g, wkv):
    m = mem2d.shape[0]
    tm = 256
    return pl.pallas_call(
        _memkv_kernel,
        grid=(m // tm,),
        in_specs=[pl.BlockSpec((tm, D_MODEL), lambda i: (i, 0)),
                  pl.BlockSpec((1, D_MODEL), lambda i: (0, 0)),
                  pl.BlockSpec((D_MODEL, 2 * XA_W), lambda i: (0, 0))],
        out_specs=pl.BlockSpec((tm, 2 * XA_W), lambda i: (i, 0)),
        out_shape=jax.ShapeDtypeStruct((m, 2 * XA_W), BF16),
        compiler_params=_cparams("parallel"),
        name="memkv",
    )(mem2d, g, wkv)


def _xattn_kernel(x_ref, g_ref, wq_ref, kv_ref, wo_ref, o_ref, *maybe_rows_ref):
    x = x_ref[...]
    q = _dot(_rms(x, g_ref[...]).astype(BF16), wq_ref[...]).astype(BF16)
    scale = XA_HD ** -0.5
    outs = []
    for h in range(XA_HEADS):
        sl = slice(h * XA_HD, (h + 1) * XA_HD)
        k = kv_ref[:, h * XA_HD:(h + 1) * XA_HD]
        v = kv_ref[:, XA_W + h * XA_HD:XA_W + (h + 1) * XA_HD]
        s = _dot_nt(q[:, sl], k) * scale
        s = s - jnp.max(s, axis=1, keepdims=True)
        p = jnp.exp(s)
        p = p / jnp.sum(p, axis=1, keepdims=True)
        outs.append(_dot(p.astype(BF16), v).astype(BF16))
    o = jnp.concatenate(outs, axis=1)
    out = x + _dot(o, wo_ref[...])
    o_ref[...] = out
    for rows_ref in maybe_rows_ref:
        for s in range(ROW_TILE):
            rows_ref[pl.ds(s, out.shape[0], stride=ROW_TILE), :] = out[:, s * 128:(s + 1) * 128]


def _xattn(x, g, wq, kv, wo, seq, mem_len, tm, with_row_tiles):
    n = x.shape[0]
    tiles_per_seq = seq // tm
    out_specs = [pl.BlockSpec((tm, D_MODEL), lambda i: (i, 0))]
    out_shape = [jax.ShapeDtypeStruct((n, D_MODEL), F32)]
    if with_row_tiles:
        out_specs.append(pl.BlockSpec((tm * ROW_TILE, 128), lambda i: (i, 0)))
        out_shape.append(jax.ShapeDtypeStruct((n * ROW_TILE, 128), F32))
    return pl.pallas_call(
        _xattn_kernel,
        grid=(n // tm,),
        in_specs=[pl.BlockSpec((tm, D_MODEL), lambda i: (i, 0)),
                  pl.BlockSpec((1, D_MODEL), lambda i: (0, 0)),
                  pl.BlockSpec((D_MODEL, XA_W), lambda i: (0, 0)),
                  pl.BlockSpec((mem_len, 2 * XA_W), lambda i: (i // tiles_per_seq, 0)),
                  pl.BlockSpec((XA_W, D_MODEL), lambda i: (0, 0))],
        out_specs=out_specs,
        out_shape=out_shape,
        compiler_params=_cparams("parallel"),
        name="xattn",
    )(x, g, wq, kv, wo)


def _swiglu_chunk(h, wg_ref, wu_ref, wd_ref):
    a = _silu(_dot(h, wg_ref[...])) * _dot(h, wu_ref[...])
    return _dot(a.astype(BF16), wd_ref[...])


def _ffn_kernel(x_ref, g_ref, wg_ref, wu_ref, wd_ref, o_ref, h_ref, acc_ref):
    j = pl.program_id(1)

    @pl.when(j == 0)
    def _():
        h_ref[...] = _rms(x_ref[...], g_ref[...]).astype(BF16)
        acc_ref[...] = x_ref[...]

    acc_ref[...] += _swiglu_chunk(h_ref[...], wg_ref, wu_ref, wd_ref)

    @pl.when(j == pl.num_programs(1) - 1)
    def _():
        o_ref[...] = acc_ref[...]


def _ffn(x, g, w_gate_up, w_down, tm, tf):
    n = x.shape[0]
    ff = w_down.shape[0]
    nf = ff // tf
    return pl.pallas_call(
        _ffn_kernel,
        grid=(n // tm, nf),
        in_specs=[pl.BlockSpec((tm, D_MODEL), lambda i, j: (i, 0)),
                  pl.BlockSpec((1, D_MODEL), lambda i, j: (0, 0)),
                  pl.BlockSpec((D_MODEL, tf), lambda i, j: (0, j)),
                  pl.BlockSpec((D_MODEL, tf), lambda i, j: (0, nf + j)),
                  pl.BlockSpec((tf, D_MODEL), lambda i, j: (j, 0))],
        out_specs=pl.BlockSpec((tm, D_MODEL), lambda i, j: (i, 0)),
        out_shape=jax.ShapeDtypeStruct((n, D_MODEL), F32),
        scratch_shapes=[pltpu.VMEM((tm, D_MODEL), BF16), pltpu.VMEM((tm, D_MODEL), F32)],
        compiler_params=_cparams("parallel", "arbitrary"),
        name="ffn",
    )(x, g, w_gate_up, w_gate_up, w_down)


def _route_kernel(x_ref, g_ref, r_ref, o_ref, *, tm):
    lane = lax.broadcasted_iota(jnp.int32, (tm, 128), 1).astype(F32)
    h = _rms(x_ref[...], g_ref[...]).astype(BF16)
    logits = jnp.where(lane < float(N_EXPERTS), _dot(h, r_ref[...]), -jnp.inf)
    m1 = jnp.max(logits, axis=1, keepdims=True)
    i1 = jnp.min(jnp.where(logits == m1, lane, 128.0), axis=1, keepdims=True)
    rest = jnp.where(lane == i1, -jnp.inf, logits)
    m2 = jnp.max(rest, axis=1, keepdims=True)
    i2 = jnp.min(jnp.where(rest == m2, lane, 128.0), axis=1, keepdims=True)
    e2 = jnp.exp(m2 - m1)
    o_ref[...] = (jnp.where(lane == 0.0, i1, 0.0) + jnp.where(lane == 1.0, i2, 0.0)
                  + jnp.where(lane == 2.0, 1.0 / (1.0 + e2), 0.0) + jnp.where(lane == 3.0, e2 / (1.0 + e2), 0.0))


def _route(x, g, router, tm):
    n = x.shape[0]
    return pl.pallas_call(
        functools.partial(_route_kernel, tm=tm),
        grid=(n // tm,),
        in_specs=[pl.BlockSpec((tm, D_MODEL), lambda i: (i, 0)),
                  pl.BlockSpec((1, D_MODEL), lambda i: (0, 0)),
                  pl.BlockSpec((D_MODEL, 128), lambda i: (0, 0))],
        out_specs=pl.BlockSpec((tm, 128), lambda i: (i, 0)),
        out_shape=jax.ShapeDtypeStruct((n, 128), F32),
        compiler_params=_cparams("parallel"),
        name="moe_route",
    )(x, g, router)


def _row_gather(idx_smem, src_hbm, dst, count, sem):
    def row_copy(r):
        s0 = pl.multiple_of(idx_smem[r] * ROW_TILE, ROW_TILE)
        d0 = pl.multiple_of(r * ROW_TILE, ROW_TILE)
        return pltpu.make_async_copy(src_hbm.at[pl.ds(s0, ROW_TILE), :], dst.at[pl.ds(d0, ROW_TILE), :], sem)

    def start_loop():
        @pl.loop(0, count, step=GATHER_UNROLL)
        def _(r0):
            for u in range(GATHER_UNROLL):
                row_copy(r0 + u).start()

    def start_inline():
        for r in range(count):
            row_copy(r).start()

    def wait_loop():
        @pl.loop(0, count, step=GATHER_UNROLL)
        def _(r0):
            for u in range(GATHER_UNROLL):
                d0 = pl.multiple_of((r0 + u) * ROW_TILE, ROW_TILE)
                pltpu.make_async_copy(src_hbm.at[pl.ds(0, ROW_TILE), :], dst.at[pl.ds(d0, ROW_TILE), :], sem).wait()

    return start_loop, start_inline, wait_loop


def _gather_ring(step, n_steps, idx_now, idx_next, idx_smem, src_hbm, buf, count, sems, compute):
    def copies(sl):
        return _row_gather(idx_smem[sl], src_hbm, buf.at[sl], count, sems.at[sl])

    @pl.when(step == 0)
    def _():
        stage = pltpu.make_async_copy(idx_now, idx_smem[0], sems.at[2])
        stage.start()
        stage.wait()
        copies(0)[0]()

    for sl in range(2):
        @pl.when(step % 2 == sl)
        def _(sl=sl):
            stage = pltpu.make_async_copy(idx_next, idx_smem[1 - sl], sems.at[2])
            stage.start()
            copies(sl)[2]()
            stage.wait()
            copies(1 - sl)[1]()
            compute(sl)

            @pl.when(step + 1 == n_steps)
            def _():
                copies(1 - sl)[2]()


def _rms_row_tiles(v, g8, rows):
    v3 = v.reshape(rows, ROW_TILE, 128)
    ms = jnp.sum(jnp.sum(v3 * v3, axis=2, keepdims=True), axis=1, keepdims=True) * (1.0 / D_MODEL)
    return (v3 * lax.rsqrt(ms + RMS_EPS) * g8[None]).reshape(rows * ROW_TILE, 128)


def _expert_kernel(te_ref, nu_ref, x_hbm, src_ref, src_next_ref, w_ref, g8_ref, wg_ref, wu_ref, wd_ref, o_ref,
                   idx0_ref, idx1_ref, xbuf_ref, x16_ref, sems, *, tm):
    t = pl.program_id(0)
    del te_ref

    @pl.when(t >= nu_ref[0])
    def _():
        o_ref[...] = jnp.zeros_like(o_ref)

    def expert_tile(slot):
        xb = xbuf_ref.at[slot]
        xb[...] = _rms_row_tiles(xb[...], g8_ref[...], tm)
        for s in range(ROW_TILE):
            x16_ref[:, s * 128:(s + 1) * 128] = xb[pl.ds(s, tm, stride=ROW_TILE), :].astype(BF16)
        y = _swiglu_chunk(x16_ref[...], wg_ref.at[0], wu_ref.at[0], wd_ref.at[0]) * w_ref[...]
        for s in range(ROW_TILE):
            o_ref[pl.ds(s, tm, stride=ROW_TILE), :] = y[:, s * 128:(s + 1) * 128]

    @pl.when(t < nu_ref[0])
    def _():
        _gather_ring(t, nu_ref[0], src_ref.at[0, 0], src_next_ref.at[0, 0], (idx0_ref, idx1_ref), x_hbm, xbuf_ref,
                     tm, sems, expert_tile)


def _experts(x_rows, src, w_sorted, tile_expert, n_used, g8, w_gate_up, w_down, tm):
    nt = src.shape[0]
    ff = w_down.shape[1]
    grid_spec = pltpu.PrefetchScalarGridSpec(
        num_scalar_prefetch=2,
        grid=(nt,),
        in_specs=[pl.BlockSpec(memory_space=pl.ANY),
                  pl.BlockSpec((1, 1, tm), lambda t, te, nu: (t, 0, 0)),
                  pl.BlockSpec((1, 1, tm), lambda t, te, nu: (jnp.minimum(t + 1, nt - 1), 0, 0)),
                  pl.BlockSpec((tm, 1), lambda t, te, nu: (t, 0)),
                  pl.BlockSpec((ROW_TILE, 128), lambda t, te, nu: (0, 0)),
                  pl.BlockSpec((1, D_MODEL, ff), lambda t, te, nu: (te[t], 0, 0)),
                  pl.BlockSpec((1, D_MODEL, ff), lambda t, te, nu: (te[t], 0, 1)),
                  pl.BlockSpec((1, ff, D_MODEL), lambda t, te, nu: (te[t], 0, 0))],
        out_specs=pl.BlockSpec((tm * ROW_TILE, 128), lambda t, te, nu: (t, 0)),
        scratch_shapes=[pltpu.SMEM((tm,), jnp.int32), pltpu.SMEM((tm,), jnp.int32),
                        pltpu.VMEM((2, tm * ROW_TILE, 128), F32),
                        pltpu.VMEM((tm, D_MODEL), BF16), pltpu.SemaphoreType.DMA((3,))])
    return pl.pallas_call(
        functools.partial(_expert_kernel, tm=tm),
        grid_spec=grid_spec,
        out_shape=jax.ShapeDtypeStruct((nt * tm * ROW_TILE, 128), F32),
        compiler_params=_cparams("arbitrary"),
        name="moe_experts",
    )(tile_expert, n_used, x_rows, src, src, w_sorted, g8, w_gate_up, w_gate_up, w_down)


def _combine_kernel(x_ref, y_hbm, dest_ref, dest_next_ref, fg_ref, o_ref, idx0_ref, idx1_ref, ybuf_ref, sems, *, tt):
    def combine_tile(slot):
        yb = ybuf_ref.at[slot]
        n8 = tt * ROW_TILE
        for s in range(ROW_TILE):
            sl = slice(s * 128, (s + 1) * 128)
            o_ref[:, sl] = (x_ref[:, sl] + yb[pl.ds(s, tt, stride=ROW_TILE), :]
                            + yb[pl.ds(n8 + s, tt, stride=ROW_TILE), :])
        o_ref[...] = _rms(o_ref[...], fg_ref[...])

    _gather_ring(pl.program_id(0), pl.num_programs(0), dest_ref.at[0, 0], dest_next_ref.at[0, 0],
                 (idx0_ref, idx1_ref), y_hbm, ybuf_ref, MOE_TOPK * tt, sems, combine_tile)


def _combine_final(x, y_rows, dest, fg, tt):
    nt = dest.shape[0]
    return pl.pallas_call(
        functools.partial(_combine_kernel, tt=tt),
        grid=(nt,),
        in_specs=[pl.BlockSpec((tt, D_MODEL), lambda i: (i, 0)),
                  pl.BlockSpec(memory_space=pl.ANY),
                  pl.BlockSpec((1, 1, MOE_TOPK * tt), lambda i: (i, 0, 0)),
                  pl.BlockSpec((1, 1, MOE_TOPK * tt), lambda i: (jnp.minimum(i + 1, nt - 1), 0, 0)),
                  pl.BlockSpec((1, D_MODEL), lambda i: (0, 0))],
        out_specs=pl.BlockSpec((tt, D_MODEL), lambda i: (i, 0)),
        out_shape=jax.ShapeDtypeStruct(x.shape, F32),
        scratch_shapes=[pltpu.SMEM((MOE_TOPK * tt,), jnp.int32), pltpu.SMEM((MOE_TOPK * tt,), jnp.int32),
                        pltpu.VMEM((2, MOE_TOPK * tt * ROW_TILE, 128), F32), pltpu.SemaphoreType.DMA((3,))],
        compiler_params=_cparams("arbitrary"),
        name="moe_combine",
    )(x, y_rows, dest, dest, fg)


def _moe_sparse_final(xf, x_rows, g, router, w_gate_up, w_down, final_g, tm, tt):
    n = xf.shape[0]
    nk = MOE_TOPK * n
    route = _route(xf, g, router, tm)
    e_flat = route[:, :MOE_TOPK].astype(jnp.int32).reshape(nk)
    w_flat = route[:, MOE_TOPK:2 * MOE_TOPK].reshape(nk)
    onehot = (e_flat[:, None] == jnp.arange(N_EXPERTS, dtype=jnp.int32)[None, :]).astype(jnp.int32)
    csum = jnp.cumsum(onehot, axis=0)
    cnt = csum[-1]
    rank = jnp.take_along_axis(csum, e_flat[:, None], axis=1)[:, 0] - 1
    padded = ((cnt + tm - 1) // tm) * tm
    ends = jnp.cumsum(padded)
    off = ends - padded
    start = jnp.cumsum(cnt) - cnt
    dest = off[e_flat] + rank
    nt = nk // tm + N_EXPERTS
    slot = jnp.arange(nt * tm, dtype=jnp.int32)
    e_slot = jnp.minimum(jnp.searchsorted(ends, slot, side="right"), N_EXPERTS - 1).astype(jnp.int32)
    loc = slot - off[e_slot]
    valid = loc < cnt[e_slot]
    order = jnp.argsort(e_flat, stable=True).astype(jnp.int32)
    pair = order[jnp.clip(start[e_slot] + loc, 0, nk - 1)]
    src = jnp.where(valid, pair // MOE_TOPK, 0).reshape(nt, 1, tm)
    w_sorted = jnp.where(valid, w_flat[pair], 0.0).reshape(nt * tm, 1)
    tile_expert = e_slot[::tm]
    n_used = (ends[-1:] // tm).astype(jnp.int32)

    y_rows = _experts(x_rows, src, w_sorted, tile_expert, n_used, g.reshape(ROW_TILE, 128), w_gate_up, w_down, tm)
    dest = dest.reshape(n // tt, tt, MOE_TOPK).transpose(0, 2, 1).reshape(n // tt, 1, MOE_TOPK * tt)
    return _combine_final(xf, y_rows, dest, final_g[None, :], tt)


def _blockdiag(pool_w):
    g, c, _ = pool_w.shape
    out = jnp.zeros((g * c, g * c), pool_w.dtype)
    for gi in range(g):
        out = out.at[gi * c:(gi + 1) * c, gi * c:(gi + 1) * c].set(pool_w[gi])
    return out


def _lane_row(v, offset):
    return jnp.zeros((1, BA_LANES), F32).at[0, offset:offset + v.shape[0]].set(v.astype(F32))


def kernel(x, mem, positions, mix_norm_g, w_in, pool_w, pool_scale, dn_conv_w, dn_a_log, dn_dt_bias, dn_norm_g,
           w_up_pool, w_up_moba, w_up_dn, w_out, xa_norm_g, mem_norm_g, xa_wq, xa_wkv, xa_wo, ffn_norm_g,
           dense_w_gate_up, dense_w_down, moe_router, moe_w_gate_up, moe_w_down, final_norm_g):
    batch, seq, d = x.shape
    depth = w_in.shape[0]
    mem_len = mem.shape[1]
    n = batch * seq
    nb = seq // MOBA_BLOCK
    assert d == D_MODEL and seq % MOBA_BLOCK == 0 and nb <= MOBA_HD and nb % (2 * MOBA_GROUP) == 0 and depth == 2
    tm = min(512, seq)

    xf = x.reshape(n, d)
    mem2d = mem.reshape(batch * mem_len, d)
    cos_t, sin_t = _rope_tables(positions.reshape(n, 1).astype(F32), tm)
    off_ba = POOL_W + 3 * MOBA_W + 3 * DN_W

    for layer in range(depth):
        wl = w_in[layer]
        w_main = jnp.concatenate([wl[:, :off_ba], wl[:, off_ba + 2 * DN_HEADS:]], axis=1).astype(BF16)
        w_ba = jnp.pad(wl[:, off_ba:off_ba + 2 * DN_HEADS], ((0, 0), (0, BA_LANES - 2 * DN_HEADS))).astype(BF16)
        proj, ba = _inproj(xf, mix_norm_g[layer][None, :], w_main, w_ba, min(2048, seq))

        y_pool = _pool(proj, _blockdiag(pool_w[layer]).astype(BF16), pool_scale[layer][None, :], batch, seq, tm)
        mq, mk, mv, kbar = _moba_prep(proj, cos_t, sin_t, nb)
        kbar = jnp.pad(kbar.reshape(batch, nb, MOBA_W), ((0, 0), (0, MOBA_KBAR_ROWS - nb), (0, 0)))
        y_moba = _moba(mq, mk, mv, kbar, batch, seq)
        w_moba = jnp.pad(w_up_moba[layer].reshape(MOBA_HEADS, MOBA_HD, d), ((0, 0), (0, MOBA_HD), (0, 0)))
        w_moba = w_moba.reshape(MOBA_AUG_W, d).astype(BF16)
        conv_w = dn_conv_w[layer].reshape(DN_CONV, 3, DN_W).transpose(1, 0, 2)
        y_dn = _deltanet(proj, ba, conv_w, _lane_row(dn_a_log[layer], DN_HEADS),
                         _lane_row(dn_dt_bias[layer], DN_HEADS), jnp.tile(dn_norm_g[layer], DN_HEADS)[None, :], batch, seq)
        xf = _merge(xf, y_pool, y_moba, y_dn, proj, w_up_pool[layer].astype(BF16), w_moba,
                    w_up_dn[layer].astype(BF16), w_out[layer].astype(BF16), tm)

        kv = _memkv(mem2d, mem_norm_g[layer][None, :], xa_wkv[layer].astype(BF16))
        routed = layer % 2 == 1
        xf, *x_rows = _xattn(xf, xa_norm_g[layer][None, :], xa_wq[layer].astype(BF16), kv,
                             xa_wo[layer].astype(BF16), seq, mem_len, tm, routed)

        if layer % 2 == 0:
            xf = _ffn(xf, ffn_norm_g[layer][None, :], dense_w_gate_up[layer // 2].astype(BF16),
                      dense_w_down[layer // 2].astype(BF16), tm, dense_w_down.shape[1] // 2)
        else:
            router = jnp.pad(moe_router[layer // 2], ((0, 0), (0, 128 - N_EXPERTS))).astype(BF16)
            xf = _moe_sparse_final(xf, x_rows[0], ffn_norm_g[layer][None, :], router, moe_w_gate_up[layer // 2].astype(BF16),
                                   moe_w_down[layer // 2].astype(BF16), final_norm_g, tm, min(256, seq))
    return xf.reshape(batch, seq, d)
```

```python
import functools

import numpy as np
import jax
import jax.numpy as jnp
from jax import lax
from jax.experimental import pallas as pl
from jax.experimental.pallas import tpu as pltpu

F32 = jnp.float32
BF16 = jnp.bfloat16

RMS_EPS = 1e-6
D_MODEL = 1024
POOL_W = 256
POOL_GROUP_W = 64
POOL_WINDOWS = (2, 4, 8, 16)
SEQ_HALO = 16
MOBA_HEADS = 4
MOBA_HD = 64
MOBA_W = 256
MOBA_AUG_W = 2 * MOBA_W
MOBA_BLOCK = 256
MOBA_TOPK = 3
MOBA_KBAR_ROWS = 128
MOBA_GROUP = 2
ROPE_THETA = 500000.0
ROPE_DIMS = 16
DN_HEADS = 8
DN_HD = 64
DN_W = 512
DN_CONV = 4
DN_CHUNK = 128
DN_STEP_CHUNKS = 2
XA_HEADS = 4
XA_HD = 128
XA_W = 512
N_EXPERTS = 8
MOE_TOPK = 2
ROW_TILE = 8
GATHER_UNROLL = 16
COL_POOL, COL_MQ, COL_MK, COL_MV = 0, 1, 2, 3
COL_DQ, COL_DZ = 2, 5
COL_GATE0 = 3
PROJ_COLS = 6144
PROJ_TN = 1024
PROJ_ID_BLOCKS = 3
BA_LANES = 128
VMEM_LIMIT = 56 * 1024 * 1024
NEG_BIG = -1e30
LOG2_E = 1.4426950408889634


def _cparams(*sem):
    return pltpu.CompilerParams(dimension_semantics=sem, vmem_limit_bytes=VMEM_LIMIT)


def _rms(x, g):
    ms = jnp.mean(x * x, axis=-1, keepdims=True)
    return x * lax.rsqrt(ms + RMS_EPS) * g


def _silu(x):
    return x * jax.nn.sigmoid(x)


def _dot(a, b):
    return jnp.dot(a, b, preferred_element_type=F32)


def _dot_nt(a, b):
    return lax.dot_general(a, b, (((1,), (1,)), ((), ())), preferred_element_type=F32)


def _dot_tn(a, b):
    return lax.dot_general(a, b, (((0,), (0,)), ((), ())), preferred_element_type=F32)


def _inproj_kernel(x_ref, g_ref, w_ref, wba_ref, proj_ref, ba_ref, h_ref):
    j = pl.program_id(1)

    @pl.when(j == 0)
    def _():
        h = _rms(x_ref[...], g_ref[...]).astype(BF16)
        h_ref[...] = h
        ba_ref[...] = _dot(h, wba_ref[...])

    acc = _dot(h_ref[...], w_ref[...])

    @pl.when(j < PROJ_ID_BLOCKS)
    def _():
        proj_ref[...] = acc.astype(BF16)

    @pl.when(j >= PROJ_ID_BLOCKS)
    def _():
        proj_ref[...] = jax.nn.sigmoid(acc).astype(BF16)


def _inproj(x, g, w_main, w_ba, tm):
    n = x.shape[0]
    return pl.pallas_call(
        _inproj_kernel,
        grid=(n // tm, PROJ_COLS // PROJ_TN),
        in_specs=[
            pl.BlockSpec((tm, D_MODEL), lambda i, j: (i, 0)),
            pl.BlockSpec((1, D_MODEL), lambda i, j: (0, 0)),
            pl.BlockSpec((D_MODEL, PROJ_TN), lambda i, j: (0, j)),
            pl.BlockSpec((D_MODEL, BA_LANES), lambda i, j: (0, 0)),
        ],
        out_specs=[
            pl.BlockSpec((tm, PROJ_TN), lambda i, j: (i, j)),
            pl.BlockSpec((tm, BA_LANES), lambda i, j: (i, 0)),
        ],
        out_shape=[jax.ShapeDtypeStruct((n, PROJ_COLS), BF16),
                   jax.ShapeDtypeStruct((n, BA_LANES), F32)],
        scratch_shapes=[pltpu.VMEM((tm, D_MODEL), BF16)],
        compiler_params=_cparams("parallel", "arbitrary"),
        name="inproj",
    )(x, g, w_main, w_ba)


def _rope_table_kernel(pos_ref, freq_ref, sign_ref, c_ref, s_ref):
    ang = pos_ref[...] * freq_ref[...]
    rot = sign_ref[...]
    c_ref[...] = jnp.where(rot != 0.0, jnp.cos(ang), 1.0)
    s_ref[...] = jnp.sin(ang) * rot


def _rope_tables(pos_f32, tm):
    n = pos_f32.shape[0]
    half = ROPE_DIMS // 2
    inv_freq = np.power(ROPE_THETA, -np.arange(half, dtype=np.float32) * 2.0 / ROPE_DIMS).astype(np.float32)
    d = np.arange(128) % MOBA_HD
    freq = np.where(d < ROPE_DIMS, inv_freq[d % half], 0.0).astype(np.float32)[None, :]
    sign = np.where(d < half, -1.0, np.where(d < ROPE_DIMS, 1.0, 0.0)).astype(np.float32)[None, :]
    return pl.pallas_call(
        _rope_table_kernel,
        grid=(n // tm,),
        in_specs=[pl.BlockSpec((tm, 1), lambda i: (i, 0)),
                  pl.BlockSpec((1, 128), lambda i: (0, 0)),
                  pl.BlockSpec((1, 128), lambda i: (0, 0))],
        out_specs=[pl.BlockSpec((tm, 128), lambda i: (i, 0)),
                   pl.BlockSpec((tm, 128), lambda i: (i, 0))],
        out_shape=[jax.ShapeDtypeStruct((n, 128), F32), jax.ShapeDtypeStruct((n, 128), F32)],
        compiler_params=_cparams("parallel"),
        name="rope_tables",
    )(pos_f32, jnp.asarray(freq), jnp.asarray(sign))


def _moba_prep_kernel(q_ref, k_ref, v_ref, c_ref, s_ref, qo_ref, ko_ref, vo_ref, kbar_ref, *, nb):
    c = jnp.concatenate([c_ref[...], c_ref[...]], axis=1)
    s = jnp.concatenate([s_ref[...], s_ref[...]], axis=1)
    lane = lax.broadcasted_iota(jnp.int32, (MOBA_BLOCK, MOBA_W), 1)
    first_half = (lane % MOBA_HD) < (ROPE_DIMS // 2)

    def rope(x):
        partner = jnp.where(first_half,
                            pltpu.roll(x, MOBA_W - ROPE_DIMS // 2, 1),
                            pltpu.roll(x, ROPE_DIMS // 2, 1))
        return x * c + partner * s

    qo_ref[...] = (rope(q_ref[...].astype(F32)) * (MOBA_HD ** -0.5 * LOG2_E)).astype(BF16)
    k = rope(k_ref[...].astype(F32))
    kbar_ref[0] = jnp.mean(k, axis=0, keepdims=True)
    k16 = k.astype(BF16)
    blk_lane = lax.broadcasted_iota(jnp.int32, (MOBA_BLOCK, MOBA_HD), 1)
    onehot = jnp.where(blk_lane == pl.program_id(0) % nb, 1.0, 0.0).astype(BF16)
    k_parts = []
    for h in range(MOBA_HEADS):
        k_parts += [k16[:, h * MOBA_HD:(h + 1) * MOBA_HD], onehot]
    ko_ref[...] = jnp.concatenate(k_parts, axis=1)
    vo_ref[...] = v_ref[...].astype(F32).T.astype(BF16)


def _moba_prep(proj, cos_t, sin_t, nb):
    n = proj.shape[0]
    nblk = n // MOBA_BLOCK
    return pl.pallas_call(
        functools.partial(_moba_prep_kernel, nb=nb),
        grid=(nblk,),
        in_specs=[pl.BlockSpec((MOBA_BLOCK, MOBA_W), lambda r: (r, COL_MQ)),
                  pl.BlockSpec((MOBA_BLOCK, MOBA_W), lambda r: (r, COL_MK)),
                  pl.BlockSpec((MOBA_BLOCK, MOBA_W), lambda r: (r, COL_MV)),
                  pl.BlockSpec((MOBA_BLOCK, 128), lambda r: (r, 0)),
                  pl.BlockSpec((MOBA_BLOCK, 128), lambda r: (r, 0))],
        out_specs=[pl.BlockSpec((MOBA_BLOCK, MOBA_W), lambda r: (r, 0)),
                   pl.BlockSpec((MOBA_BLOCK, MOBA_AUG_W), lambda r: (r, 0)),
                   pl.BlockSpec((MOBA_W, MOBA_BLOCK), lambda r: (r, 0)),
                   pl.BlockSpec((1, 1, MOBA_W), lambda r: (r, 0, 0))],
        out_shape=[jax.ShapeDtypeStruct((n, MOBA_W), BF16),
                   jax.ShapeDtypeStruct((n, MOBA_AUG_W), BF16),
                   jax.ShapeDtypeStruct((nblk * MOBA_W, MOBA_BLOCK), BF16),
                   jax.ShapeDtypeStruct((nblk, 1, MOBA_W), F32)],
        compiler_params=_cparams("parallel"),
        name="moba_prep",
    )(proj, proj, proj, cos_t, sin_t)


def _moba_kernel(q_ref, k_ref, vt_ref, kbar_ref, o_ref, qa_ref, m_ref, ls_ref, acc_ref, s_ref):
    i = pl.program_id(1)
    heads = range(MOBA_HEADS)
    blk = lax.broadcasted_iota(jnp.int32, (MOBA_KBAR_ROWS, MOBA_BLOCK), 0).astype(F32)
    row = lax.broadcasted_iota(jnp.int32, (MOBA_BLOCK, MOBA_BLOCK), 0)
    col = lax.broadcasted_iota(jnp.int32, (MOBA_BLOCK, MOBA_BLOCK), 1)
    causal = row <= col
    i_f = i.astype(F32)
    own = pl.multiple_of(i * MOBA_BLOCK, MOBA_BLOCK)
    hs = [slice(h * 2 * MOBA_HD, (h + 1) * 2 * MOBA_HD) for h in heads]

    for h in heads:
        sl = slice(h * MOBA_HD, (h + 1) * MOBA_HD)
        qh = q_ref[:, sl]
        gate = _dot_nt(kbar_ref[0, :, sl].astype(BF16), qh)
        gate = jnp.where(blk < i_f, gate, -jnp.inf)
        keep = jnp.zeros(gate.shape, jnp.bool_)
        for _ in range(MOBA_TOPK):
            m = jnp.max(gate, axis=0, keepdims=True)
            idx = jnp.min(jnp.where(gate == m, blk, float(MOBA_KBAR_ROWS)), axis=0, keepdims=True)
            pick = (blk == idx) & (m > -jnp.inf)
            keep = keep | pick
            gate = jnp.where(pick, -jnp.inf, gate)
        bias = jnp.where(keep, 0.0, NEG_BIG).T[:, :MOBA_HD]
        qa_ref[h] = jnp.concatenate([qh, bias.astype(BF16)], axis=1)
        qa_ref[MOBA_HEADS + h] = jnp.concatenate([qh, jnp.zeros_like(qh)], axis=1)

    groups8 = MOBA_BLOCK // ROW_TILE

    def scores(h, start):
        return _dot_nt(k_ref[pl.ds(start, MOBA_BLOCK), hs[h]], qa_ref[h])

    def own_scores(h):
        s = _dot_nt(k_ref[pl.ds(own, MOBA_BLOCK), hs[h]], qa_ref[MOBA_HEADS + h])
        return jnp.where(causal, s, NEG_BIG)

    def rows8(x):
        return x.reshape(groups8, ROW_TILE, MOBA_BLOCK)

    def all8(x, op):
        return jnp.broadcast_to(op(x, axis=0, keepdims=True), x.shape)

    def values_t(h, start):
        return vt_ref[pl.ds(start + h * MOBA_HD, MOBA_HD), :]

    n_pairs = (i + (2 * MOBA_GROUP - 1)) // (2 * MOBA_GROUP)

    def group_blocks(g):
        return [pl.multiple_of((g * MOBA_GROUP + t) * MOBA_BLOCK, MOBA_BLOCK) for t in range(MOBA_GROUP)]

    for h in heads:
        s3 = rows8(own_scores(h))
        m = all8(jnp.max(s3, axis=0), jnp.max)
        p3 = jnp.exp2(s3 - m[None])
        ls_ref[h] = jnp.sum(p3, axis=0)
        acc_ref[h] = _dot(values_t(h, own), p3.reshape(MOBA_BLOCK, MOBA_BLOCK).astype(BF16))
        m_ref[h] = m

    def score_group(g, buf):
        gmax = [None] * MOBA_HEADS
        for t, start in enumerate(group_blocks(g)):
            for h in heads:
                s = scores(h, start)
                s_ref[buf, t, h] = s
                smax = jnp.max(rows8(s), axis=0)
                gmax[h] = smax if gmax[h] is None else jnp.maximum(gmax[h], smax)
        return gmax

    def rescale(gmax):
        for h in heads:
            m_old = m_ref[h]
            m_new = jnp.maximum(m_old, all8(gmax[h], jnp.max))
            alpha = jnp.exp2(m_old - m_new)
            ls_ref[h] = ls_ref[h] * alpha
            acc = acc_ref[h].reshape(MOBA_HD // ROW_TILE, ROW_TILE, MOBA_BLOCK) * alpha[None]
            acc_ref[h] = acc.reshape(MOBA_HD, MOBA_BLOCK)
            m_ref[h] = m_new

    def accumulate_group(g, buf):
        for t, start in enumerate(group_blocks(g)):
            for h in heads:
                p3 = jnp.exp2(rows8(s_ref[buf, t, h]) - m_ref[h][None])
                ls_ref[h] += jnp.sum(p3, axis=0)
                acc_ref[h] += _dot(values_t(h, start), p3.reshape(MOBA_BLOCK, MOBA_BLOCK).astype(BF16))

    @pl.when(n_pairs > 0)
    def _():
        rescale(score_group(0, 0))

    @pl.loop(0, n_pairs)
    def _(p):
        gmax = score_group(2 * p + 1, 1)
        accumulate_group(2 * p, 0)
        rescale(gmax)

        @pl.when(p + 1 < n_pairs)
        def _():
            gmax = score_group(2 * p + 2, 0)
            accumulate_group(2 * p + 1, 1)
            rescale(gmax)

        @pl.when(p + 1 == n_pairs)
        def _():
            accumulate_group(2 * p + 1, 1)

    for h in heads:
        o_t = acc_ref[h] / jnp.sum(ls_ref[h], axis=0, keepdims=True)
        o_ref[:, hs[h]] = jnp.concatenate([o_t, jnp.zeros_like(o_t)], axis=0).T.astype(BF16)


def _moba(q, k_aug, v_aug, kbar, batch, seq):
    n = q.shape[0]
    nb = seq // MOBA_BLOCK
    return pl.pallas_call(
        _moba_kernel,
        grid=(batch, nb),
        in_specs=[pl.BlockSpec((MOBA_BLOCK, MOBA_W), lambda b, i: (b * nb + i, 0)),
                  pl.BlockSpec((seq, MOBA_AUG_W), lambda b, i: (b, 0)),
                  pl.BlockSpec((seq, MOBA_W), lambda b, i: (b, 0)),
                  pl.BlockSpec((1, MOBA_KBAR_ROWS, MOBA_W), lambda b, i: (b, 0, 0))],
        out_specs=pl.BlockSpec((MOBA_BLOCK, MOBA_AUG_W), lambda b, i: (b * nb + i, 0)),
        out_shape=jax.ShapeDtypeStruct((n, MOBA_AUG_W), BF16),
        scratch_shapes=[pltpu.VMEM((2 * MOBA_HEADS, MOBA_BLOCK, 2 * MOBA_HD), BF16)]
        + [pltpu.VMEM((MOBA_HEADS, ROW_TILE, MOBA_BLOCK), F32)] * 2
        + [pltpu.VMEM((MOBA_HEADS, MOBA_HD, MOBA_BLOCK), F32)]
        + [pltpu.VMEM((2, MOBA_GROUP, MOBA_HEADS, MOBA_BLOCK, MOBA_BLOCK), F32)],
        compiler_params=_cparams("parallel", "arbitrary"),
        name="moba",
    )(q, k_aug, v_aug, kbar)


def _pool_kernel(p_ref, halo_ref, w_ref, scale_ref, o_ref, *, ts):
    i = pl.program_id(1)
    halo = jnp.where(i == 0, 0.0, halo_ref[...].astype(F32))
    p = p_ref[...].astype(F32)
    cur = jnp.concatenate([halo, p], axis=0)
    lane = lax.broadcasted_iota(jnp.int32, (ts, POOL_W), 1)
    t1 = (lax.broadcasted_iota(jnp.int32, (ts, POOL_W), 0) + i * ts + 1).astype(F32)
    total = jnp.zeros((ts, POOL_W), F32)
    count = jnp.ones((ts, POOL_W), F32)
    span = 1
    for gi, w in enumerate(POOL_WINDOWS):
        while span < w:
            cur = cur + pltpu.roll(cur, span, 0)
            span *= 2
        in_group = (lane >= gi * POOL_GROUP_W) & (lane < (gi + 1) * POOL_GROUP_W)
        total = jnp.where(in_group, cur[SEQ_HALO:], total)
        count = jnp.where(in_group, jnp.minimum(t1, float(w)), count)
    pooled = total / count - p
    o_ref[...] = (_dot(pooled.astype(BF16), w_ref[...]) * scale_ref[...]).astype(BF16)


def _pool(proj, w_blockdiag, scale, batch, seq, ts):
    n = proj.shape[0]
    nt = seq // ts
    hb = ts // SEQ_HALO
    return pl.pallas_call(
        functools.partial(_pool_kernel, ts=ts),
        grid=(batch, nt),
        in_specs=[pl.BlockSpec((ts, POOL_W), lambda b, i: (b * nt + i, COL_POOL)),
                  pl.BlockSpec((SEQ_HALO, POOL_W), lambda b, i: (jnp.maximum((b * nt + i) * hb - 1, 0), COL_POOL)),
                  pl.BlockSpec((POOL_W, POOL_W), lambda b, i: (0, 0)),
                  pl.BlockSpec((1, POOL_W), lambda b, i: (0, 0))],
        out_specs=pl.BlockSpec((ts, POOL_W), lambda b, i: (b * nt + i, 0)),
        out_shape=jax.ShapeDtypeStruct((n, POOL_W), BF16),
        compiler_params=_cparams("parallel", "parallel"),
        name="pool",
    )(proj, proj, w_blockdiag, scale)


def _deltanet_kernel(q_ref, k_ref, v_ref, qh_ref, kh_ref, vh_ref, z_ref, ba_ref, cw_ref, alog_ref, dtb_ref,
                     ng_ref, expb_ref, expg_ref, o_ref, state_ref, oraw_ref):
    i = pl.program_id(1)
    C = DN_CHUNK
    TS = DN_STEP_CHUNKS * C

    @pl.when(i == 0)
    def _():
        state_ref[...] = jnp.zeros_like(state_ref)

    def conv(x_ref, halo_ref, w):
        halo = jnp.where(i == 0, 0.0, halo_ref[...].astype(F32))
        ext = jnp.concatenate([halo, x_ref[...].astype(F32)], axis=0)
        y = ext * w[DN_CONV - 1:DN_CONV]
        for lag in range(1, DN_CONV):
            y = y + pltpu.roll(ext, lag, 0) * w[DN_CONV - 1 - lag:DN_CONV - lag]
        return _silu(y[SEQ_HALO:])

    cw = cw_ref[...]
    qc = conv(q_ref, qh_ref, cw[0])
    kc = conv(k_ref, kh_ref, cw[1])
    vc = conv(v_ref, vh_ref, cw[2])

    ba = ba_ref[...]
    beta = jax.nn.sigmoid(ba)
    g = -jnp.exp(alog_ref[...]) * jax.nn.softplus(ba + dtb_ref[...])
    rows = lax.broadcasted_iota(jnp.int32, (TS, BA_LANES), 0) & (C - 1)
    G = g
    span = 1
    while span < C:
        G = G + jnp.where(rows >= span, pltpu.roll(G, span, 0), 0.0)
        span *= 2
    GT = G.T
    r_i = lax.broadcasted_iota(jnp.int32, (C, C), 0)
    c_i = lax.broadcasted_iota(jnp.int32, (C, C), 1)
    tril = c_i <= r_i
    eye = (c_i == r_i).astype(F32)
    levels = C.bit_length() - 1
    level_masks = [((r_i >> k) & 1 == 1) & ((c_i >> k) == (r_i >> k) - 1) for k in range(levels)]
    expb = expb_ref[...]
    expg = expg_ref[...]

    def per_head(x, e):
        hi = x.astype(BF16)
        lo = (x - hi.astype(F32)).astype(BF16)
        return _dot(hi, e) + _dot(lo, e)

    def inv_norm(x):
        ss = _dot_nt((x * x).astype(BF16), expb)
        return per_head(lax.rsqrt(ss + RMS_EPS), expb)

    qn_all = qc * (inv_norm(qc) * (DN_HD ** -0.5))
    kn_all = kc * inv_norm(kc)
    beta_all = per_head(beta, expb)
    eG_all = per_head(jnp.exp(G), expg)
    G_last_rows = jnp.concatenate(
        [jnp.broadcast_to(G[(c + 1) * C - 1:(c + 1) * C, :], (C, BA_LANES)) for c in range(DN_STEP_CHUNKS)], axis=0)
    kb_all = kn_all * beta_all
    vb_all = vc * beta_all
    kbe_all = kb_all * eG_all
    qn16 = qn_all.astype(BF16)
    kn16 = kn_all.astype(BF16)
    kb16 = kb_all.astype(BF16)
    qe16 = (qn_all * eG_all).astype(BF16)
    kdec16 = (kn_all * per_head(jnp.exp(G_last_rows - G), expg)).astype(BF16)

    heads = range(DN_HEADS)
    units = [(c, h) for c in range(DN_STEP_CHUNKS) for h in heads]
    rs = [slice(c * C, (c + 1) * C) for c in range(DN_STEP_CHUNKS)]
    sls = [slice(h * DN_HD, (h + 1) * DN_HD) for h in heads]
    G_c, A, aqk, Z = {}, {}, {}, {}
    for u in units:
        c, h = u
        G_c[u] = G[rs[c], DN_HEADS + h:DN_HEADS + h + 1]
        gram = _dot_nt(jnp.concatenate([kb16[rs[c], sls[h]], qn16[rs[c], sls[h]]], axis=0), kn16[rs[c], sls[h]])
        G_r = GT[DN_HEADS + h:DN_HEADS + h + 1, rs[c]]
        decay = jnp.exp(jnp.where(tril, G_c[u] - G_r, -jnp.inf))
        A[u] = (gram[:C] * decay).astype(BF16)
        aqk[u] = (gram[C:] * decay).astype(BF16)
    for u in units:
        c, h = u
        X = jnp.concatenate([vb_all[rs[c], sls[h]], kbe_all[rs[c], sls[h]]], axis=1)
        L1 = jnp.where(level_masks[0], A[u], 0.0)
        Z[u] = jnp.concatenate([eye - L1.astype(F32), X - _dot(L1, X.astype(BF16))], axis=1)
    for lvl in range(1, levels):
        Z16 = {u: Z[u].astype(BF16) for u in units}
        cols = slice(0, C + 2 * DN_HD) if lvl < levels - 1 else slice(C, C + 2 * DN_HD)
        Y = {u: _dot(jnp.where(level_masks[lvl], A[u], 0.0), Z16[u][:, cols]).astype(BF16) for u in units}
        Z = {u: Z[u][:, cols] - _dot(Z16[u][:, :C], Y[u]) for u in units}
    S = [state_ref[h] for h in heads]
    for c in range(DN_STEP_CHUNKS):
        us = [(c, h) for h in heads]
        ws = [_dot(jnp.concatenate([Z[u][:, DN_HD:].astype(BF16), qe16[rs[c], sls[u[1]]]], axis=0),
                   S[u[1]].astype(BF16)) for u in us]
        v16 = [(Z[u][:, :DN_HD] - ws[u[1]][:C]).astype(BF16) for u in us]
        for u in us:
            h = u[1]
            G_last = G_c[u][C - 1:C, :]
            S[h] = S[h] * jnp.exp(G_last) + _dot_tn(kdec16[rs[c], sls[h]], v16[h])
        for u in us:
            h = u[1]
            oraw_ref[rs[c], sls[h]] = ws[h][C:] + _dot(aqk[u], v16[h])
    for h in heads:
        state_ref[h] = S[h]
    o = oraw_ref[...]
    ms = _dot_nt((o * o).astype(BF16), expb) * (1.0 / DN_HD)
    o = o * per_head(lax.rsqrt(ms + RMS_EPS), expb) * ng_ref[...]
    o_ref[...] = (o * _silu(z_ref[...].astype(F32))).astype(BF16)


def _deltanet(proj, ba, conv_w, alog_row, dtb_row, norm_g, batch, seq):
    n = proj.shape[0]
    C = DN_STEP_CHUNKS * DN_CHUNK
    nt = seq // C
    hb = C // SEQ_HALO

    def head_lanes(first_row):
        m = np.zeros((BA_LANES, DN_W), np.float32)
        for h in range(DN_HEADS):
            m[first_row + h, h * DN_HD:(h + 1) * DN_HD] = 1.0
        return m

    def cur(col):
        return pl.BlockSpec((C, DN_W), lambda b, i: (b * nt + i, col))

    def halo(col):
        return pl.BlockSpec((SEQ_HALO, DN_W), lambda b, i: (jnp.maximum((b * nt + i) * hb - 1, 0), col))

    return pl.pallas_call(
        _deltanet_kernel,
        grid=(batch, nt),
        in_specs=[cur(COL_DQ), cur(COL_DQ + 1), cur(COL_DQ + 2),
                  halo(COL_DQ), halo(COL_DQ + 1), halo(COL_DQ + 2),
                  cur(COL_DZ),
                  pl.BlockSpec((C, BA_LANES), lambda b, i: (b * nt + i, 0)),
                  pl.BlockSpec((3, DN_CONV, DN_W), lambda b, i: (0, 0, 0)),
                  pl.BlockSpec((1, BA_LANES), lambda b, i: (0, 0)),
                  pl.BlockSpec((1, BA_LANES), lambda b, i: (0, 0)),
                  pl.BlockSpec((1, DN_W), lambda b, i: (0, 0)),
                  pl.BlockSpec((BA_LANES, DN_W), lambda b, i: (0, 0)),
                  pl.BlockSpec((BA_LANES, DN_W), lambda b, i: (0, 0))],
        out_specs=pl.BlockSpec((C, DN_W), lambda b, i: (b * nt + i, 0)),
        out_shape=jax.ShapeDtypeStruct((n, DN_W), BF16),
        scratch_shapes=[pltpu.VMEM((DN_HEADS, DN_HD, DN_HD), F32), pltpu.VMEM((C, DN_W), F32)],
        compiler_params=_cparams("parallel", "arbitrary"),
        name="deltanet",
    )(proj, proj, proj, proj, proj, proj, proj, ba, conv_w, alog_row, dtb_row, norm_g,
      jnp.asarray(head_lanes(0), BF16), jnp.asarray(head_lanes(DN_HEADS), BF16))


def _merge_kernel(x_ref, yp_ref, ym_ref, yd_ref, g0_ref, g1_ref, g2_ref, wp_ref, wm_ref, wd_ref, wo_ref, o_ref):
    merged = (g0_ref[...].astype(F32) * _dot(yp_ref[...], wp_ref[...])
              + g1_ref[...].astype(F32) * _dot(ym_ref[...], wm_ref[...])
              + g2_ref[...].astype(F32) * _dot(yd_ref[...], wd_ref[...]))
    o_ref[...] = x_ref[...] + _dot(merged.astype(BF16), wo_ref[...])


def _merge(x, y_pool, y_moba, y_dn, proj, w_up_pool, w_up_moba, w_up_dn, w_out, tm):
    n = x.shape[0]

    def rows(width, col=0):
        return pl.BlockSpec((tm, width), lambda i: (i, col))

    def whole(shape):
        return pl.BlockSpec(shape, lambda i: (0, 0))

    return pl.pallas_call(
        _merge_kernel,
        grid=(n // tm,),
        in_specs=[rows(D_MODEL), rows(POOL_W), rows(MOBA_AUG_W), rows(DN_W),
                  rows(D_MODEL, COL_GATE0), rows(D_MODEL, COL_GATE0 + 1), rows(D_MODEL, COL_GATE0 + 2),
                  whole((POOL_W, D_MODEL)), whole((MOBA_AUG_W, D_MODEL)), whole((DN_W, D_MODEL)),
                  whole((D_MODEL, D_MODEL))],
        out_specs=rows(D_MODEL),
        out_shape=jax.ShapeDtypeStruct((n, D_MODEL), F32),
        compiler_params=_cparams("parallel"),
        name="merge",
    )(x, y_pool, y_moba, y_dn, proj, proj, proj, w_up_pool, w_up_moba, w_up_dn, w_out)


def _memkv_kernel(mem_ref, g_ref, w_ref, o_ref):
    o_ref[...] = _dot(_rms(mem_ref[...], g_ref[...]).astype(BF16), w_ref[...]).astype(BF16)


def _memkv(mem2d, g, wkv):
    m = mem2d.shape[0]
    tm = 256
    return pl.pallas_call(
        _memkv_kernel,
        grid=(m // tm,),
        in_specs=[pl.BlockSpec((tm, D_MODEL), lambda i: (i, 0)),
                  pl.BlockSpec((1, D_MODEL), lambda i: (0, 0)),
                  pl.BlockSpec((D_MODEL, 2 * XA_W), lambda i: (0, 0))],
        out_specs=pl.BlockSpec((tm, 2 * XA_W), lambda i: (i, 0)),
        out_shape=jax.ShapeDtypeStruct((m, 2 * XA_W), BF16),
        compiler_params=_cparams("parallel"),
        name="memkv",
    )(mem2d, g, wkv)


def _xattn_kernel(x_ref, g_ref, wq_ref, kv_ref, wo_ref, o_ref, *maybe_rows_ref):
    x = x_ref[...]
    q = _dot(_rms(x, g_ref[...]).astype(BF16), wq_ref[...]).astype(BF16)
    scale = XA_HD ** -0.5
    outs = []
    for h in range(XA_HEADS):
        sl = slice(h * XA_HD, (h + 1) * XA_HD)
        k = kv_ref[:, h * XA_HD:(h + 1) * XA_HD]
        v = kv_ref[:, XA_W + h * XA_HD:XA_W + (h + 1) * XA_HD]
        s = _dot_nt(q[:, sl], k) * scale
        s = s - jnp.max(s, axis=1, keepdims=True)
        p = jnp.exp(s)
        p = p / jnp.sum(p, axis=1, keepdims=True)
        outs.append(_dot(p.astype(BF16), v).astype(BF16))
    o = jnp.concatenate(outs, axis=1)
    out = x + _dot(o, wo_ref[...])
    o_ref[...] = out
    for rows_ref in maybe_rows_ref:
        for s in range(ROW_TILE):
            rows_ref[pl.ds(s, out.shape[0], stride=ROW_TILE), :] = out[:, s * 128:(s + 1) * 128]


def _xattn(x, g, wq, kv, wo, seq, mem_len, tm, with_row_tiles):
    n = x.shape[0]
    tiles_per_seq = seq // tm
    out_specs = [pl.BlockSpec((tm, D_MODEL), lambda i: (i, 0))]
    out_shape = [jax.ShapeDtypeStruct((n, D_MODEL), F32)]
    if with_row_tiles:
        out_specs.append(pl.BlockSpec((tm * ROW_TILE, 128), lambda i: (i, 0)))
        out_shape.append(jax.ShapeDtypeStruct((n * ROW_TILE, 128), F32))
    return pl.pallas_call(
        _xattn_kernel,
        grid=(n // tm,),
        in_specs=[pl.BlockSpec((tm, D_MODEL), lambda i: (i, 0)),
                  pl.BlockSpec((1, D_MODEL), lambda i: (0, 0)),
                  pl.BlockSpec((D_MODEL, XA_W), lambda i: (0, 0)),
                  pl.BlockSpec((mem_len, 2 * XA_W), lambda i: (i // tiles_per_seq, 0)),
                  pl.BlockSpec((XA_W, D_MODEL), lambda i: (0, 0))],
        out_specs=out_specs,
        out_shape=out_shape,
        compiler_params=_cparams("parallel"),
        name="xattn",
    )(x, g, wq, kv, wo)


def _swiglu_chunk(h, wg_ref, wu_ref, wd_ref):
    a = _silu(_dot(h, wg_ref[...])) * _dot(h, wu_ref[...])
    return _dot(a.astype(BF16), wd_ref[...])


def _ffn_kernel(x_ref, g_ref, wg_ref, wu_ref, wd_ref, o_ref, h_ref, acc_ref):
    j = pl.program_id(1)

    @pl.when(j == 0)
    def _():
        h_ref[...] = _rms(x_ref[...], g_ref[...]).astype(BF16)
        acc_ref[...] = x_ref[...]

    acc_ref[...] += _swiglu_chunk(h_ref[...], wg_ref, wu_ref, wd_ref)

    @pl.when(j == pl.num_programs(1) - 1)
    def _():
        o_ref[...] = acc_ref[...]


def _ffn(x, g, w_gate_up, w_down, tm, tf):
    n = x.shape[0]
    ff = w_down.shape[0]
    nf = ff // tf
    return pl.pallas_call(
        _ffn_kernel,
        grid=(n // tm, nf),
        in_specs=[pl.BlockSpec((tm, D_MODEL), lambda i, j: (i, 0)),
                  pl.BlockSpec((1, D_MODEL), lambda i, j: (0, 0)),
                  pl.BlockSpec((D_MODEL, tf), lambda i, j: (0, j)),
                  pl.BlockSpec((D_MODEL, tf), lambda i, j: (0, nf + j)),
                  pl.BlockSpec((tf, D_MODEL), lambda i, j: (j, 0))],
        out_specs=pl.BlockSpec((tm, D_MODEL), lambda i, j: (i, 0)),
        out_shape=jax.ShapeDtypeStruct((n, D_MODEL), F32),
        scratch_shapes=[pltpu.VMEM((tm, D_MODEL), BF16), pltpu.VMEM((tm, D_MODEL), F32)],
        compiler_params=_cparams("parallel", "arbitrary"),
        name="ffn",
    )(x, g, w_gate_up, w_gate_up, w_down)


def _route_kernel(x_ref, g_ref, r_ref, o_ref, *, tm):
    lane = lax.broadcasted_iota(jnp.int32, (tm, 128), 1).astype(F32)
    h = _rms(x_ref[...], g_ref[...]).astype(BF16)
    logits = jnp.where(lane < float(N_EXPERTS), _dot(h, r_ref[...]), -jnp.inf)
    m1 = jnp.max(logits, axis=1, keepdims=True)
    i1 = jnp.min(jnp.where(logits == m1, lane, 128.0), axis=1, keepdims=True)
    rest = jnp.where(lane == i1, -jnp.inf, logits)
    m2 = jnp.max(rest, axis=1, keepdims=True)
    i2 = jnp.min(jnp.where(rest == m2, lane, 128.0), axis=1, keepdims=True)
    e2 = jnp.exp(m2 - m1)
    o_ref[...] = (jnp.where(lane == 0.0, i1, 0.0) + jnp.where(lane == 1.0, i2, 0.0)
                  + jnp.where(lane == 2.0, 1.0 / (1.0 + e2), 0.0) + jnp.where(lane == 3.0, e2 / (1.0 + e2), 0.0))


def _route(x, g, router, tm):
    n = x.shape[0]
    return pl.pallas_call(
        functools.partial(_route_kernel, tm=tm),
        grid=(n // tm,),
        in_specs=[pl.BlockSpec((tm, D_MODEL), lambda i: (i, 0)),
                  pl.BlockSpec((1, D_MODEL), lambda i: (0, 0)),
                  pl.BlockSpec((D_MODEL, 128), lambda i: (0, 0))],
        out_specs=pl.BlockSpec((tm, 128), lambda i: (i, 0)),
        out_shape=jax.ShapeDtypeStruct((n, 128), F32),
        compiler_params=_cparams("parallel"),
        name="moe_route",
    )(x, g, router)


def _row_gather(idx_smem, src_hbm, dst, count, sem):
    def row_copy(r):
        s0 = pl.multiple_of(idx_smem[r] * ROW_TILE, ROW_TILE)
        d0 = pl.multiple_of(r * ROW_TILE, ROW_TILE)
        return pltpu.make_async_copy(src_hbm.at[pl.ds(s0, ROW_TILE), :], dst.at[pl.ds(d0, ROW_TILE), :], sem)

    def start():
        @pl.loop(0, count, step=GATHER_UNROLL)
        def _(r0):
            for u in range(GATHER_UNROLL):
                row_copy(r0 + u).start()

    def wait():
        @pl.loop(0, count, step=GATHER_UNROLL)
        def _(r0):
            for u in range(GATHER_UNROLL):
                row_copy(r0 + u).wait()

    return start, wait


def _gather_ahead(step, n_steps, idx_now, idx_next, idx_smem, src_hbm, buf, count, sems):
    slot = step % 2

    def begin(idx_vmem, sl):
        stage = pltpu.make_async_copy(idx_vmem, idx_smem[sl], sems.at[2])
        stage.start()
        stage.wait()
        _row_gather(idx_smem[sl], src_hbm, buf.at[sl], count, sems.at[sl])[0]()

    @pl.when(step == 0)
    def _():
        begin(idx_now, 0)

    for sl in range(2):
        @pl.when(slot == sl)
        def _(sl=sl):
            @pl.when(step + 1 < n_steps)
            def _():
                begin(idx_next, 1 - sl)

            _row_gather(idx_smem[sl], src_hbm, buf.at[sl], count, sems.at[sl])[1]()

    return slot


def _rms_row_tiles(v, g8, rows):
    v3 = v.reshape(rows, ROW_TILE, 128)
    ms = jnp.sum(jnp.sum(v3 * v3, axis=2, keepdims=True), axis=1, keepdims=True) * (1.0 / D_MODEL)
    return (v3 * lax.rsqrt(ms + RMS_EPS) * g8[None]).reshape(rows * ROW_TILE, 128)


def _expert_kernel(te_ref, nu_ref, x_hbm, src_ref, src_next_ref, w_ref, g8_ref, wg_ref, wu_ref, wd_ref, o_ref,
                   idx0_ref, idx1_ref, xbuf_ref, x16_ref, sems, *, tm):
    t = pl.program_id(0)
    del te_ref

    @pl.when(t >= nu_ref[0])
    def _():
        o_ref[...] = jnp.zeros_like(o_ref)

    @pl.when(t < nu_ref[0])
    def _():
        slot = _gather_ahead(t, nu_ref[0], src_ref.at[0, 0], src_next_ref.at[0, 0], (idx0_ref, idx1_ref), x_hbm, xbuf_ref,
                             tm, sems)
        xb = xbuf_ref.at[slot]
        xb[...] = _rms_row_tiles(xb[...], g8_ref[...], tm)
        for s in range(ROW_TILE):
            x16_ref[:, s * 128:(s + 1) * 128] = xb[pl.ds(s, tm, stride=ROW_TILE), :].astype(BF16)
        y = _swiglu_chunk(x16_ref[...], wg_ref.at[0], wu_ref.at[0], wd_ref.at[0]) * w_ref[...]
        for s in range(ROW_TILE):
            o_ref[pl.ds(s, tm, stride=ROW_TILE), :] = y[:, s * 128:(s + 1) * 128]


def _experts(x_rows, src, w_sorted, tile_expert, n_used, g8, w_gate_up, w_down, tm):
    nt = src.shape[0]
    ff = w_down.shape[1]
    grid_spec = pltpu.PrefetchScalarGridSpec(
        num_scalar_prefetch=2,
        grid=(nt,),
        in_specs=[pl.BlockSpec(memory_space=pl.ANY),
                  pl.BlockSpec((1, 1, tm), lambda t, te, nu: (t, 0, 0)),
                  pl.BlockSpec((1, 1, tm), lambda t, te, nu: (jnp.minimum(t + 1, nt - 1), 0, 0)),
                  pl.BlockSpec((tm, 1), lambda t, te, nu: (t, 0)),
                  pl.BlockSpec((ROW_TILE, 128), lambda t, te, nu: (0, 0)),
                  pl.BlockSpec((1, D_MODEL, ff), lambda t, te, nu: (te[t], 0, 0)),
                  pl.BlockSpec((1, D_MODEL, ff), lambda t, te, nu: (te[t], 0, 1)),
                  pl.BlockSpec((1, ff, D_MODEL), lambda t, te, nu: (te[t], 0, 0))],
        out_specs=pl.BlockSpec((tm * ROW_TILE, 128), lambda t, te, nu: (t, 0)),
        scratch_shapes=[pltpu.SMEM((tm,), jnp.int32), pltpu.SMEM((tm,), jnp.int32),
                        pltpu.VMEM((2, tm * ROW_TILE, 128), F32),
                        pltpu.VMEM((tm, D_MODEL), BF16), pltpu.SemaphoreType.DMA((3,))])
    return pl.pallas_call(
        functools.partial(_expert_kernel, tm=tm),
        grid_spec=grid_spec,
        out_shape=jax.ShapeDtypeStruct((nt * tm * ROW_TILE, 128), F32),
        compiler_params=_cparams("arbitrary"),
        name="moe_experts",
    )(tile_expert, n_used, x_rows, src, src, w_sorted, g8, w_gate_up, w_gate_up, w_down)


def _combine_kernel(x_ref, y_hbm, dest_ref, dest_next_ref, fg_ref, o_ref, idx0_ref, idx1_ref, ybuf_ref, sems, *, tt):
    slot = _gather_ahead(pl.program_id(0), pl.num_programs(0), dest_ref.at[0, 0], dest_next_ref.at[0, 0],
                         (idx0_ref, idx1_ref),
                         y_hbm, ybuf_ref, MOE_TOPK * tt, sems)
    yb = ybuf_ref.at[slot]
    n8 = tt * ROW_TILE
    for s in range(ROW_TILE):
        sl = slice(s * 128, (s + 1) * 128)
        o_ref[:, sl] = (x_ref[:, sl] + yb[pl.ds(s, tt, stride=ROW_TILE), :]
                        + yb[pl.ds(n8 + s, tt, stride=ROW_TILE), :])
    o_ref[...] = _rms(o_ref[...], fg_ref[...])


def _combine_final(x, y_rows, dest, fg, tt):
    nt = dest.shape[0]
    return pl.pallas_call(
        functools.partial(_combine_kernel, tt=tt),
        grid=(nt,),
        in_specs=[pl.BlockSpec((tt, D_MODEL), lambda i: (i, 0)),
                  pl.BlockSpec(memory_space=pl.ANY),
                  pl.BlockSpec((1, 1, MOE_TOPK * tt), lambda i: (i, 0, 0)),
                  pl.BlockSpec((1, 1, MOE_TOPK * tt), lambda i: (jnp.minimum(i + 1, nt - 1), 0, 0)),
                  pl.BlockSpec((1, D_MODEL), lambda i: (0, 0))],
        out_specs=pl.BlockSpec((tt, D_MODEL), lambda i: (i, 0)),
        out_shape=jax.ShapeDtypeStruct(x.shape, F32),
        scratch_shapes=[pltpu.SMEM((MOE_TOPK * tt,), jnp.int32), pltpu.SMEM((MOE_TOPK * tt,), jnp.int32),
                        pltpu.VMEM((2, MOE_TOPK * tt * ROW_TILE, 128), F32), pltpu.SemaphoreType.DMA((3,))],
        compiler_params=_cparams("arbitrary"),
        name="moe_combine",
    )(x, y_rows, dest, dest, fg)


def _moe_sparse_final(xf, x_rows, g, router, w_gate_up, w_down, final_g, tm, tt):
    n = xf.shape[0]
    nk = MOE_TOPK * n
    route = _route(xf, g, router, tm)
    e_flat = route[:, :MOE_TOPK].astype(jnp.int32).reshape(nk)
    w_flat = route[:, MOE_TOPK:2 * MOE_TOPK].reshape(nk)
    onehot = (e_flat[:, None] == jnp.arange(N_EXPERTS, dtype=jnp.int32)[None, :]).astype(jnp.int32)
    csum = jnp.cumsum(onehot, axis=0)
    cnt = csum[-1]
    rank = jnp.take_along_axis(csum, e_flat[:, None], axis=1)[:, 0] - 1
    padded = ((cnt + tm - 1) // tm) * tm
    ends = jnp.cumsum(padded)
    off = ends - padded
    start = jnp.cumsum(cnt) - cnt
    dest = off[e_flat] + rank
    nt = nk // tm + N_EXPERTS
    slot = jnp.arange(nt * tm, dtype=jnp.int32)
    e_slot = jnp.minimum(jnp.searchsorted(ends, slot, side="right"), N_EXPERTS - 1).astype(jnp.int32)
    loc = slot - off[e_slot]
    valid = loc < cnt[e_slot]
    order = jnp.argsort(e_flat, stable=True).astype(jnp.int32)
    pair = order[jnp.clip(start[e_slot] + loc, 0, nk - 1)]
    src = jnp.where(valid, pair // MOE_TOPK, 0).reshape(nt, 1, tm)
    w_sorted = jnp.where(valid, w_flat[pair], 0.0).reshape(nt * tm, 1)
    tile_expert = e_slot[::tm]
    n_used = (ends[-1:] // tm).astype(jnp.int32)

    y_rows = _experts(x_rows, src, w_sorted, tile_expert, n_used, g.reshape(ROW_TILE, 128), w_gate_up, w_down, tm)
    dest = dest.reshape(n // tt, tt, MOE_TOPK).transpose(0, 2, 1).reshape(n // tt, 1, MOE_TOPK * tt)
    return _combine_final(xf, y_rows, dest, final_g[None, :], tt)


def _blockdiag(pool_w):
    g, c, _ = pool_w.shape
    out = jnp.zeros((g * c, g * c), pool_w.dtype)
    for gi in range(g):
        out = out.at[gi * c:(gi + 1) * c, gi * c:(gi + 1) * c].set(pool_w[gi])
    return out


def _lane_row(v, offset):
    return jnp.zeros((1, BA_LANES), F32).at[0, offset:offset + v.shape[0]].set(v.astype(F32))


def kernel(x, mem, positions, mix_norm_g, w_in, pool_w, pool_scale, dn_conv_w, dn_a_log, dn_dt_bias, dn_norm_g,
           w_up_pool, w_up_moba, w_up_dn, w_out, xa_norm_g, mem_norm_g, xa_wq, xa_wkv, xa_wo, ffn_norm_g,
           dense_w_gate_up, dense_w_down, moe_router, moe_w_gate_up, moe_w_down, final_norm_g):
    batch, seq, d = x.shape
    depth = w_in.shape[0]
    mem_len = mem.shape[1]
    n = batch * seq
    nb = seq // MOBA_BLOCK
    assert d == D_MODEL and seq % MOBA_BLOCK == 0 and nb <= MOBA_HD and nb % (2 * MOBA_GROUP) == 0 and depth == 2
    tm = min(512, seq)

    xf = x.reshape(n, d)
    mem2d = mem.reshape(batch * mem_len, d)
    cos_t, sin_t = _rope_tables(positions.reshape(n, 1).astype(F32), tm)
    off_ba = POOL_W + 3 * MOBA_W + 3 * DN_W

    for layer in range(depth):
        wl = w_in[layer]
        w_main = jnp.concatenate([wl[:, :off_ba], wl[:, off_ba + 2 * DN_HEADS:]], axis=1).astype(BF16)
        w_ba = jnp.pad(wl[:, off_ba:off_ba + 2 * DN_HEADS], ((0, 0), (0, BA_LANES - 2 * DN_HEADS))).astype(BF16)
        proj, ba = _inproj(xf, mix_norm_g[layer][None, :], w_main, w_ba, min(2048, seq))

        y_pool = _pool(proj, _blockdiag(pool_w[layer]).astype(BF16), pool_scale[layer][None, :], batch, seq, tm)
        mq, mk, mv, kbar = _moba_prep(proj, cos_t, sin_t, nb)
        kbar = jnp.pad(kbar.reshape(batch, nb, MOBA_W), ((0, 0), (0, MOBA_KBAR_ROWS - nb), (0, 0)))
        y_moba = _moba(mq, mk, mv, kbar, batch, seq)
        w_moba = jnp.pad(w_up_moba[layer].reshape(MOBA_HEADS, MOBA_HD, d), ((0, 0), (0, MOBA_HD), (0, 0)))
        w_moba = w_moba.reshape(MOBA_AUG_W, d).astype(BF16)
        conv_w = dn_conv_w[layer].reshape(DN_CONV, 3, DN_W).transpose(1, 0, 2)
        y_dn = _deltanet(proj, ba, conv_w, _lane_row(dn_a_log[layer], DN_HEADS),
                         _lane_row(dn_dt_bias[layer], DN_HEADS), jnp.tile(dn_norm_g[layer], DN_HEADS)[None, :], batch, seq)
        xf = _merge(xf, y_pool, y_moba, y_dn, proj, w_up_pool[layer].astype(BF16), w_moba,
                    w_up_dn[layer].astype(BF16), w_out[layer].astype(BF16), tm)

        kv = _memkv(mem2d, mem_norm_g[layer][None, :], xa_wkv[layer].astype(BF16))
        routed = layer % 2 == 1
        xf, *x_rows = _xattn(xf, xa_norm_g[layer][None, :], xa_wq[layer].astype(BF16), kv,
                             xa_wo[layer].astype(BF16), seq, mem_len, tm, routed)

        if layer % 2 == 0:
            xf = _ffn(xf, ffn_norm_g[layer][None, :], dense_w_gate_up[layer // 2].astype(BF16),
                      dense_w_down[layer // 2].astype(BF16), tm, dense_w_down.shape[1] // 2)
        else:
            router = jnp.pad(moe_router[layer // 2], ((0, 0), (0, 128 - N_EXPERTS))).astype(BF16)
            xf = _moe_sparse_final(xf, x_rows[0], ffn_norm_g[layer][None, :], router, moe_w_gate_up[layer // 2].astype(BF16),
                                   moe_w_down[layer // 2].astype(BF16), final_norm_g, tm, min(512, seq))
    return xf.reshape(batch, seq, d)
```
